```python
import math
import jax
import jax.numpy as jnp
from jax import lax
import numpy as np

D_MODEL = 1024
BATCH = 4
SEQ = 8192
DEPTH = 2

CTX_LEN = 256
GRID_W = 64
EPS = 1e-6

DN_HEADS = 4
DN_HEAD_DIM = 128
DN_WIDTH = DN_HEADS * DN_HEAD_DIM
DN_CHUNK = 64
POOL_WINDOWS = (2, 4, 8, 16)
POOL_GROUPS = len(POOL_WINDOWS)
POOL_WIDTH = D_MODEL // 4
POOL_GROUP_DIM = POOL_WIDTH // POOL_GROUPS
SC_WIDTH = D_MODEL // 4
N_BRANCH = 3
D_FF = ((8 * D_MODEL + 3 * 256 - 1) // (3 * 256)) * 256

OFF_Z = 3 * DN_WIDTH
OFF_A = OFF_Z + DN_WIDTH
OFF_BETA = OFF_A + 2 * DN_HEADS
OFF_POOL = OFF_BETA + 2 * DN_HEADS
OFF_SC = OFF_POOL + POOL_WIDTH
OFF_GATE = OFF_SC + 3 * SC_WIDTH
N_IN = OFF_GATE + N_BRANCH * D_MODEL
IN_SPLITS = (OFF_Z, OFF_A, OFF_BETA, OFF_POOL, OFF_SC, OFF_GATE)

kernel_name = "hybrid_parallel_deltanet_pool_shortconv_dit"


def rmsnorm(x, g):
    xf = x.astype(jnp.float32)
    y = xf * lax.rsqrt(jnp.mean(xf * xf, axis=-1, keepdims=True) + EPS)
    return (y * g.astype(jnp.float32)).astype(x.dtype)


def l2norm(x):
    return x * lax.rsqrt(jnp.sum(x * x, axis=-1, keepdims=True) + EPS)


def conv3(x, w):
    xp = jnp.pad(x, ((0, 0), (1, 1), (0, 0)))
    return xp[:, :-2] * w[0] + xp[:, 1:-1] * w[1] + xp[:, 2:] * w[2]


def decay_gate(p_a, p_b, a_log, dt_bias):
    bn, l, _ = p_a.shape
    a = p_a.astype(jnp.float32).reshape(bn, l, 2, DN_HEADS)
    g = -jnp.exp(a_log.astype(jnp.float32)) * jax.nn.softplus(a + dt_bias.astype(jnp.float32))
    beta = jax.nn.sigmoid(p_b.astype(jnp.float32).reshape(bn, l, 2, DN_HEADS))
    return g, beta


def gated_delta_chunked(q, k, v, g, beta, s0):
    bn, l, h, _ = k.shape
    dv = v.shape[-1]
    n = l // DN_CHUNK

    def chunks(t):
        t = t.reshape(bn, n, DN_CHUNK, h, *t.shape[3:])
        return jnp.moveaxis(t, (1, 3), (0, 2))

    kc, vc, bc = chunks(k), chunks(v), chunks(beta)
    gc = jnp.cumsum(chunks(g), axis=-1)
    idx = jnp.arange(DN_CHUNK)
    incl = idx[:, None] >= idx[None, :]
    strict = idx[:, None] > idx[None, :]
    decay = jnp.exp(jnp.where(incl, gc[..., :, None] - gc[..., None, :], -jnp.inf))
    kb = kc * bc[..., None]
    a = jnp.einsum('nbhid,nbhjd->nbhij', kb, kc) * jnp.where(strict, decay, 0.0)
    eye = jnp.eye(DN_CHUNK, dtype=jnp.float32)
    rhs = jnp.concatenate([vc * bc[..., None], kb * jnp.exp(gc)[..., None]], axis=-1)
    sol = lax.linalg.triangular_solve(eye + a, rhs, left_side=True, lower=True, unit_diagonal=True)
    u, w = sol[..., :dv], sol[..., dv:]
    g_last = gc[..., -1]
    k_state = kc * jnp.exp(g_last[..., None] - gc)[..., None]
    with_output = q is not None
    xs = (u, w, k_state, g_last)
    if with_output:
        qc = chunks(q)
        q_dec = qc * jnp.exp(gc)[..., None]
        a_qk = jnp.einsum('nbhid,nbhjd->nbhij', qc, kc) * decay
        xs = xs + (q_dec, a_qk)

    def step(s, xs_i):
        u_i, w_i, k_i, gl_i = xs_i[:4]
        v_new = u_i - jnp.einsum('bhck,bhkv->bhcv', w_i, s)
        s_next = s * jnp.exp(gl_i)[..., None, None] + jnp.einsum('bhck,bhcv->bhkv', k_i, v_new)
        if with_output:
            qd_i, aqk_i = xs_i[4:]
            o_i = jnp.einsum('bhck,bhkv->bhcv', qd_i, s) + jnp.einsum('bhij,bhjv->bhiv', aqk_i, v_new)
            return s_next, o_i
        return s_next, None

    s_fin, o = lax.scan(step, s0, xs)
    if with_output:
        o = jnp.moveaxis(o, (0, 2), (1, 3)).reshape(bn, l, h, dv)
    return o, s_fin


def dn_bidir(q, k, v, g, beta, s0_f, s0_b):
    rev = lambda t: jnp.flip(t, axis=1)
    o_f, s_f = gated_delta_chunked(q, k, v, g[:, :, 0], beta[:, :, 0], s0_f)
    o_b, s_b = gated_delta_chunked(None if q is None else rev(q), rev(k), rev(v),
                                   rev(g[:, :, 1]), rev(beta[:, :, 1]), s0_b)
    o = None if q is None else o_f + rev(o_b)
    return o, s_f, s_b


def box_mean(x, w, axis):
    l = x.shape[axis]
    lo = w // 2
    hi = w - 1 - lo
    cs = jnp.cumsum(x, axis=axis)
    cs = jnp.concatenate([jnp.zeros_like(lax.slice_in_dim(cs, 0, 1, axis=axis)), cs], axis=axis)
    pos = jnp.arange(l)
    start = jnp.clip(pos - lo, 0, l)
    end = jnp.clip(pos + hi + 1, 0, l)
    total = jnp.take(cs, end, axis=axis) - jnp.take(cs, start, axis=axis)
    shape = [1] * x.ndim
    shape[axis] = l
    return total / (end - start).astype(x.dtype).reshape(shape)


def pool_mixer(u, pool_w, pool_scale, rows):
    bn, l, _ = u.shape
    uf = u.astype(jnp.float32)
    outs = []
    for gi, w in enumerate(POOL_WINDOWS):
        ug = uf[..., gi * POOL_GROUP_DIM:(gi + 1) * POOL_GROUP_DIM]
        if rows is None:
            m = box_mean(ug, w, 1)
        else:
            ug2 = ug.reshape(bn, rows, GRID_W, POOL_GROUP_DIM)
            m = box_mean(box_mean(ug2, w, 1), w, 2).reshape(bn, l, POOL_GROUP_DIM)
        outs.append(m - ug)
    d = jnp.stack(outs, axis=2).astype(u.dtype)
    y = jnp.einsum('blgc,gcd->blgd', d, pool_w).reshape(bn, l, POOL_WIDTH)
    return y * pool_scale


def shortconv_mixer(p, conv_w):
    xin, gate_b, gate_c = jnp.split(p, 3, axis=-1)
    return gate_b * conv3(gate_c * xin, conv_w)


def mixer(h, lp, s0_f, s0_b, rows):
    bn, l, _ = h.shape
    p = h @ lp['w_in']
    p_qkv, p_z, p_a, p_b, p_pool, p_sc, p_gate = jnp.split(p, IN_SPLITS, axis=-1)
    qkv = jax.nn.silu(conv3(p_qkv, lp['dn_conv_w'])).astype(jnp.float32)
    qkv = qkv.reshape(bn, l, 3, DN_HEADS, DN_HEAD_DIM)
    q = l2norm(qkv[:, :, 0]) * (DN_HEAD_DIM ** -0.5)
    k = l2norm(qkv[:, :, 1])
    v = qkv[:, :, 2]
    g, beta = decay_gate(p_a, p_b, lp['dn_a_log'], lp['dn_dt_bias'])
    o, s_f, s_b = dn_bidir(q, k, v, g, beta, s0_f, s0_b)
    z = p_z.astype(jnp.float32).reshape(bn, l, DN_HEADS, DN_HEAD_DIM)
    o = rmsnorm(o, lp['dn_norm_g']) * jax.nn.silu(z)
    y_a = o.reshape(bn, l, DN_WIDTH).astype(h.dtype) @ lp['w_br_a']
    y_b = pool_mixer(p_pool, lp['pool_w'], lp['pool_scale'], rows) @ lp['w_br_b']
    y_c = shortconv_mixer(p_sc, lp['sc_conv_w']) @ lp['w_br_c']
    gates = jax.nn.sigmoid(p_gate.reshape(bn, l, N_BRANCH, D_MODEL))
    y = gates[:, :, 0] * y_a + gates[:, :, 1] * y_b + gates[:, :, 2] * y_c
    return y @ lp['w_o'], s_f, s_b


def context_states(h, lp, s0):
    bn, l, _ = h.shape
    w = lp['w_in']
    kv = jax.nn.silu(conv3(h @ w[:, DN_WIDTH:OFF_Z], lp['dn_conv_w'][:, DN_WIDTH:])).astype(jnp.float32)
    kv = kv.reshape(bn, l, 2, DN_HEADS, DN_HEAD_DIM)
    k = l2norm(kv[:, :, 0])
    v = kv[:, :, 1]
    p_a, p_b = jnp.split(h @ w[:, OFF_A:OFF_POOL], 2, axis=-1)
    g, beta = decay_gate(p_a, p_b, lp['dn_a_log'], lp['dn_dt_bias'])
    _, s_f, s_b = dn_bidir(None, k, v, g, beta, s0, s0)
    return s_f, s_b


def swiglu(h, w_gu, w_down):
    gate, up = jnp.split(h @ w_gu, 2, axis=-1)
    return (jax.nn.silu(gate) * up) @ w_down


def modulate(x, norm_g, shift, scale):
    return rmsnorm(x, norm_g) * (1 + scale) + shift


def setup_inputs(seed: int = 0) -> dict:
    key = jax.random.key(seed)
    ks = jax.random.split(key, 24)
    f32 = jnp.float32
    nrm = lambda k, shape, s: jax.random.normal(k, shape, f32) * s
    L = DEPTH
    dt = jnp.exp(jax.random.uniform(ks[9], (L, 2, DN_HEADS), f32, math.log(1e-3), math.log(1e-1)))
    return {
        'x': nrm(ks[0], (BATCH, SEQ, D_MODEL), 1.0),
        'c': nrm(ks[1], (BATCH, D_MODEL), 1.0),
        'ctx': nrm(ks[2], (BATCH, CTX_LEN, D_MODEL), 1.0),
        'c_ctx': nrm(ks[3], (D_MODEL,), 1.0),
        'w_ada': nrm(ks[4], (L, D_MODEL, 6 * D_MODEL), 0.5 * D_MODEL ** -0.5),
        'b_ada': nrm(ks[5], (L, 6 * D_MODEL), 0.01),
        'norm1_g': 1.0 + nrm(ks[6], (L, D_MODEL), 0.1),
        'norm2_g': 1.0 + nrm(ks[7], (L, D_MODEL), 0.1),
        'w_in': nrm(ks[8], (L, D_MODEL, N_IN), D_MODEL ** -0.5),
        'dn_conv_w': nrm(ks[10], (L, 3, 3 * DN_WIDTH), 3 ** -0.5),
        'dn_a_log': jnp.log(jax.random.uniform(ks[11], (L, 2, DN_HEADS), f32, 1.0, 16.0)),
        'dn_dt_bias': dt + jnp.log(-jnp.expm1(-dt)),
        'dn_norm_g': 1.0 + nrm(ks[12], (L, DN_HEAD_DIM), 0.1),
        'pool_w': nrm(ks[13], (L, POOL_GROUPS, POOL_GROUP_DIM, POOL_GROUP_DIM), POOL_GROUP_DIM ** -0.5),
        'pool_scale': 1.0 + nrm(ks[14], (L, POOL_WIDTH), 0.1),
        'sc_conv_w': nrm(ks[15], (L, 3, SC_WIDTH), 3 ** -0.5),
        'w_br_a': nrm(ks[16], (L, DN_WIDTH, D_MODEL), DN_WIDTH ** -0.5),
        'w_br_b': nrm(ks[17], (L, POOL_WIDTH, D_MODEL), POOL_WIDTH ** -0.5),
        'w_br_c': nrm(ks[18], (L, SC_WIDTH, D_MODEL), SC_WIDTH ** -0.5),
        'w_o': nrm(ks[19], (L, D_MODEL, D_MODEL), D_MODEL ** -0.5),
        'w_gu': nrm(ks[20], (L, D_MODEL, 2 * D_FF), D_MODEL ** -0.5),
        'w_down': nrm(ks[21], (L, D_FF, D_MODEL), D_FF ** -0.5),
        'final_norm_g': 1.0 + nrm(ks[22], (D_MODEL,), 0.1),
    }


def reference(x, c, ctx, c_ctx, w_ada, b_ada, norm1_g, norm2_g, w_in, dn_conv_w, dn_a_log,
              dn_dt_bias, dn_norm_g, pool_w, pool_scale, sc_conv_w, w_br_a, w_br_b, w_br_c,
              w_o, w_gu, w_down, final_norm_g):
    bn = x.shape[0]
    rows = x.shape[1] // GRID_W
    s0 = jnp.zeros((bn, DN_HEADS, DN_HEAD_DIM, DN_HEAD_DIM), jnp.float32)
    for l in range(DEPTH):
        lp = {'w_in': w_in[l], 'dn_conv_w': dn_conv_w[l], 'dn_a_log': dn_a_log[l],
              'dn_dt_bias': dn_dt_bias[l], 'dn_norm_g': dn_norm_g[l], 'pool_w': pool_w[l],
              'pool_scale': pool_scale[l], 'sc_conv_w': sc_conv_w[l], 'w_br_a': w_br_a[l],
              'w_br_b': w_br_b[l], 'w_br_c': w_br_c[l], 'w_o': w_o[l]}
        mod = jax.nn.silu(c) @ w_ada[l] + b_ada[l]
        sh1, sc1, g1, sh2, sc2, g2 = jnp.split(mod[:, None, :], 6, axis=-1)
        mod_c = jax.nn.silu(c_ctx) @ w_ada[l] + b_ada[l]
        sh1c, sc1c, g1c, sh2c, sc2c, g2c = jnp.split(mod_c, 6)
        hc = modulate(ctx, norm1_g[l], sh1c, sc1c)
        if l == DEPTH - 1:
            s_f, s_b = context_states(hc, lp, s0)
        else:
            mix_c, s_f, s_b = mixer(hc, lp, s0, s0, None)
            ctx = ctx + g1c * mix_c
            ctx = ctx + g2c * swiglu(modulate(ctx, norm2_g[l], sh2c, sc2c), w_gu[l], w_down[l])
        h = modulate(x, norm1_g[l], sh1, sc1)
        mix, _, _ = mixer(h, lp, s_f, s_b, rows)
        x = x + g1 * mix
        x = x + g2 * swiglu(modulate(x, norm2_g[l], sh2, sc2), w_gu[l], w_down[l])
    return rmsnorm(x, final_norm_g)
```

```python
import functools
import math

import jax
import jax.numpy as jnp
from jax import lax
from jax.experimental import pallas as pl
from jax.experimental.pallas import tpu as pltpu

F32 = jnp.float32
BF16 = jnp.bfloat16

EPS = 1e-6
GRID_W = 64
DN_HEADS = 4
DN_HEAD_DIM = 128
DN_WIDTH = DN_HEADS * DN_HEAD_DIM
POOL_WINDOWS = (2, 4, 8, 16)
POOL_GROUP_DIM = 64
POOL_WIDTH = POOL_GROUP_DIM * len(POOL_WINDOWS)
SC_WIDTH = 256
N_BRANCH = 3

LANES = 128
SUBLANES = 8
SCAN_CHUNK = 256
INV_BASE_BLOCK = 8
TOKEN_TILE = 256
POOL_TILE_ROWS = 8
VMEM_LIMIT = 56 * 1024 * 1024
NEG_BIG = -1e30


def _dot(a, b):
    return jnp.dot(a.astype(BF16), b.astype(BF16), preferred_element_type=F32)


def _dot_nt(a, b):
    return lax.dot_general(a.astype(BF16), b.astype(BF16), (((1,), (1,)), ((), ())),
                           preferred_element_type=F32)


def _dot_tn(a, b):
    return lax.dot_general(a.astype(BF16), b.astype(BF16), (((0,), (0,)), ((), ())),
                           preferred_element_type=F32)


def _silu(x):
    return x * jax.nn.sigmoid(x)


def _rms(x):
    return x * lax.rsqrt(jnp.mean(x * x, axis=-1, keepdims=True) + EPS)


def _modulate(x, g, shift, scale):
    return (_rms(x) * g) * (1.0 + scale) + shift


def _shift_rows(x, prev_row, next_row):
    n = x.shape[0]
    r = lax.broadcasted_iota(jnp.int32, x.shape, 0)
    dn = jnp.where(r == 0, prev_row, pltpu.roll(x, 1, axis=0))
    up = jnp.where(r == n - 1, next_row, pltpu.roll(x, n - 1, axis=0))
    return dn, up


def _const_spec(shape):
    nd = len(shape)
    return pl.BlockSpec(shape, lambda *_: (0,) * nd, pipeline_mode=pl.Buffered(1))


def _params(sem):
    return pltpu.CompilerParams(dimension_semantics=sem, vmem_limit_bytes=VMEM_LIMIT)


def _ada_kernel(c_ref, w_ref, b_ref, o_ref):
    a = _silu(c_ref[...])
    o_ref[0] = jnp.dot(a, w_ref[0], preferred_element_type=F32,
                       precision=lax.Precision.HIGHEST) + b_ref[0]


def _ada_call(cs, w_ada, b_ada):
    nl, d, n6 = w_ada.shape
    rows = cs.shape[0]
    tn = n6 // 4
    return pl.pallas_call(
        _ada_kernel,
        grid=(nl, n6 // tn),
        in_specs=[pl.BlockSpec((rows, d), lambda l, j: (0, 0)),
                  pl.BlockSpec((1, d, tn), lambda l, j: (l, 0, j)),
                  pl.BlockSpec((1, 1, tn), lambda l, j: (l, 0, j))],
        out_specs=pl.BlockSpec((1, rows, tn), lambda l, j: (l, 0, j)),
        out_shape=jax.ShapeDtypeStruct((nl, rows, n6), F32),
        compiler_params=_params(("parallel", "parallel")),
        name="ada",
    )(cs, w_ada, b_ada.reshape(nl, 1, n6))


def _proj_kernel(x_ref, xp_ref, xn_ref, sh_ref, sc_ref, g_ref, wqkv_ref, wab_ref, wpool_ref,
                 cw_ref, alog_ref, dtb_ref, q_ref, k_ref, v_ref, gb_ref, pin_ref):
    i = pl.program_id(1)
    last = pl.num_programs(1) - 1
    g, sh, sc = g_ref[...], sh_ref[0], sc_ref[0]
    hb = _modulate(x_ref[0], g, sh, sc).astype(BF16)
    hp = _modulate(xp_ref[0], g, sh, sc).astype(BF16)
    hn = _modulate(xn_ref[0], g, sh, sc).astype(BF16)
    p = _dot(hb, wqkv_ref[...])
    pp = _dot(hp, wqkv_ref[...])[SUBLANES - 1:SUBLANES] * jnp.where(i > 0, 1.0, 0.0)
    pn = _dot(hn, wqkv_ref[...])[0:1] * jnp.where(i < last, 1.0, 0.0)
    dn, up = _shift_rows(p, pp, pn)
    a = _silu(dn * cw_ref[0:1] + p * cw_ref[1:2] + up * cw_ref[2:3])
    for h in range(DN_HEADS):
        lo = h * DN_HEAD_DIM
        qh = a[:, lo:lo + DN_HEAD_DIM]
        kh = a[:, DN_WIDTH + lo:DN_WIDTH + lo + DN_HEAD_DIM]
        q_ref[0, :, lo:lo + DN_HEAD_DIM] = (
            qh * lax.rsqrt(jnp.sum(qh * qh, axis=-1, keepdims=True) + EPS) * (DN_HEAD_DIM ** -0.5))
        k_ref[0, :, lo:lo + DN_HEAD_DIM] = (
            kh * lax.rsqrt(jnp.sum(kh * kh, axis=-1, keepdims=True) + EPS))
    v_ref[0] = a[:, 2 * DN_WIDTH:]
    pab = _dot(hb, wab_ref[...])
    z = pab + dtb_ref[...]
    softplus = jnp.maximum(z, 0.0) + jnp.log(1.0 + jnp.exp(-jnp.abs(z)))
    gdec = -jnp.exp(alog_ref[...]) * softplus
    lane = lax.broadcasted_iota(jnp.int32, pab.shape, 1)
    gb_ref[0] = jnp.where(lane < 2 * DN_HEADS, gdec,
                          jnp.where(lane < 4 * DN_HEADS, jax.nn.sigmoid(pab), 0.0))
    pin_ref[0] = _dot(hb, wpool_ref[...])


def _proj_call(x, shift, scale, norm_g, w_qkv, w_ab, w_pool, conv_w, alog, dtb):
    b, s, d = x.shape
    tm = min(TOKEN_TILE, s)
    nt = s // tm
    r8 = tm // SUBLANES
    nb8 = s // SUBLANES
    tok = lambda n: pl.BlockSpec((1, tm, n), lambda bi, i: (bi, i, 0))
    vec = pl.BlockSpec((1, 1, d), lambda bi, i: (bi, 0, 0))
    return pl.pallas_call(
        _proj_kernel,
        grid=(b, nt),
        in_specs=[tok(d),
                  pl.BlockSpec((1, SUBLANES, d), lambda bi, i: (bi, jnp.maximum(i * r8 - 1, 0), 0)),
                  pl.BlockSpec((1, SUBLANES, d), lambda bi, i: (bi, jnp.minimum((i + 1) * r8, nb8 - 1), 0)),
                  vec, vec, _const_spec((1, d)),
                  _const_spec(w_qkv.shape), _const_spec(w_ab.shape), _const_spec(w_pool.shape),
                  _const_spec(conv_w.shape), _const_spec(alog.shape), _const_spec(dtb.shape)],
        out_specs=[tok(DN_WIDTH), tok(DN_WIDTH), tok(DN_WIDTH), tok(LANES), tok(POOL_WIDTH)],
        out_shape=[jax.ShapeDtypeStruct((b, s, DN_WIDTH), F32)] * 3
        + [jax.ShapeDtypeStruct((b, s, LANES), F32), jax.ShapeDtypeStruct((b, s, POOL_WIDTH), F32)],
        compiler_params=_params(("parallel", "parallel")),
        name="proj_in",
    )(x, x, x, shift, scale, norm_g, w_qkv, w_ab, w_pool, conv_w, alog, dtb)


def _prefix_sum_rows(x):
    n = x.shape[0]
    r = lax.broadcasted_iota(jnp.int32, x.shape, 0)
    s = 1
    while s < n:
        x = x + jnp.where(r >= s, pltpu.roll(x, s, axis=0), 0.0)
        s *= 2
    return x


def _unit_tri_inverse(a):
    n = a.shape[0]
    row = lax.broadcasted_iota(jnp.int32, (n, n), 0)
    col = lax.broadcasted_iota(jnp.int32, (n, n), 1)
    eye = (row == col).astype(F32)

    def in_block(b):
        sh = int(math.log2(b))
        return jnp.right_shift(row, sh) == jnp.right_shift(col, sh)

    b = INV_BASE_BLOCK
    d = jnp.where(in_block(b), a, 0.0)
    t = eye - d
    x = d
    for _ in range(int(math.log2(b)) - 1):
        x = _dot(x, x)
        t = t + _dot(t, x)
    inner = d
    while b < n:
        outer = jnp.where(in_block(2 * b), a, 0.0)
        t = t - _dot(_dot(t, outer - inner), t)
        inner = outer
        b *= 2
    return t, eye


def _scan_kernel(qf_ref, kf_ref, vf_ref, gf_ref, qb_ref, kb_ref, vb_ref, gbk_ref, s0_ref,
                 of_ref, ob_ref, s_ref):
    c = qf_ref.shape[1]
    i = pl.program_id(1)

    @pl.when(i == 0)
    def _():
        s_ref[...] = s0_ref[...]

    row = lax.broadcasted_iota(jnp.int32, (c, c), 0)
    col = lax.broadcasted_iota(jnp.int32, (c, c), 1)
    dirs = ((qf_ref, kf_ref, vf_ref, gf_ref, of_ref), (qb_ref, kb_ref, vb_ref, gbk_ref, ob_ref))
    for d, (q_ref, k_ref, v_ref, gb_ref, o_ref) in enumerate(dirs):
        gb = gb_ref[0]
        cs = _prefix_sum_rows(gb)
        tot = cs[c - 1:c, :]
        if d == 1:
            cs = tot - cs + gb
        cs_t = cs.T
        incl = (row >= col) if d == 0 else (row <= col)
        strict = (row > col) if d == 0 else (row < col)
        for h in range(DN_HEADS):
            gi = d * DN_HEADS + h
            bi = 2 * DN_HEADS + gi
            lo = h * DN_HEAD_DIM
            gc = cs[:, gi:gi + 1]
            beta = gb[:, bi:bi + 1]
            dec = jnp.exp(jnp.where(incl, gc - cs_t[gi:gi + 1, :], NEG_BIG))
            qh = q_ref[0, :, lo:lo + DN_HEAD_DIM]
            kh = k_ref[0, :, lo:lo + DN_HEAD_DIM]
            vh = v_ref[0, :, lo:lo + DN_HEAD_DIM]
            kbeta = kh * beta
            a = _dot_nt(kbeta, kh) * jnp.where(strict, dec, 0.0)
            t, eye = _unit_tri_inverse(a)
            egc = jnp.exp(gc)
            rhs = jnp.concatenate([vh * beta, kbeta * egc], axis=1)
            sol = rhs + _dot(t - eye, rhs)
            u, w = sol[:, :DN_HEAD_DIM], sol[:, DN_HEAD_DIM:]
            s = s_ref[0, d, h]
            v_new = u - _dot(w, s)
            aqk = _dot_nt(qh, kh) * dec
            o_ref[0, :, lo:lo + DN_HEAD_DIM] = _dot(qh * egc, s) + _dot(aqk, v_new)
            gl = tot[:, gi:gi + 1]
            s_ref[0, d, h] = s * jnp.exp(gl) + _dot_tn(kh * jnp.exp(gl - gc), v_new)


def _scan_call(q, k, v, gb, s0):
    b, s, _ = q.shape
    c = min(SCAN_CHUNK, s)
    nc = s // c
    fwd = lambda n: pl.BlockSpec((1, c, n), lambda bi, i: (bi, i, 0))
    bwd = lambda n: pl.BlockSpec((1, c, n), lambda bi, i: (bi, nc - 1 - i, 0))
    st = pl.BlockSpec((1, 2, DN_HEADS, DN_HEAD_DIM, DN_HEAD_DIM), lambda bi, i: (bi, 0, 0, 0, 0))
    return pl.pallas_call(
        _scan_kernel,
        grid=(b, nc),
        in_specs=[fwd(DN_WIDTH), fwd(DN_WIDTH), fwd(DN_WIDTH), fwd(LANES),
                  bwd(DN_WIDTH), bwd(DN_WIDTH), bwd(DN_WIDTH), bwd(LANES), st],
        out_specs=[fwd(DN_WIDTH), bwd(DN_WIDTH), st],
        out_shape=[jax.ShapeDtypeStruct((b, s, DN_WIDTH), F32)] * 2
        + [jax.ShapeDtypeStruct(s0.shape, F32)],
        compiler_params=_params(("parallel", "arbitrary")),
        name="scan",
    )(q, k, v, gb, q, k, v, gb, s0)


def _pool_kernel(x_ref, o_ref, pad_ref, *, rows, width, tile):
    s = rows * width
    halo = SUBLANES * width
    ext_n = tile + 2 * halo
    wshift = int(math.log2(width))
    for slab in range(POOL_WIDTH // LANES):
        w_small, w_large = POOL_WINDOWS[2 * slab], POOL_WINDOWS[2 * slab + 1]
        lanes = slice(slab * LANES, (slab + 1) * LANES)
        pad_ref[0:halo, :] = jnp.zeros((halo, LANES), F32)
        pad_ref[halo + s:, :] = jnp.zeros((halo, LANES), F32)
        pad_ref[halo:halo + s, :] = x_ref[0, :, lanes]

        def offsets(w):
            return range(-(w // 2), w - (w // 2))

        def body(ti, carry):
            start = pl.multiple_of(ti * tile, SUBLANES)
            ext = pad_ref[pl.ds(start, ext_n), :]
            centre = ext[halo:halo + tile]
            t = start + lax.broadcasted_iota(jnp.int32, (tile, LANES), 0)
            lane = lax.broadcasted_iota(jnp.int32, (tile, LANES), 1)
            grow = jnp.right_shift(t, wshift)
            gcol = jnp.bitwise_and(t, width - 1)
            small_lane = lane < POOL_GROUP_DIM

            def count(pos, w, n):
                lo = w // 2
                hi = w - 1 - lo
                return (jnp.minimum(pos + hi + 1, n) - jnp.maximum(pos - lo, 0)).astype(F32)

            acc_s = jnp.zeros((tile, LANES), F32)
            acc_l = jnp.zeros((tile, LANES), F32)
            for r in offsets(w_large):
                off = r * width
                if off % SUBLANES == 0:
                    term = ext[halo + off:halo + off + tile]
                else:
                    term = pltpu.roll(ext, (-off) % ext_n, axis=0)[halo:halo + tile]
                acc_l = acc_l + term
                if r in offsets(w_small):
                    acc_s = acc_s + term
            m1 = jnp.where(small_lane, acc_s / count(grow, w_small, rows),
                           acc_l / count(grow, w_large, rows))
            acc_s = jnp.zeros((tile, LANES), F32)
            acc_l = jnp.zeros((tile, LANES), F32)
            for cc in offsets(w_large):
                if abs(cc) >= width:
                    continue
                if cc == 0:
                    term = m1
                else:
                    valid = (gcol >= -cc) if cc < 0 else (gcol < width - cc)
                    term = jnp.where(valid, pltpu.roll(m1, (-cc) % tile, axis=0), 0.0)
                acc_l = acc_l + term
                if cc in offsets(w_small):
                    acc_s = acc_s + term
            m2 = jnp.where(small_lane, acc_s / count(gcol, w_small, width),
                           acc_l / count(gcol, w_large, width))
            o_ref[0, pl.ds(start, tile), lanes] = m2 - centre
            return carry

        lax.fori_loop(0, s // tile, body, 0)


def _pool_call(pin, rows, width):
    b, s, n = pin.shape
    tile = min(max(POOL_TILE_ROWS * width, TOKEN_TILE), s)
    halo = SUBLANES * width
    blk = pl.BlockSpec((1, s, n), lambda bi: (bi, 0, 0))
    return pl.pallas_call(
        functools.partial(_pool_kernel, rows=rows, width=width, tile=tile),
        grid=(b,),
        in_specs=[blk],
        out_specs=blk,
        out_shape=jax.ShapeDtypeStruct((b, s, n), F32),
        scratch_shapes=[pltpu.VMEM((s + 2 * halo, LANES), F32)],
        compiler_params=_params(("parallel",)),
        name="pool",
    )(pin)


def _mix_kernel(x_ref, xp_ref, xn_ref, sh_ref, sc_ref, gt_ref, g_ref, of_ref, ob_ref, pd_ref,
                wz_ref, wsc_ref, wgate_ref, dng_ref, wpool_ref, pscale_ref, scw_ref,
                wa_ref, wb_ref, wc_ref, wo_ref, out_ref):
    i = pl.program_id(1)
    last = pl.num_programs(1) - 1
    d = x_ref.shape[2]
    g, sh, sc = g_ref[...], sh_ref[0], sc_ref[0]
    xm = x_ref[0]
    hb = _modulate(xm, g, sh, sc).astype(BF16)
    hp = _modulate(xp_ref[0], g, sh, sc).astype(BF16)
    hn = _modulate(xn_ref[0], g, sh, sc).astype(BF16)
    z = _dot(hb, wz_ref[...])
    o = of_ref[0] + ob_ref[0]
    parts = []
    for h in range(DN_HEADS):
        lo = h * DN_HEAD_DIM
        parts.append(_rms(o[:, lo:lo + DN_HEAD_DIM]) * dng_ref[...] * _silu(z[:, lo:lo + DN_HEAD_DIM]))
    y_a = _dot(jnp.concatenate(parts, axis=1), wa_ref[...])
    y_b = _dot(_dot(pd_ref[0], wpool_ref[...]) * pscale_ref[...], wb_ref[...])
    psc = _dot(hb, wsc_ref[...])
    pscp = _dot(hp, wsc_ref[...])[SUBLANES - 1:SUBLANES]
    pscn = _dot(hn, wsc_ref[...])[0:1]
    cx = psc[:, 2 * SC_WIDTH:] * psc[:, :SC_WIDTH]
    cxp = pscp[:, 2 * SC_WIDTH:] * pscp[:, :SC_WIDTH] * jnp.where(i > 0, 1.0, 0.0)
    cxn = pscn[:, 2 * SC_WIDTH:] * pscn[:, :SC_WIDTH] * jnp.where(i < last, 1.0, 0.0)
    dn, up = _shift_rows(cx, cxp, cxn)
    conv = dn * scw_ref[0:1] + cx * scw_ref[1:2] + up * scw_ref[2:3]
    y_c = _dot(psc[:, SC_WIDTH:2 * SC_WIDTH] * conv, wc_ref[...])
    gates = jax.nn.sigmoid(_dot(hb, wgate_ref[...]))
    y = gates[:, :d] * y_a + gates[:, d:2 * d] * y_b + gates[:, 2 * d:] * y_c
    out_ref[0] = xm + gt_ref[0] * _dot(y, wo_ref[...])


def _mix_call(x, shift, scale, gate, norm_g, o_f, o_b, pd, consts):
    b, s, d = x.shape
    tm = min(TOKEN_TILE, s)
    nt = s // tm
    r8 = tm // SUBLANES
    nb8 = s // SUBLANES
    tok = lambda n: pl.BlockSpec((1, tm, n), lambda bi, i: (bi, i, 0))
    vec = pl.BlockSpec((1, 1, d), lambda bi, i: (bi, 0, 0))
    return pl.pallas_call(
        _mix_kernel,
        grid=(b, nt),
        in_specs=[tok(d),
                  pl.BlockSpec((1, SUBLANES, d), lambda bi, i: (bi, jnp.maximum(i * r8 - 1, 0), 0)),
                  pl.BlockSpec((1, SUBLANES, d), lambda bi, i: (bi, jnp.minimum((i + 1) * r8, nb8 - 1), 0)),
                  vec, vec, vec, _const_spec((1, d)),
                  tok(DN_WIDTH), tok(DN_WIDTH), tok(POOL_WIDTH)]
        + [_const_spec(w.shape) for w in consts],
        out_specs=tok(d),
        out_shape=jax.ShapeDtypeStruct((b, s, d), F32),
        compiler_params=_params(("parallel", "parallel")),
        name="mix",
    )(x, x, x, shift, scale, gate, norm_g, o_f, o_b, pd, *consts)


def _ffn_kernel(x_ref, sh_ref, sc_ref, gt_ref, g_ref, wgu_ref, wdown_ref, gf_ref, out_ref, *, final):
    dff = wdown_ref.shape[0]
    xm = x_ref[0]
    hb = _modulate(xm, g_ref[...], sh_ref[0], sc_ref[0]).astype(BF16)
    gu = _dot(hb, wgu_ref[...])
    act = _silu(gu[:, :dff]) * gu[:, dff:]
    r = xm + gt_ref[0] * _dot(act, wdown_ref[...])
    if final:
        r = _rms(r) * gf_ref[...]
    out_ref[0] = r


def _ffn_call(x, shift, scale, gate, norm_g, w_gu, w_down, final_g, final):
    b, s, d = x.shape
    tm = min(TOKEN_TILE, s)
    tok = pl.BlockSpec((1, tm, d), lambda bi, i: (bi, i, 0))
    vec = pl.BlockSpec((1, 1, d), lambda bi, i: (bi, 0, 0))
    return pl.pallas_call(
        functools.partial(_ffn_kernel, final=final),
        grid=(b, s // tm),
        in_specs=[tok, vec, vec, vec, _const_spec((1, d)), _const_spec(w_gu.shape),
                  _const_spec(w_down.shape), _const_spec((1, d))],
        out_specs=tok,
        out_shape=jax.ShapeDtypeStruct((b, s, d), F32),
        compiler_params=_params(("parallel", "parallel")),
        name="ffn",
    )(x, shift, scale, gate, norm_g, w_gu, w_down, final_g)


def _block_diag(w):
    g, ci, co = w.shape
    out = jnp.zeros((g * ci, g * co), w.dtype)
    for j in range(g):
        out = out.at[j * ci:(j + 1) * ci, j * co:(j + 1) * co].set(w[j])
    return out


def kernel(x, c, ctx, c_ctx, w_ada, b_ada, norm1_g, norm2_g, w_in, dn_conv_w, dn_a_log, dn_dt_bias,
           dn_norm_g, pool_w, pool_scale, sc_conv_w, w_br_a, w_br_b, w_br_c, w_o, w_gu, w_down,
           final_norm_g):
    bn, seq, d = x.shape
    depth = w_ada.shape[0]
    rows = seq // GRID_W
    off_z = 3 * DN_WIDTH
    off_a = off_z + DN_WIDTH
    off_pool = off_a + 4 * DN_HEADS
    off_sc = off_pool + POOL_WIDTH
    off_gate = off_sc + 3 * SC_WIDTH

    n_c = -(-(bn + 1) // SUBLANES) * SUBLANES
    cs = jnp.concatenate([c, c_ctx[None], jnp.zeros((n_c - bn - 1, d), F32)], axis=0)
    mod = _ada_call(cs, w_ada, b_ada)

    s0 = jnp.zeros((bn, 2, DN_HEADS, DN_HEAD_DIM, DN_HEAD_DIM), F32)
    final_g = final_norm_g.reshape(1, d)
    for l in range(depth):
        wl = w_in[l]
        w_qkv = wl[:, :off_z].astype(BF16)
        w_z = wl[:, off_z:off_a].astype(BF16)
        w_ab = jnp.pad(wl[:, off_a:off_pool], ((0, 0), (0, LANES - 4 * DN_HEADS))).astype(BF16)
        w_pool = wl[:, off_pool:off_sc].astype(BF16)
        w_sc = wl[:, off_sc:off_gate].astype(BF16)
        w_gate = wl[:, off_gate:].astype(BF16)
        alog = jnp.pad(dn_a_log[l].reshape(1, -1), ((0, 0), (0, LANES - 2 * DN_HEADS)))
        dtb = jnp.pad(dn_dt_bias[l].reshape(1, -1), ((0, 0), (0, LANES - 2 * DN_HEADS)))
        n1 = norm1_g[l].reshape(1, d)
        n2 = norm2_g[l].reshape(1, d)
        mix_consts = (w_z, w_sc, w_gate, dn_norm_g[l].reshape(1, -1),
                      _block_diag(pool_w[l]).astype(BF16), pool_scale[l].reshape(1, -1), sc_conv_w[l],
                      w_br_a[l].astype(BF16), w_br_b[l].astype(BF16), w_br_c[l].astype(BF16),
                      w_o[l].astype(BF16))
        wgu = w_gu[l].astype(BF16)
        wdn = w_down[l].astype(BF16)
        lat = [mod[l, :bn, j * d:(j + 1) * d][:, None, :] for j in range(6)]
        cxm = [jnp.broadcast_to(mod[l, bn:bn + 1, j * d:(j + 1) * d][None], (bn, 1, d)) for j in range(6)]

        q, k, v, gb, pin = _proj_call(ctx, cxm[0], cxm[1], n1, w_qkv, w_ab, w_pool, dn_conv_w[l], alog, dtb)
        o_f, o_b, s_ctx = _scan_call(q, k, v, gb, s0)
        if l < depth - 1:
            pd = _pool_call(pin, ctx.shape[1], 1)
            ctx = _mix_call(ctx, cxm[0], cxm[1], cxm[2], n1, o_f, o_b, pd, mix_consts)
            ctx = _ffn_call(ctx, cxm[3], cxm[4], cxm[5], n2, wgu, wdn, final_g, False)

        q, k, v, gb, pin = _proj_call(x, lat[0], lat[1], n1, w_qkv, w_ab, w_pool, dn_conv_w[l], alog, dtb)
        o_f, o_b, _ = _scan_call(q, k, v, gb, s_ctx)
        pd = _pool_call(pin, rows, GRID_W)
        x = _mix_call(x, lat[0], lat[1], lat[2], n1, o_f, o_b, pd, mix_consts)
        x = _ffn_call(x, lat[3], lat[4], lat[5], n2, wgu, wdn, final_g, l == depth - 1)
    return x
```

```python
import functools
import math

import jax
import jax.numpy as jnp
from jax import lax
from jax.experimental import pallas as pl
from jax.experimental.pallas import tpu as pltpu

F32 = jnp.float32
BF16 = jnp.bfloat16

EPS = 1e-6
GRID_W = 64
DN_HEADS = 4
DN_HEAD_DIM = 128
DN_WIDTH = DN_HEADS * DN_HEAD_DIM
POOL_WINDOWS = (2, 4, 8, 16)
POOL_GROUP_DIM = 64
POOL_WIDTH = POOL_GROUP_DIM * len(POOL_WINDOWS)
SC_WIDTH = 256
N_BRANCH = 3

LANES = 128
SUBLANES = 8
SCAN_CHUNK = 128
SCAN_BLOCK = 256
TOKEN_TILE = 256
POOL_TILE_ROWS = 8
VMEM_LIMIT = 56 * 1024 * 1024
NEG_BIG = -1e30


def _dot(a, b):
    return jnp.dot(a.astype(BF16), b.astype(BF16), preferred_element_type=F32)


def _dot_nt(a, b):
    return lax.dot_general(a.astype(BF16), b.astype(BF16), (((1,), (1,)), ((), ())),
                           preferred_element_type=F32)


def _dot_tn(a, b):
    return lax.dot_general(a.astype(BF16), b.astype(BF16), (((0,), (0,)), ((), ())),
                           preferred_element_type=F32)


def _silu(x):
    return x * jax.nn.sigmoid(x)


def _rms(x):
    return x * lax.rsqrt(jnp.mean(x * x, axis=-1, keepdims=True) + EPS)


def _modulate(x, g, shift, scale):
    return (_rms(x) * g) * (1.0 + scale) + shift


def _shift_rows(x, prev_row, next_row):
    n = x.shape[0]
    r = lax.broadcasted_iota(jnp.int32, x.shape, 0)
    dn = jnp.where(r == 0, prev_row, pltpu.roll(x, 1, axis=0))
    up = jnp.where(r == n - 1, next_row, pltpu.roll(x, n - 1, axis=0))
    return dn, up


def _const_spec(shape):
    nd = len(shape)
    return pl.BlockSpec(shape, lambda *_: (0,) * nd, pipeline_mode=pl.Buffered(1))


def _params(sem):
    return pltpu.CompilerParams(dimension_semantics=sem, vmem_limit_bytes=VMEM_LIMIT)


def _ada_kernel(c_ref, w_ref, b_ref, o_ref):
    a = _silu(c_ref[...])
    o_ref[0] = jnp.dot(a, w_ref[0], preferred_element_type=F32,
                       precision=lax.Precision.HIGHEST) + b_ref[0]


def _ada_call(cs, w_ada, b_ada):
    nl, d, n6 = w_ada.shape
    rows = cs.shape[0]
    tn = n6 // 4
    return pl.pallas_call(
        _ada_kernel,
        grid=(nl, n6 // tn),
        in_specs=[pl.BlockSpec((rows, d), lambda l, j: (0, 0)),
                  pl.BlockSpec((1, d, tn), lambda l, j: (l, 0, j)),
                  pl.BlockSpec((1, 1, tn), lambda l, j: (l, 0, j))],
        out_specs=pl.BlockSpec((1, rows, tn), lambda l, j: (l, 0, j)),
        out_shape=jax.ShapeDtypeStruct((nl, rows, n6), F32),
        compiler_params=_params(("parallel", "parallel")),
        name="ada",
    )(cs, w_ada, b_ada.reshape(nl, 1, n6))


def _proj_kernel(x_ref, xp_ref, xn_ref, sh_ref, sc_ref, g_ref, wqkv_ref, wab_ref, wpool_ref,
                 cw_ref, alog_ref, dtb_ref, q_ref, k_ref, v_ref, gb_ref, pin_ref):
    i = pl.program_id(1)
    last = pl.num_programs(1) - 1
    g, sh, sc = g_ref[...], sh_ref[0], sc_ref[0]
    hb = _modulate(x_ref[0], g, sh, sc).astype(BF16)
    hp = _modulate(xp_ref[0], g, sh, sc).astype(BF16)
    hn = _modulate(xn_ref[0], g, sh, sc).astype(BF16)
    p = _dot(hb, wqkv_ref[...])
    pp = _dot(hp, wqkv_ref[...])[SUBLANES - 1:SUBLANES] * jnp.where(i > 0, 1.0, 0.0)
    pn = _dot(hn, wqkv_ref[...])[0:1] * jnp.where(i < last, 1.0, 0.0)
    dn, up = _shift_rows(p, pp, pn)
    a = _silu(dn * cw_ref[0:1] + p * cw_ref[1:2] + up * cw_ref[2:3])
    for h in range(DN_HEADS):
        lo = h * DN_HEAD_DIM
        qh = a[:, lo:lo + DN_HEAD_DIM]
        kh = a[:, DN_WIDTH + lo:DN_WIDTH + lo + DN_HEAD_DIM]
        q_ref[0, :, lo:lo + DN_HEAD_DIM] = (
            qh * lax.rsqrt(jnp.sum(qh * qh, axis=-1, keepdims=True) + EPS) * (DN_HEAD_DIM ** -0.5))
        k_ref[0, :, lo:lo + DN_HEAD_DIM] = (
            kh * lax.rsqrt(jnp.sum(kh * kh, axis=-1, keepdims=True) + EPS))
    v_ref[0] = a[:, 2 * DN_WIDTH:]
    pab = _dot(hb, wab_ref[...])
    z = pab + dtb_ref[...]
    softplus = jnp.maximum(z, 0.0) + jnp.log(1.0 + jnp.exp(-jnp.abs(z)))
    gdec = -jnp.exp(alog_ref[...]) * softplus
    lane = lax.broadcasted_iota(jnp.int32, pab.shape, 1)
    gb_ref[0] = jnp.where(lane < 2 * DN_HEADS, gdec,
                          jnp.where(lane < 4 * DN_HEADS, jax.nn.sigmoid(pab), 0.0))
    pin_ref[0] = _dot(hb, wpool_ref[...])


def _proj_call(x, shift, scale, norm_g, w_qkv, w_ab, w_pool, conv_w, alog, dtb):
    b, s, d = x.shape
    tm = min(TOKEN_TILE, s)
    nt = s // tm
    r8 = tm // SUBLANES
    nb8 = s // SUBLANES
    tok = lambda n: pl.BlockSpec((1, tm, n), lambda bi, i: (bi, i, 0))
    vec = pl.BlockSpec((1, 1, d), lambda bi, i: (bi, 0, 0))
    return pl.pallas_call(
        _proj_kernel,
        grid=(b, nt),
        in_specs=[tok(d),
                  pl.BlockSpec((1, SUBLANES, d), lambda bi, i: (bi, jnp.maximum(i * r8 - 1, 0), 0)),
                  pl.BlockSpec((1, SUBLANES, d), lambda bi, i: (bi, jnp.minimum((i + 1) * r8, nb8 - 1), 0)),
                  vec, vec, _const_spec((1, d)),
                  _const_spec(w_qkv.shape), _const_spec(w_ab.shape), _const_spec(w_pool.shape),
                  _const_spec(conv_w.shape), _const_spec(alog.shape), _const_spec(dtb.shape)],
        out_specs=[tok(DN_WIDTH), tok(DN_WIDTH), tok(DN_WIDTH), tok(LANES), tok(POOL_WIDTH)],
        out_shape=[jax.ShapeDtypeStruct((b, s, DN_WIDTH), F32)] * 3
        + [jax.ShapeDtypeStruct((b, s, LANES), F32), jax.ShapeDtypeStruct((b, s, POOL_WIDTH), F32)],
        compiler_params=_params(("parallel", "parallel")),
        name="proj_in",
    )(x, x, x, shift, scale, norm_g, w_qkv, w_ab, w_pool, conv_w, alog, dtb)


def _prefix_sum_rows(x):
    n = x.shape[0]
    r = lax.broadcasted_iota(jnp.int32, x.shape, 0)
    s = 1
    while s < n:
        x = x + jnp.where(r >= s, pltpu.roll(x, s, axis=0), 0.0)
        s *= 2
    return x


def _mm(a, b):
    return jnp.dot(a, b, preferred_element_type=F32)


def _scan_kernel(qf_ref, kf_ref, vf_ref, gf_ref, qb_ref, kb_ref, vb_ref, gbk_ref, s0_ref,
                 of_ref, ob_ref, s_ref, msk_ref, tri_ref, *, c):
    nlev = msk_ref.shape[0] - 1
    i = pl.program_id(1)

    @pl.when(i == 0)
    def _():
        s_ref[...] = s0_ref[...]
        row = lax.broadcasted_iota(jnp.int32, (c, c), 0)
        col = lax.broadcasted_iota(jnp.int32, (c, c), 1)

        def same(sh):
            return jnp.right_shift(row, sh) == jnp.right_shift(col, sh)

        one = jnp.ones((c, c), F32)
        zero = jnp.zeros((c, c), F32)
        msk_ref[0] = jnp.where(same(1), jnp.where(row == col, zero, one), zero).astype(BF16)
        for j in range(1, nlev):
            msk_ref[j] = jnp.where(same(j + 1), jnp.where(same(j), zero, one), zero).astype(BF16)
        msk_ref[nlev] = jnp.where(row == col, one, zero).astype(BF16)
        tri_ref[0] = jnp.where(row >= col, 0.0, NEG_BIG)
        tri_ref[1] = jnp.where(row <= col, 0.0, NEG_BIG)

    nblk = qf_ref.shape[1] // c
    units = []
    for d, (q_ref, k_ref, v_ref, gb_ref, o_ref) in enumerate(
            ((qf_ref, kf_ref, vf_ref, gf_ref, of_ref), (qb_ref, kb_ref, vb_ref, gbk_ref, ob_ref))):
        for g in range(nblk):
            rs = slice(g * c, (g + 1) * c)
            gb = gb_ref[0, rs, :]
            cs = _prefix_sum_rows(gb)
            tot = cs[c - 1:c, :]
            if d == 1:
                cs = tot - cs + gb
            cs_t = cs.T
            for h in range(DN_HEADS):
                gi = d * DN_HEADS + h
                bi = 2 * DN_HEADS + gi
                hs = slice(h * DN_HEAD_DIM, (h + 1) * DN_HEAD_DIM)
                units.append(dict(d=d, h=h, g=g, rs=rs, hs=hs, o_ref=o_ref, gc=cs[:, gi:gi + 1],
                                  gc_row=cs_t[gi:gi + 1, :], gl=tot[:, gi:gi + 1], beta=gb[:, bi:bi + 1],
                                  q=q_ref[0, rs, hs], k=k_ref[0, rs, hs], v=v_ref[0, rs, hs]))
    nu = range(len(units))
    dec = [jnp.exp((u["gc"] - u["gc_row"]) + tri_ref[u["d"]]) for u in units]
    egc = [jnp.exp(u["gc"]) for u in units]
    k16 = [u["k"].astype(BF16) for u in units]
    kbeta = [u["k"] * u["beta"] for u in units]
    a16 = [(_dot_nt(kbeta[j], k16[j]) * dec[j]).astype(BF16) for j in nu]
    aqk16 = [(_dot_nt(units[j]["q"], k16[j]) * dec[j]).astype(BF16) for j in nu]
    eye16 = msk_ref[nlev]
    t16 = [eye16 - a16[j] * msk_ref[0] for j in nu]
    for lev in range(1, nlev):
        p16 = [_mm(t16[j], a16[j] * msk_ref[lev]).astype(BF16) for j in nu]
        t16 = [t16[j] - _mm(p16[j], t16[j]).astype(BF16) for j in nu]
    rhs = [jnp.concatenate([units[j]["v"] * units[j]["beta"], kbeta[j] * egc[j]], axis=1) for j in nu]
    sol = [rhs[j] + _mm(t16[j] - eye16, rhs[j].astype(BF16)) for j in nu]
    qd16 = [(units[j]["q"] * egc[j]).astype(BF16) for j in nu]
    kst16 = [(units[j]["k"] * jnp.exp(units[j]["gl"] - units[j]["gc"])).astype(BF16) for j in nu]
    state = {(d, h): s_ref[0, d, h] for d in range(2) for h in range(DN_HEADS)}
    for step in range(nblk):
        cur = [j for j in nu if units[j]["g"] == (step if units[j]["d"] == 0 else nblk - 1 - step)]
        s16 = {j: state[units[j]["d"], units[j]["h"]].astype(BF16) for j in cur}
        vn16 = {j: (sol[j][:, :DN_HEAD_DIM] - _mm(sol[j][:, DN_HEAD_DIM:].astype(BF16), s16[j])).astype(BF16)
                for j in cur}
        for j in cur:
            u = units[j]
            u["o_ref"][0, u["rs"], u["hs"]] = _mm(qd16[j], s16[j]) + _mm(aqk16[j], vn16[j])
        for j in cur:
            u = units[j]
            state[u["d"], u["h"]] = (state[u["d"], u["h"]] * jnp.exp(u["gl"])
                                     + lax.dot_general(kst16[j], vn16[j], (((0,), (0,)), ((), ())),
                                                       preferred_element_type=F32))
    for (d, h), val in state.items():
        s_ref[0, d, h] = val


def _scan_call(q, k, v, gb, s0):
    b, s, _ = q.shape
    c = min(SCAN_CHUNK, s)
    blk = min(SCAN_BLOCK, s)
    nb = s // blk
    nlev = int(math.log2(c))
    fwd = lambda n: pl.BlockSpec((1, blk, n), lambda bi, i: (bi, i, 0))
    bwd = lambda n: pl.BlockSpec((1, blk, n), lambda bi, i: (bi, nb - 1 - i, 0))
    st = pl.BlockSpec((1, 2, DN_HEADS, DN_HEAD_DIM, DN_HEAD_DIM), lambda bi, i: (bi, 0, 0, 0, 0))
    return pl.pallas_call(
        functools.partial(_scan_kernel, c=c),
        grid=(b, nb),
        in_specs=[fwd(DN_WIDTH), fwd(DN_WIDTH), fwd(DN_WIDTH), fwd(LANES),
                  bwd(DN_WIDTH), bwd(DN_WIDTH), bwd(DN_WIDTH), bwd(LANES), st],
        out_specs=[fwd(DN_WIDTH), bwd(DN_WIDTH), st],
        out_shape=[jax.ShapeDtypeStruct((b, s, DN_WIDTH), F32)] * 2
        + [jax.ShapeDtypeStruct(s0.shape, F32)],
        scratch_shapes=[pltpu.VMEM((nlev + 1, c, c), BF16), pltpu.VMEM((2, c, c), F32)],
        compiler_params=_params(("parallel", "arbitrary")),
        name="scan",
    )(q, k, v, gb, q, k, v, gb, s0)


def _pool_kernel(x_ref, o_ref, pad_ref, *, rows, width, tile):
    s = rows * width
    halo = SUBLANES * width
    ext_n = tile + 2 * halo
    wshift = int(math.log2(width))
    for slab in range(POOL_WIDTH // LANES):
        w_small, w_large = POOL_WINDOWS[2 * slab], POOL_WINDOWS[2 * slab + 1]
        lanes = slice(slab * LANES, (slab + 1) * LANES)
        pad_ref[0:halo, :] = jnp.zeros((halo, LANES), F32)
        pad_ref[halo + s:, :] = jnp.zeros((halo, LANES), F32)
        pad_ref[halo:halo + s, :] = x_ref[0, :, lanes]

        def offsets(w):
            return range(-(w // 2), w - (w // 2))

        def body(ti, carry):
            start = pl.multiple_of(ti * tile, SUBLANES)
            ext = pad_ref[pl.ds(start, ext_n), :]
            centre = ext[halo:halo + tile]
            t = start + lax.broadcasted_iota(jnp.int32, (tile, LANES), 0)
            lane = lax.broadcasted_iota(jnp.int32, (tile, LANES), 1)
            grow = jnp.right_shift(t, wshift)
            gcol = jnp.bitwise_and(t, width - 1)
            small_lane = lane < POOL_GROUP_DIM

            def count(pos, w, n):
                lo = w // 2
                hi = w - 1 - lo
                return (jnp.minimum(pos + hi + 1, n) - jnp.maximum(pos - lo, 0)).astype(F32)

            acc_s = jnp.zeros((tile, LANES), F32)
            acc_l = jnp.zeros((tile, LANES), F32)
            for r in offsets(w_large):
                off = r * width
                if off % SUBLANES == 0:
                    term = ext[halo + off:halo + off + tile]
                else:
                    term = pltpu.roll(ext, (-off) % ext_n, axis=0)[halo:halo + tile]
                acc_l = acc_l + term
                if r in offsets(w_small):
                    acc_s = acc_s + term
            m1 = jnp.where(small_lane, acc_s / count(grow, w_small, rows),
                           acc_l / count(grow, w_large, rows))
            acc_s = jnp.zeros((tile, LANES), F32)
            acc_l = jnp.zeros((tile, LANES), F32)
            for cc in offsets(w_large):
                if abs(cc) >= width:
                    continue
                if cc == 0:
                    term = m1
                else:
                    valid = (gcol >= -cc) if cc < 0 else (gcol < width - cc)
                    term = jnp.where(valid, pltpu.roll(m1, (-cc) % tile, axis=0), 0.0)
                acc_l = acc_l + term
                if cc in offsets(w_small):
                    acc_s = acc_s + term
            m2 = jnp.where(small_lane, acc_s / count(gcol, w_small, width),
                           acc_l / count(gcol, w_large, width))
            o_ref[0, pl.ds(start, tile), lanes] = m2 - centre
            return carry

        lax.fori_loop(0, s // tile, body, 0)


def _pool_call(pin, rows, width):
    b, s, n = pin.shape
    tile = min(max(POOL_TILE_ROWS * width, TOKEN_TILE), s)
    halo = SUBLANES * width
    blk = pl.BlockSpec((1, s, n), lambda bi: (bi, 0, 0))
    return pl.pallas_call(
        functools.partial(_pool_kernel, rows=rows, width=width, tile=tile),
        grid=(b,),
        in_specs=[blk],
        out_specs=blk,
        out_shape=jax.ShapeDtypeStruct((b, s, n), F32),
        scratch_shapes=[pltpu.VMEM((s + 2 * halo, LANES), F32)],
        compiler_params=_params(("parallel",)),
        name="pool",
    )(pin)


def _mix_kernel(x_ref, xp_ref, xn_ref, sh_ref, sc_ref, gt_ref, g_ref, of_ref, ob_ref, pd_ref,
                wz_ref, wsc_ref, wgate_ref, dng_ref, wpool_ref, pscale_ref, scw_ref,
                wa_ref, wb_ref, wc_ref, wo_ref, out_ref):
    i = pl.program_id(1)
    last = pl.num_programs(1) - 1
    d = x_ref.shape[2]
    g, sh, sc = g_ref[...], sh_ref[0], sc_ref[0]
    xm = x_ref[0]
    hb = _modulate(xm, g, sh, sc).astype(BF16)
    hp = _modulate(xp_ref[0], g, sh, sc).astype(BF16)
    hn = _modulate(xn_ref[0], g, sh, sc).astype(BF16)
    z = _dot(hb, wz_ref[...])
    o = of_ref[0] + ob_ref[0]
    parts = []
    for h in range(DN_HEADS):
        lo = h * DN_HEAD_DIM
        parts.append(_rms(o[:, lo:lo + DN_HEAD_DIM]) * dng_ref[...] * _silu(z[:, lo:lo + DN_HEAD_DIM]))
    y_a = _dot(jnp.concatenate(parts, axis=1), wa_ref[...])
    y_b = _dot(_dot(pd_ref[0], wpool_ref[...]) * pscale_ref[...], wb_ref[...])
    psc = _dot(hb, wsc_ref[...])
    pscp = _dot(hp, wsc_ref[...])[SUBLANES - 1:SUBLANES]
    pscn = _dot(hn, wsc_ref[...])[0:1]
    cx = psc[:, 2 * SC_WIDTH:] * psc[:, :SC_WIDTH]
    cxp = pscp[:, 2 * SC_WIDTH:] * pscp[:, :SC_WIDTH] * jnp.where(i > 0, 1.0, 0.0)
    cxn = pscn[:, 2 * SC_WIDTH:] * pscn[:, :SC_WIDTH] * jnp.where(i < last, 1.0, 0.0)
    dn, up = _shift_rows(cx, cxp, cxn)
    conv = dn * scw_ref[0:1] + cx * scw_ref[1:2] + up * scw_ref[2:3]
    y_c = _dot(psc[:, SC_WIDTH:2 * SC_WIDTH] * conv, wc_ref[...])
    gates = jax.nn.sigmoid(_dot(hb, wgate_ref[...]))
    y = gates[:, :d] * y_a + gates[:, d:2 * d] * y_b + gates[:, 2 * d:] * y_c
    out_ref[0] = xm + gt_ref[0] * _dot(y, wo_ref[...])


def _mix_call(x, shift, scale, gate, norm_g, o_f, o_b, pd, consts):
    b, s, d = x.shape
    tm = min(TOKEN_TILE, s)
    nt = s // tm
    r8 = tm // SUBLANES
    nb8 = s // SUBLANES
    tok = lambda n: pl.BlockSpec((1, tm, n), lambda bi, i: (bi, i, 0))
    vec = pl.BlockSpec((1, 1, d), lambda bi, i: (bi, 0, 0))
    return pl.pallas_call(
        _mix_kernel,
        grid=(b, nt),
        in_specs=[tok(d),
                  pl.BlockSpec((1, SUBLANES, d), lambda bi, i: (bi, jnp.maximum(i * r8 - 1, 0), 0)),
                  pl.BlockSpec((1, SUBLANES, d), lambda bi, i: (bi, jnp.minimum((i + 1) * r8, nb8 - 1), 0)),
                  vec, vec, vec, _const_spec((1, d)),
                  tok(DN_WIDTH), tok(DN_WIDTH), tok(POOL_WIDTH)]
        + [_const_spec(w.shape) for w in consts],
        out_specs=tok(d),
        out_shape=jax.ShapeDtypeStruct((b, s, d), F32),
        compiler_params=_params(("parallel", "parallel")),
        name="mix",
    )(x, x, x, shift, scale, gate, norm_g, o_f, o_b, pd, *consts)


def _ffn_kernel(x_ref, sh_ref, sc_ref, gt_ref, g_ref, wgu_ref, wdown_ref, gf_ref, out_ref, *, final):
    dff = wdown_ref.shape[0]
    xm = x_ref[0]
    hb = _modulate(xm, g_ref[...], sh_ref[0], sc_ref[0]).astype(BF16)
    gu = _dot(hb, wgu_ref[...])
    act = _silu(gu[:, :dff]) * gu[:, dff:]
    r = xm + gt_ref[0] * _dot(act, wdown_ref[...])
    if final:
        r = _rms(r) * gf_ref[...]
    out_ref[0] = r


def _ffn_call(x, shift, scale, gate, norm_g, w_gu, w_down, final_g, final):
    b, s, d = x.shape
    tm = min(TOKEN_TILE, s)
    tok = pl.BlockSpec((1, tm, d), lambda bi, i: (bi, i, 0))
    vec = pl.BlockSpec((1, 1, d), lambda bi, i: (bi, 0, 0))
    return pl.pallas_call(
        functools.partial(_ffn_kernel, final=final),
        grid=(b, s // tm),
        in_specs=[tok, vec, vec, vec, _const_spec((1, d)), _const_spec(w_gu.shape),
                  _const_spec(w_down.shape), _const_spec((1, d))],
        out_specs=tok,
        out_shape=jax.ShapeDtypeStruct((b, s, d), F32),
        compiler_params=_params(("parallel", "parallel")),
        name="ffn",
    )(x, shift, scale, gate, norm_g, w_gu, w_down, final_g)


def _block_diag(w):
    g, ci, co = w.shape
    out = jnp.zeros((g * ci, g * co), w.dtype)
    for j in range(g):
        out = out.at[j * ci:(j + 1) * ci, j * co:(j + 1) * co].set(w[j])
    return out


def kernel(x, c, ctx, c_ctx, w_ada, b_ada, norm1_g, norm2_g, w_in, dn_conv_w, dn_a_log, dn_dt_bias,
           dn_norm_g, pool_w, pool_scale, sc_conv_w, w_br_a, w_br_b, w_br_c, w_o, w_gu, w_down,
           final_norm_g):
    bn, seq, d = x.shape
    depth = w_ada.shape[0]
    rows = seq // GRID_W
    off_z = 3 * DN_WIDTH
    off_a = off_z + DN_WIDTH
    off_pool = off_a + 4 * DN_HEADS
    off_sc = off_pool + POOL_WIDTH
    off_gate = off_sc + 3 * SC_WIDTH

    n_c = -(-(bn + 1) // SUBLANES) * SUBLANES
    cs = jnp.concatenate([c, c_ctx[None], jnp.zeros((n_c - bn - 1, d), F32)], axis=0)
    mod = _ada_call(cs, w_ada, b_ada)

    s0 = jnp.zeros((bn, 2, DN_HEADS, DN_HEAD_DIM, DN_HEAD_DIM), F32)
    final_g = final_norm_g.reshape(1, d)
    for l in range(depth):
        wl = w_in[l]
        w_qkv = wl[:, :off_z].astype(BF16)
        w_z = wl[:, off_z:off_a].astype(BF16)
        w_ab = jnp.pad(wl[:, off_a:off_pool], ((0, 0), (0, LANES - 4 * DN_HEADS))).astype(BF16)
        w_pool = wl[:, off_pool:off_sc].astype(BF16)
        w_sc = wl[:, off_sc:off_gate].astype(BF16)
        w_gate = wl[:, off_gate:].astype(BF16)
        alog = jnp.pad(dn_a_log[l].reshape(1, -1), ((0, 0), (0, LANES - 2 * DN_HEADS)))
        dtb = jnp.pad(dn_dt_bias[l].reshape(1, -1), ((0, 0), (0, LANES - 2 * DN_HEADS)))
        n1 = norm1_g[l].reshape(1, d)
        n2 = norm2_g[l].reshape(1, d)
        mix_consts = (w_z, w_sc, w_gate, dn_norm_g[l].reshape(1, -1),
                      _block_diag(pool_w[l]).astype(BF16), pool_scale[l].reshape(1, -1), sc_conv_w[l],
                      w_br_a[l].astype(BF16), w_br_b[l].astype(BF16), w_br_c[l].astype(BF16),
                      w_o[l].astype(BF16))
        wgu = w_gu[l].astype(BF16)
        wdn = w_down[l].astype(BF16)
        lat = [mod[l, :bn, j * d:(j + 1) * d][:, None, :] for j in range(6)]
        cxm = [jnp.broadcast_to(mod[l, bn:bn + 1, j * d:(j + 1) * d][None], (bn, 1, d)) for j in range(6)]

        q, k, v, gb, pin = _proj_call(ctx, cxm[0], cxm[1], n1, w_qkv, w_ab, w_pool, dn_conv_w[l], alog, dtb)
        o_f, o_b, s_ctx = _scan_call(q, k, v, gb, s0)
        if l < depth - 1:
            pd = _pool_call(pin, ctx.shape[1], 1)
            ctx = _mix_call(ctx, cxm[0], cxm[1], cxm[2], n1, o_f, o_b, pd, mix_consts)
            ctx = _ffn_call(ctx, cxm[3], cxm[4], cxm[5], n2, wgu, wdn, final_g, False)

        q, k, v, gb, pin = _proj_call(x, lat[0], lat[1], n1, w_qkv, w_ab, w_pool, dn_conv_w[l], alog, dtb)
        o_f, o_b, _ = _scan_call(q, k, v, gb, s_ctx)
        pd = _pool_call(pin, rows, GRID_W)
        x = _mix_call(x, lat[0], lat[1], lat[2], n1, o_f, o_b, pd, mix_consts)
        x = _ffn_call(x, lat[3], lat[4], lat[5], n2, wgu, wdn, final_g, l == depth - 1)
    return x
```

```python
import functools
import math

import jax
import jax.numpy as jnp
from jax import lax
from jax.experimental import pallas as pl
from jax.experimental.pallas import tpu as pltpu

F32 = jnp.float32
BF16 = jnp.bfloat16

EPS = 1e-6
GRID_W = 64
DN_HEADS = 4
DN_HEAD_DIM = 128
DN_WIDTH = DN_HEADS * DN_HEAD_DIM
POOL_WINDOWS = (2, 4, 8, 16)
POOL_GROUP_DIM = 64
POOL_WIDTH = POOL_GROUP_DIM * len(POOL_WINDOWS)
SC_WIDTH = 256
N_BRANCH = 3

LANES = 128
SUBLANES = 8
SCAN_CHUNK = 128
SCAN_BLOCK = 256
TOKEN_TILE = 512
POOL_TILE_ROWS = 8
VMEM_LIMIT = 56 * 1024 * 1024
NEG_BIG = -1e30


def _dot(a, b):
    return jnp.dot(a.astype(BF16), b.astype(BF16), preferred_element_type=F32)


def _dot_nt(a, b):
    return lax.dot_general(a.astype(BF16), b.astype(BF16), (((1,), (1,)), ((), ())),
                           preferred_element_type=F32)


def _dot_tn(a, b):
    return lax.dot_general(a.astype(BF16), b.astype(BF16), (((0,), (0,)), ((), ())),
                           preferred_element_type=F32)


def _silu(x):
    return x * jax.nn.sigmoid(x)


def _rms(x):
    return x * lax.rsqrt(jnp.mean(x * x, axis=-1, keepdims=True) + EPS)


def _modulate(x, g, shift, scale):
    return (_rms(x) * g) * (1.0 + scale) + shift


def _halo_modulate(x_ref, xp_ref, xn_ref, g, shift, scale):
    x_ext = jnp.concatenate([xp_ref[0], x_ref[0], xn_ref[0]], axis=0)
    return _modulate(x_ext, g, shift, scale)


def _conv3_ext(p_ext, w_ref, keep_prev, keep_next):
    n = p_ext.shape[0] - 2 * SUBLANES
    pe = jnp.concatenate([p_ext[:SUBLANES] * keep_prev, p_ext[SUBLANES:SUBLANES + n],
                          p_ext[SUBLANES + n:] * keep_next], axis=0)
    return (pe[SUBLANES - 1:SUBLANES - 1 + n] * w_ref[0:1] + pe[SUBLANES:SUBLANES + n] * w_ref[1:2]
            + pe[SUBLANES + 1:SUBLANES + 1 + n] * w_ref[2:3])


def _const_spec(shape):
    nd = len(shape)
    return pl.BlockSpec(shape, lambda *_: (0,) * nd, pipeline_mode=pl.Buffered(1))


def _params(sem):
    return pltpu.CompilerParams(dimension_semantics=sem, vmem_limit_bytes=VMEM_LIMIT)


def _ada_kernel(c_ref, w_ref, b_ref, o_ref):
    a = _silu(c_ref[...])
    w = w_ref[0]
    a_hi, w_hi = a.astype(BF16), w.astype(BF16)
    a_lo = (a - a_hi.astype(F32)).astype(BF16)
    w_lo = (w - w_hi.astype(F32)).astype(BF16)
    mm = functools.partial(jnp.dot, preferred_element_type=F32)
    o_ref[0] = (mm(a_hi, w_hi) + (mm(a_lo, w_hi) + mm(a_hi, w_lo))) + b_ref[0]


def _ada_call(cs, w_ada, b_ada):
    nl, d, n6 = w_ada.shape
    rows = cs.shape[0]
    tn = n6 // 4
    return pl.pallas_call(
        _ada_kernel,
        grid=(nl, n6 // tn),
        in_specs=[pl.BlockSpec((rows, d), lambda l, j: (0, 0)),
                  pl.BlockSpec((1, d, tn), lambda l, j: (l, 0, j)),
                  pl.BlockSpec((1, 1, tn), lambda l, j: (l, 0, j))],
        out_specs=pl.BlockSpec((1, rows, tn), lambda l, j: (l, 0, j)),
        out_shape=jax.ShapeDtypeStruct((nl, rows, n6), F32),
        compiler_params=_params(("parallel", "parallel")),
        name="ada",
    )(cs, w_ada, b_ada.reshape(nl, 1, n6))


def _proj_kernel(x_ref, xp_ref, xn_ref, sh_ref, sc_ref, g_ref, wqkv_ref, wab_ref, wpool_ref,
                 cw_ref, alog_ref, dtb_ref, q_ref, k_ref, v_ref, gb_ref, pin_ref):
    i = pl.program_id(1)
    last = pl.num_programs(1) - 1
    tm = x_ref.shape[1]
    h_ext = _halo_modulate(x_ref, xp_ref, xn_ref, g_ref[...], sh_ref[0], sc_ref[0])
    hb = h_ext[SUBLANES:SUBLANES + tm].astype(BF16)
    p_ext = _dot(h_ext, wqkv_ref[...])
    a = _silu(_conv3_ext(p_ext, cw_ref, jnp.where(i > 0, 1.0, 0.0), jnp.where(i < last, 1.0, 0.0)))
    for h in range(DN_HEADS):
        lo = h * DN_HEAD_DIM
        qh = a[:, lo:lo + DN_HEAD_DIM]
        kh = a[:, DN_WIDTH + lo:DN_WIDTH + lo + DN_HEAD_DIM]
        q_ref[0, :, lo:lo + DN_HEAD_DIM] = (
            qh * lax.rsqrt(jnp.sum(qh * qh, axis=-1, keepdims=True) + EPS) * (DN_HEAD_DIM ** -0.5))
        k_ref[0, :, lo:lo + DN_HEAD_DIM] = (
            kh * lax.rsqrt(jnp.sum(kh * kh, axis=-1, keepdims=True) + EPS))
    v_ref[0] = a[:, 2 * DN_WIDTH:]
    pab = _dot(hb, wab_ref[...])
    z = pab + dtb_ref[...]
    softplus = jnp.maximum(z, 0.0) + jnp.log(1.0 + jnp.exp(-jnp.abs(z)))
    gdec = -jnp.exp(alog_ref[...]) * softplus
    lane = lax.broadcasted_iota(jnp.int32, pab.shape, 1)
    gb_ref[0] = jnp.where(lane < 2 * DN_HEADS, gdec,
                          jnp.where(lane < 4 * DN_HEADS, jax.nn.sigmoid(pab), 0.0))
    pin_ref[0] = _dot(hb, wpool_ref[...])


def _proj_call(x, shift, scale, norm_g, w_qkv, w_ab, w_pool, conv_w, alog, dtb):
    b, s, d = x.shape
    tm = min(TOKEN_TILE, s)
    nt = s // tm
    r8 = tm // SUBLANES
    nb8 = s // SUBLANES
    tok = lambda n: pl.BlockSpec((1, tm, n), lambda bi, i: (bi, i, 0))
    vec = pl.BlockSpec((1, 1, d), lambda bi, i: (bi, 0, 0))
    return pl.pallas_call(
        _proj_kernel,
        grid=(b, nt),
        in_specs=[tok(d),
                  pl.BlockSpec((1, SUBLANES, d), lambda bi, i: (bi, jnp.maximum(i * r8 - 1, 0), 0)),
                  pl.BlockSpec((1, SUBLANES, d), lambda bi, i: (bi, jnp.minimum((i + 1) * r8, nb8 - 1), 0)),
                  vec, vec, _const_spec((1, d)),
                  _const_spec(w_qkv.shape), _const_spec(w_ab.shape), _const_spec(w_pool.shape),
                  _const_spec(conv_w.shape), _const_spec(alog.shape), _const_spec(dtb.shape)],
        out_specs=[tok(DN_WIDTH), tok(DN_WIDTH), tok(DN_WIDTH), tok(LANES), tok(POOL_WIDTH)],
        out_shape=[jax.ShapeDtypeStruct((b, s, DN_WIDTH), F32)] * 3
        + [jax.ShapeDtypeStruct((b, s, LANES), F32), jax.ShapeDtypeStruct((b, s, POOL_WIDTH), F32)],
        compiler_params=_params(("parallel", "parallel")),
        name="proj_in",
    )(x, x, x, shift, scale, norm_g, w_qkv, w_ab, w_pool, conv_w, alog, dtb)


def _prefix_sum_rows(x):
    n = x.shape[0]
    r = lax.broadcasted_iota(jnp.int32, x.shape, 0)
    s = 1
    while s < n:
        x = x + jnp.where(r >= s, pltpu.roll(x, s, axis=0), 0.0)
        s *= 2
    return x


def _mm(a, b):
    return jnp.dot(a, b, preferred_element_type=F32)


def _scan_kernel(qf_ref, kf_ref, vf_ref, gf_ref, qb_ref, kb_ref, vb_ref, gbk_ref, s0_ref,
                 of_ref, ob_ref, s_ref, msk_ref, tri_ref, *, c):
    nlev = msk_ref.shape[0] - 1
    i = pl.program_id(1)

    @pl.when(i == 0)
    def _():
        s_ref[...] = s0_ref[...]
        row = lax.broadcasted_iota(jnp.int32, (c, c), 0)
        col = lax.broadcasted_iota(jnp.int32, (c, c), 1)

        def same(sh):
            return jnp.right_shift(row, sh) == jnp.right_shift(col, sh)

        one = jnp.ones((c, c), F32)
        zero = jnp.zeros((c, c), F32)
        msk_ref[0] = jnp.where(same(1), jnp.where(row == col, zero, one), zero).astype(BF16)
        for j in range(1, nlev):
            msk_ref[j] = jnp.where(same(j + 1), jnp.where(same(j), zero, one), zero).astype(BF16)
        msk_ref[nlev] = jnp.where(row == col, one, zero).astype(BF16)
        tri_ref[0] = jnp.where(row >= col, 0.0, NEG_BIG)
        tri_ref[1] = jnp.where(row <= col, 0.0, NEG_BIG)

    nblk = qf_ref.shape[1] // c
    units = []
    for d, (q_ref, k_ref, v_ref, gb_ref, o_ref) in enumerate(
            ((qf_ref, kf_ref, vf_ref, gf_ref, of_ref), (qb_ref, kb_ref, vb_ref, gbk_ref, ob_ref))):
        for g in range(nblk):
            rs = slice(g * c, (g + 1) * c)
            gb = gb_ref[0, rs, :]
            cs = _prefix_sum_rows(gb)
            tot = cs[c - 1:c, :]
            if d == 1:
                cs = tot - cs + gb
            cs_t = cs.T
            for h in range(DN_HEADS):
                gi = d * DN_HEADS + h
                bi = 2 * DN_HEADS + gi
                hs = slice(h * DN_HEAD_DIM, (h + 1) * DN_HEAD_DIM)
                units.append(dict(d=d, h=h, g=g, rs=rs, hs=hs, o_ref=o_ref, gc=cs[:, gi:gi + 1],
                                  gc_row=cs_t[gi:gi + 1, :], gl=tot[:, gi:gi + 1], beta=gb[:, bi:bi + 1],
                                  q=q_ref[0, rs, hs], k=k_ref[0, rs, hs], v=v_ref[0, rs, hs]))
    nu = range(len(units))
    dec = [jnp.exp((u["gc"] - u["gc_row"]) + tri_ref[u["d"]]) for u in units]
    egc = [jnp.exp(u["gc"]) for u in units]
    k16 = [u["k"].astype(BF16) for u in units]
    kbeta = [u["k"] * u["beta"] for u in units]
    a16 = [(_dot_nt(kbeta[j], k16[j]) * dec[j]).astype(BF16) for j in nu]
    aqk16 = [(_dot_nt(units[j]["q"], k16[j]) * dec[j]).astype(BF16) for j in nu]
    eye16 = msk_ref[nlev]
    t16 = [eye16 - a16[j] * msk_ref[0] for j in nu]
    for lev in range(1, nlev):
        p16 = [_mm(t16[j], a16[j] * msk_ref[lev]).astype(BF16) for j in nu]
        t16 = [t16[j] - _mm(p16[j], t16[j]).astype(BF16) for j in nu]
    rhs = [jnp.concatenate([units[j]["v"] * units[j]["beta"], kbeta[j] * egc[j]], axis=1) for j in nu]
    sol = [rhs[j] + _mm(t16[j] - eye16, rhs[j].astype(BF16)) for j in nu]
    qd16 = [(units[j]["q"] * egc[j]).astype(BF16) for j in nu]
    kst16 = [(units[j]["k"] * jnp.exp(units[j]["gl"] - units[j]["gc"])).astype(BF16) for j in nu]
    state = {(d, h): s_ref[0, d, h] for d in range(2) for h in range(DN_HEADS)}
    for step in range(nblk):
        cur = [j for j in nu if units[j]["g"] == (step if units[j]["d"] == 0 else nblk - 1 - step)]
        s16 = {j: state[units[j]["d"], units[j]["h"]].astype(BF16) for j in cur}
        vn16 = {j: (sol[j][:, :DN_HEAD_DIM] - _mm(sol[j][:, DN_HEAD_DIM:].astype(BF16), s16[j])).astype(BF16)
                for j in cur}
        for j in cur:
            u = units[j]
            u["o_ref"][0, u["rs"], u["hs"]] = _mm(qd16[j], s16[j]) + _mm(aqk16[j], vn16[j])
        for j in cur:
            u = units[j]
            state[u["d"], u["h"]] = (state[u["d"], u["h"]] * jnp.exp(u["gl"])
                                     + lax.dot_general(kst16[j], vn16[j], (((0,), (0,)), ((), ())),
                                                       preferred_element_type=F32))
    for (d, h), val in state.items():
        s_ref[0, d, h] = val


def _scan_call(q, k, v, gb, s0):
    b, s, _ = q.shape
    c = min(SCAN_CHUNK, s)
    blk = min(SCAN_BLOCK, s)
    nb = s // blk
    nlev = int(math.log2(c))
    fwd = lambda n: pl.BlockSpec((1, blk, n), lambda bi, i: (bi, i, 0))
    bwd = lambda n: pl.BlockSpec((1, blk, n), lambda bi, i: (bi, nb - 1 - i, 0))
    st = pl.BlockSpec((1, 2, DN_HEADS, DN_HEAD_DIM, DN_HEAD_DIM), lambda bi, i: (bi, 0, 0, 0, 0))
    return pl.pallas_call(
        functools.partial(_scan_kernel, c=c),
        grid=(b, nb),
        in_specs=[fwd(DN_WIDTH), fwd(DN_WIDTH), fwd(DN_WIDTH), fwd(LANES),
                  bwd(DN_WIDTH), bwd(DN_WIDTH), bwd(DN_WIDTH), bwd(LANES), st],
        out_specs=[fwd(DN_WIDTH), bwd(DN_WIDTH), st],
        out_shape=[jax.ShapeDtypeStruct((b, s, DN_WIDTH), F32)] * 2
        + [jax.ShapeDtypeStruct(s0.shape, F32)],
        scratch_shapes=[pltpu.VMEM((nlev + 1, c, c), BF16), pltpu.VMEM((2, c, c), F32)],
        compiler_params=_params(("parallel", "arbitrary")),
        name="scan",
    )(q, k, v, gb, q, k, v, gb, s0)


def _pool_kernel(x_ref, o_ref, pad_ref, *, rows, width, tile):
    s = rows * width
    halo = SUBLANES * width
    ext_n = tile + 2 * halo
    wshift = int(math.log2(width))
    for slab in range(POOL_WIDTH // LANES):
        w_small, w_large = POOL_WINDOWS[2 * slab], POOL_WINDOWS[2 * slab + 1]
        lanes = slice(slab * LANES, (slab + 1) * LANES)
        pad_ref[0:halo, :] = jnp.zeros((halo, LANES), F32)
        pad_ref[halo + s:, :] = jnp.zeros((halo, LANES), F32)
        pad_ref[halo:halo + s, :] = x_ref[0, :, lanes]

        def offsets(w):
            return range(-(w // 2), w - (w // 2))

        def body(ti, carry):
            start = pl.multiple_of(ti * tile, SUBLANES)
            ext = pad_ref[pl.ds(start, ext_n), :]
            centre = ext[halo:halo + tile]
            t = start + lax.broadcasted_iota(jnp.int32, (tile, LANES), 0)
            lane = lax.broadcasted_iota(jnp.int32, (tile, LANES), 1)
            grow = jnp.right_shift(t, wshift)
            gcol = jnp.bitwise_and(t, width - 1)
            small_lane = lane < POOL_GROUP_DIM

            def count(pos, w, n):
                lo = w // 2
                hi = w - 1 - lo
                return (jnp.minimum(pos + hi + 1, n) - jnp.maximum(pos - lo, 0)).astype(F32)

            acc_s = jnp.zeros((tile, LANES), F32)
            acc_l = jnp.zeros((tile, LANES), F32)
            for r in offsets(w_large):
                off = r * width
                if off % SUBLANES == 0:
                    term = ext[halo + off:halo + off + tile]
                else:
                    term = pltpu.roll(ext, (-off) % ext_n, axis=0)[halo:halo + tile]
                acc_l = acc_l + term
                if r in offsets(w_small):
                    acc_s = acc_s + term
            m1 = jnp.where(small_lane, acc_s / count(grow, w_small, rows),
                           acc_l / count(grow, w_large, rows))
            acc_s = jnp.zeros((tile, LANES), F32)
            acc_l = jnp.zeros((tile, LANES), F32)
            for cc in offsets(w_large):
                if abs(cc) >= width:
                    continue
                if cc == 0:
                    term = m1
                else:
                    valid = (gcol >= -cc) if cc < 0 else (gcol < width - cc)
                    term = jnp.where(valid, pltpu.roll(m1, (-cc) % tile, axis=0), 0.0)
                acc_l = acc_l + term
                if cc in offsets(w_small):
                    acc_s = acc_s + term
            m2 = jnp.where(small_lane, acc_s / count(gcol, w_small, width),
                           acc_l / count(gcol, w_large, width))
            o_ref[0, pl.ds(start, tile), lanes] = m2 - centre
            return carry

        lax.fori_loop(0, s // tile, body, 0)


def _pool_call(pin, rows, width):
    b, s, n = pin.shape
    tile = min(max(POOL_TILE_ROWS * width, TOKEN_TILE), s)
    halo = SUBLANES * width
    blk = pl.BlockSpec((1, s, n), lambda bi: (bi, 0, 0))
    return pl.pallas_call(
        functools.partial(_pool_kernel, rows=rows, width=width, tile=tile),
        grid=(b,),
        in_specs=[blk],
        out_specs=blk,
        out_shape=jax.ShapeDtypeStruct((b, s, n), F32),
        scratch_shapes=[pltpu.VMEM((s + 2 * halo, LANES), F32)],
        compiler_params=_params(("parallel",)),
        name="pool",
    )(pin)


def _mix_kernel(x_ref, xp_ref, xn_ref, sh_ref, sc_ref, gt_ref, g_ref, of_ref, ob_ref, pd_ref,
                wz_ref, wsc_ref, wgate_ref, dng_ref, wpool_ref, pscale_ref, scw_ref,
                wa_ref, wb_ref, wc_ref, wo_ref, out_ref):
    i = pl.program_id(1)
    last = pl.num_programs(1) - 1
    tm, d = x_ref.shape[1], x_ref.shape[2]
    xm = x_ref[0]
    h_ext = _halo_modulate(x_ref, xp_ref, xn_ref, g_ref[...], sh_ref[0], sc_ref[0])
    hb = h_ext[SUBLANES:SUBLANES + tm].astype(BF16)
    z = _dot(hb, wz_ref[...])
    o = of_ref[0] + ob_ref[0]
    parts = []
    for h in range(DN_HEADS):
        lo = h * DN_HEAD_DIM
        parts.append(_rms(o[:, lo:lo + DN_HEAD_DIM]) * dng_ref[...] * _silu(z[:, lo:lo + DN_HEAD_DIM]))
    y_a = _dot(jnp.concatenate(parts, axis=1), wa_ref[...])
    y_b = _dot(_dot(pd_ref[0], wpool_ref[...]) * pscale_ref[...], wb_ref[...])
    psc = _dot(h_ext, wsc_ref[...])
    conv = _conv3_ext(psc[:, 2 * SC_WIDTH:] * psc[:, :SC_WIDTH], scw_ref,
                      jnp.where(i > 0, 1.0, 0.0), jnp.where(i < last, 1.0, 0.0))
    y_c = _dot(psc[SUBLANES:SUBLANES + tm, SC_WIDTH:2 * SC_WIDTH] * conv, wc_ref[...])
    gates = jax.nn.sigmoid(_dot(hb, wgate_ref[...]))
    y = gates[:, :d] * y_a + gates[:, d:2 * d] * y_b + gates[:, 2 * d:] * y_c
    out_ref[0] = xm + gt_ref[0] * _dot(y, wo_ref[...])


def _mix_call(x, shift, scale, gate, norm_g, o_f, o_b, pd, consts):
    b, s, d = x.shape
    tm = min(TOKEN_TILE, s)
    nt = s // tm
    r8 = tm // SUBLANES
    nb8 = s // SUBLANES
    tok = lambda n: pl.BlockSpec((1, tm, n), lambda bi, i: (bi, i, 0))
    vec = pl.BlockSpec((1, 1, d), lambda bi, i: (bi, 0, 0))
    return pl.pallas_call(
        _mix_kernel,
        grid=(b, nt),
        in_specs=[tok(d),
                  pl.BlockSpec((1, SUBLANES, d), lambda bi, i: (bi, jnp.maximum(i * r8 - 1, 0), 0)),
                  pl.BlockSpec((1, SUBLANES, d), lambda bi, i: (bi, jnp.minimum((i + 1) * r8, nb8 - 1), 0)),
                  vec, vec, vec, _const_spec((1, d)),
                  tok(DN_WIDTH), tok(DN_WIDTH), tok(POOL_WIDTH)]
        + [_const_spec(w.shape) for w in consts],
        out_specs=tok(d),
        out_shape=jax.ShapeDtypeStruct((b, s, d), F32),
        compiler_params=_params(("parallel", "parallel")),
        name="mix",
    )(x, x, x, shift, scale, gate, norm_g, o_f, o_b, pd, *consts)


def _ffn_kernel(x_ref, sh_ref, sc_ref, gt_ref, g_ref, wgu_ref, wdown_ref, gf_ref, out_ref, *, final):
    dff = wdown_ref.shape[0]
    xm = x_ref[0]
    hb = _modulate(xm, g_ref[...], sh_ref[0], sc_ref[0]).astype(BF16)
    gu = _dot(hb, wgu_ref[...])
    act = _silu(gu[:, :dff]) * gu[:, dff:]
    r = xm + gt_ref[0] * _dot(act, wdown_ref[...])
    if final:
        r = _rms(r) * gf_ref[...]
    out_ref[0] = r


def _ffn_call(x, shift, scale, gate, norm_g, w_gu, w_down, final_g, final):
    b, s, d = x.shape
    tm = min(TOKEN_TILE, s)
    tok = pl.BlockSpec((1, tm, d), lambda bi, i: (bi, i, 0))
    vec = pl.BlockSpec((1, 1, d), lambda bi, i: (bi, 0, 0))
    return pl.pallas_call(
        functools.partial(_ffn_kernel, final=final),
        grid=(b, s // tm),
        in_specs=[tok, vec, vec, vec, _const_spec((1, d)), _const_spec(w_gu.shape),
                  _const_spec(w_down.shape), _const_spec((1, d))],
        out_specs=tok,
        out_shape=jax.ShapeDtypeStruct((b, s, d), F32),
        compiler_params=_params(("parallel", "parallel")),
        name="ffn",
    )(x, shift, scale, gate, norm_g, w_gu, w_down, final_g)


def _block_diag(w):
    g, ci, co = w.shape
    out = jnp.zeros((g * ci, g * co), w.dtype)
    for j in range(g):
        out = out.at[j * ci:(j + 1) * ci, j * co:(j + 1) * co].set(w[j])
    return out


def kernel(x, c, ctx, c_ctx, w_ada, b_ada, norm1_g, norm2_g, w_in, dn_conv_w, dn_a_log, dn_dt_bias,
           dn_norm_g, pool_w, pool_scale, sc_conv_w, w_br_a, w_br_b, w_br_c, w_o, w_gu, w_down,
           final_norm_g):
    bn, seq, d = x.shape
    depth = w_ada.shape[0]
    rows = seq // GRID_W
    off_z = 3 * DN_WIDTH
    off_a = off_z + DN_WIDTH
    off_pool = off_a + 4 * DN_HEADS
    off_sc = off_pool + POOL_WIDTH
    off_gate = off_sc + 3 * SC_WIDTH

    n_c = -(-(bn + 1) // SUBLANES) * SUBLANES
    cs = jnp.concatenate([c, c_ctx[None], jnp.zeros((n_c - bn - 1, d), F32)], axis=0)
    mod = _ada_call(cs, w_ada, b_ada)

    s0 = jnp.zeros((bn, 2, DN_HEADS, DN_HEAD_DIM, DN_HEAD_DIM), F32)
    final_g = final_norm_g.reshape(1, d)
    for l in range(depth):
        wl = w_in[l]
        w_qkv = wl[:, :off_z].astype(BF16)
        w_z = wl[:, off_z:off_a].astype(BF16)
        w_ab = jnp.pad(wl[:, off_a:off_pool], ((0, 0), (0, LANES - 4 * DN_HEADS))).astype(BF16)
        w_pool = wl[:, off_pool:off_sc].astype(BF16)
        w_sc = wl[:, off_sc:off_gate].astype(BF16)
        w_gate = wl[:, off_gate:].astype(BF16)
        alog = jnp.pad(dn_a_log[l].reshape(1, -1), ((0, 0), (0, LANES - 2 * DN_HEADS)))
        dtb = jnp.pad(dn_dt_bias[l].reshape(1, -1), ((0, 0), (0, LANES - 2 * DN_HEADS)))
        n1 = norm1_g[l].reshape(1, d)
        n2 = norm2_g[l].reshape(1, d)
        mix_consts = (w_z, w_sc, w_gate, dn_norm_g[l].reshape(1, -1),
                      _block_diag(pool_w[l]).astype(BF16), pool_scale[l].reshape(1, -1), sc_conv_w[l],
                      w_br_a[l].astype(BF16), w_br_b[l].astype(BF16), w_br_c[l].astype(BF16),
                      w_o[l].astype(BF16))
        wgu = w_gu[l].astype(BF16)
        wdn = w_down[l].astype(BF16)
        lat = [mod[l, :bn, j * d:(j + 1) * d][:, None, :] for j in range(6)]
        cxm = [jnp.broadcast_to(mod[l, bn:bn + 1, j * d:(j + 1) * d][None], (bn, 1, d)) for j in range(6)]

        q, k, v, gb, pin = _proj_call(ctx, cxm[0], cxm[1], n1, w_qkv, w_ab, w_pool, dn_conv_w[l], alog, dtb)
        o_f, o_b, s_ctx = _scan_call(q, k, v, gb, s0)
        if l < depth - 1:
            pd = _pool_call(pin, ctx.shape[1], 1)
            ctx = _mix_call(ctx, cxm[0], cxm[1], cxm[2], n1, o_f, o_b, pd, mix_consts)
            ctx = _ffn_call(ctx, cxm[3], cxm[4], cxm[5], n2, wgu, wdn, final_g, False)

        q, k, v, gb, pin = _proj_call(x, lat[0], lat[1], n1, w_qkv, w_ab, w_pool, dn_conv_w[l], alog, dtb)
        o_f, o_b, _ = _scan_call(q, k, v, gb, s_ctx)
        pd = _pool_call(pin, rows, GRID_W)
        x = _mix_call(x, lat[0], lat[1], lat[2], n1, o_f, o_b, pd, mix_consts)
        x = _ffn_call(x, lat[3], lat[4], lat[5], n2, wgu, wdn, final_g, l == depth - 1)
    return x
```

```python
import functools
import math

import jax
import jax.numpy as jnp
from jax import lax
from jax.experimental import pallas as pl
from jax.experimental.pallas import tpu as pltpu

F32 = jnp.float32
BF16 = jnp.bfloat16

EPS = 1e-6
GRID_W = 64
DN_HEADS = 4
DN_HEAD_DIM = 128
DN_WIDTH = DN_HEADS * DN_HEAD_DIM
POOL_WINDOWS = (2, 4, 8, 16)
POOL_GROUP_DIM = 64
POOL_WIDTH = POOL_GROUP_DIM * len(POOL_WINDOWS)
SC_WIDTH = 256
N_BRANCH = 3

LANES = 128
SUBLANES = 8
SCAN_CHUNK = 128
SCAN_BLOCK = 512
TOKEN_TILE = 512
VMEM_LIMIT = 56 * 1024 * 1024
NEG_BIG = -1e30


def _dot(a, b):
    return jnp.dot(a.astype(BF16), b.astype(BF16), preferred_element_type=F32)


def _dot_nt(a, b):
    return lax.dot_general(a.astype(BF16), b.astype(BF16), (((1,), (1,)), ((), ())),
                           preferred_element_type=F32)


def _dot_tn(a, b):
    return lax.dot_general(a.astype(BF16), b.astype(BF16), (((0,), (0,)), ((), ())),
                           preferred_element_type=F32)


def _silu(x):
    return x * jax.nn.sigmoid(x)


def _rms(x):
    return x * lax.rsqrt(jnp.mean(x * x, axis=-1, keepdims=True) + EPS)


def _modulate(x, g, shift, scale):
    return (_rms(x) * g) * (1.0 + scale) + shift


def _halo_modulate(x_ref, xp_ref, xn_ref, g, shift, scale):
    x_ext = jnp.concatenate([xp_ref[0], x_ref[0], xn_ref[0]], axis=0)
    return _modulate(x_ext, g, shift, scale)


def _conv3_ext(p_ext, w_ref, keep_prev, keep_next):
    n = p_ext.shape[0] - 2 * SUBLANES
    pe = jnp.concatenate([p_ext[:SUBLANES] * keep_prev, p_ext[SUBLANES:SUBLANES + n],
                          p_ext[SUBLANES + n:] * keep_next], axis=0)
    return (pe[SUBLANES - 1:SUBLANES - 1 + n] * w_ref[0:1] + pe[SUBLANES:SUBLANES + n] * w_ref[1:2]
            + pe[SUBLANES + 1:SUBLANES + 1 + n] * w_ref[2:3])


def _const_spec(shape):
    nd = len(shape)
    return pl.BlockSpec(shape, lambda *_: (0,) * nd, pipeline_mode=pl.Buffered(1))


def _params(sem):
    return pltpu.CompilerParams(dimension_semantics=sem, vmem_limit_bytes=VMEM_LIMIT)


def _ada_kernel(c_ref, w_ref, b_ref, o_ref):
    a = _silu(c_ref[...])
    w = w_ref[0]
    a_hi, w_hi = a.astype(BF16), w.astype(BF16)
    a_lo = (a - a_hi.astype(F32)).astype(BF16)
    w_lo = (w - w_hi.astype(F32)).astype(BF16)
    mm = functools.partial(jnp.dot, preferred_element_type=F32)
    o_ref[0] = (mm(a_hi, w_hi) + (mm(a_lo, w_hi) + mm(a_hi, w_lo))) + b_ref[0]


def _ada_call(cs, w_ada, b_ada):
    nl, d, n6 = w_ada.shape
    rows = cs.shape[0]
    tn = n6 // 4
    return pl.pallas_call(
        _ada_kernel,
        grid=(nl, n6 // tn),
        in_specs=[pl.BlockSpec((rows, d), lambda l, j: (0, 0)),
                  pl.BlockSpec((1, d, tn), lambda l, j: (l, 0, j)),
                  pl.BlockSpec((1, 1, tn), lambda l, j: (l, 0, j))],
        out_specs=pl.BlockSpec((1, rows, tn), lambda l, j: (l, 0, j)),
        out_shape=jax.ShapeDtypeStruct((nl, rows, n6), F32),
        compiler_params=_params(("parallel", "parallel")),
        name="ada",
    )(cs, w_ada, b_ada.reshape(nl, 1, n6))


def _proj_kernel(x_ref, xp_ref, xn_ref, sh_ref, sc_ref, g_ref, wqkv_ref, wab_ref, wpool_ref,
                 cw_ref, alog_ref, dtb_ref, q_ref, k_ref, v_ref, gb_ref, pin_ref):
    i = pl.program_id(1)
    last = pl.num_programs(1) - 1
    tm = x_ref.shape[1]
    h_ext = _halo_modulate(x_ref, xp_ref, xn_ref, g_ref[...], sh_ref[0], sc_ref[0])
    hb = h_ext[SUBLANES:SUBLANES + tm].astype(BF16)
    p_ext = _dot(h_ext, wqkv_ref[...])
    a = _silu(_conv3_ext(p_ext, cw_ref, jnp.where(i > 0, 1.0, 0.0), jnp.where(i < last, 1.0, 0.0)))
    for h in range(DN_HEADS):
        lo = h * DN_HEAD_DIM
        qh = a[:, lo:lo + DN_HEAD_DIM]
        kh = a[:, DN_WIDTH + lo:DN_WIDTH + lo + DN_HEAD_DIM]
        q_ref[0, :, lo:lo + DN_HEAD_DIM] = (
            qh * lax.rsqrt(jnp.sum(qh * qh, axis=-1, keepdims=True) + EPS) * (DN_HEAD_DIM ** -0.5))
        k_ref[0, :, lo:lo + DN_HEAD_DIM] = (
            kh * lax.rsqrt(jnp.sum(kh * kh, axis=-1, keepdims=True) + EPS))
    v_ref[0] = a[:, 2 * DN_WIDTH:]
    pab = _dot(hb, wab_ref[...])
    z = pab + dtb_ref[...]
    softplus = jnp.maximum(z, 0.0) + jnp.log(1.0 + jnp.exp(-jnp.abs(z)))
    gdec = -jnp.exp(alog_ref[...]) * softplus
    lane = lax.broadcasted_iota(jnp.int32, pab.shape, 1)
    gb_ref[0] = jnp.where(lane < 2 * DN_HEADS, gdec,
                          jnp.where(lane < 4 * DN_HEADS, jax.nn.sigmoid(pab), 0.0))
    pin_ref[0] = _dot(hb, wpool_ref[...])


def _proj_call(x, shift, scale, norm_g, w_qkv, w_ab, w_pool, conv_w, alog, dtb):
    b, s, d = x.shape
    tm = min(TOKEN_TILE, s)
    nt = s // tm
    r8 = tm // SUBLANES
    nb8 = s // SUBLANES
    tok = lambda n: pl.BlockSpec((1, tm, n), lambda bi, i: (bi, i, 0))
    vec = pl.BlockSpec((1, 1, d), lambda bi, i: (bi, 0, 0))
    return pl.pallas_call(
        _proj_kernel,
        grid=(b, nt),
        in_specs=[tok(d),
                  pl.BlockSpec((1, SUBLANES, d), lambda bi, i: (bi, jnp.maximum(i * r8 - 1, 0), 0)),
                  pl.BlockSpec((1, SUBLANES, d), lambda bi, i: (bi, jnp.minimum((i + 1) * r8, nb8 - 1), 0)),
                  vec, vec, _const_spec((1, d)),
                  _const_spec(w_qkv.shape), _const_spec(w_ab.shape), _const_spec(w_pool.shape),
                  _const_spec(conv_w.shape), _const_spec(alog.shape), _const_spec(dtb.shape)],
        out_specs=[tok(DN_WIDTH), tok(DN_WIDTH), tok(DN_WIDTH), tok(LANES), tok(POOL_WIDTH)],
        out_shape=[jax.ShapeDtypeStruct((b, s, DN_WIDTH), F32)] * 3
        + [jax.ShapeDtypeStruct((b, s, LANES), F32), jax.ShapeDtypeStruct((b, s, POOL_WIDTH), F32)],
        compiler_params=_params(("parallel", "parallel")),
        name="proj_in",
    )(x, x, x, shift, scale, norm_g, w_qkv, w_ab, w_pool, conv_w, alog, dtb)


def _prefix_sum_rows(x):
    n = x.shape[0]
    r = lax.broadcasted_iota(jnp.int32, x.shape, 0)
    s = 1
    while s < n:
        x = x + jnp.where(r >= s, pltpu.roll(x, s, axis=0), 0.0)
        s *= 2
    return x


def _mm(a, b):
    return jnp.dot(a, b, preferred_element_type=F32)


def _scan_kernel(qf_ref, kf_ref, vf_ref, gf_ref, qb_ref, kb_ref, vb_ref, gbk_ref, s0_ref,
                 of_ref, ob_ref, s_ref, msk_ref, tri_ref, *, c):
    nlev = msk_ref.shape[0] - 1
    i = pl.program_id(1)

    @pl.when(i == 0)
    def _():
        s_ref[...] = s0_ref[...]
        row = lax.broadcasted_iota(jnp.int32, (c, c), 0)
        col = lax.broadcasted_iota(jnp.int32, (c, c), 1)

        def same(sh):
            return jnp.right_shift(row, sh) == jnp.right_shift(col, sh)

        one = jnp.ones((c, c), F32)
        zero = jnp.zeros((c, c), F32)
        msk_ref[0] = jnp.where(same(1), jnp.where(row == col, zero, one), zero).astype(BF16)
        for j in range(1, nlev):
            msk_ref[j] = jnp.where(same(j + 1), jnp.where(same(j), zero, one), zero).astype(BF16)
        msk_ref[nlev] = jnp.where(row == col, one, zero).astype(BF16)
        tri_ref[0] = jnp.where(row >= col, 0.0, NEG_BIG)
        tri_ref[1] = jnp.where(row <= col, 0.0, NEG_BIG)

    nblk = qf_ref.shape[1] // c
    units = []
    for d, (q_ref, k_ref, v_ref, gb_ref, o_ref) in enumerate(
            ((qf_ref, kf_ref, vf_ref, gf_ref, of_ref), (qb_ref, kb_ref, vb_ref, gbk_ref, ob_ref))):
        for g in range(nblk):
            rs = slice(g * c, (g + 1) * c)
            gb = gb_ref[0, rs, :]
            cs = _prefix_sum_rows(gb)
            tot = cs[c - 1:c, :]
            if d == 1:
                cs = tot - cs + gb
            cs_t = cs.T
            for h in range(DN_HEADS):
                gi = d * DN_HEADS + h
                bi = 2 * DN_HEADS + gi
                hs = slice(h * DN_HEAD_DIM, (h + 1) * DN_HEAD_DIM)
                units.append(dict(d=d, h=h, g=g, rs=rs, hs=hs, o_ref=o_ref, gc=cs[:, gi:gi + 1],
                                  gc_row=cs_t[gi:gi + 1, :], gl=tot[:, gi:gi + 1], beta=gb[:, bi:bi + 1],
                                  q=q_ref[0, rs, hs], k=k_ref[0, rs, hs], v=v_ref[0, rs, hs]))
    nu = range(len(units))
    dec = [jnp.exp((u["gc"] - u["gc_row"]) + tri_ref[u["d"]]) for u in units]
    egc = [jnp.exp(u["gc"]) for u in units]
    k16 = [u["k"].astype(BF16) for u in units]
    kbeta = [u["k"] * u["beta"] for u in units]
    kq = [_dot_nt(jnp.concatenate([kbeta[j].astype(BF16), units[j]["q"].astype(BF16)], axis=0), k16[j])
          for j in nu]
    a16 = [(kq[j][:c] * dec[j]).astype(BF16) for j in nu]
    aqk16 = [(kq[j][c:] * dec[j]).astype(BF16) for j in nu]
    eye16 = msk_ref[nlev]
    t16 = [eye16 - a16[j] * msk_ref[0] for j in nu]
    for lev in range(1, nlev):
        p16 = [_mm(t16[j], a16[j] * msk_ref[lev]).astype(BF16) for j in nu]
        t16 = [t16[j] - _mm(p16[j], t16[j]).astype(BF16) for j in nu]
    rhs = [jnp.concatenate([units[j]["v"] * units[j]["beta"], kbeta[j] * egc[j]], axis=1) for j in nu]
    sol = [rhs[j] + _mm(t16[j] - eye16, rhs[j].astype(BF16)) for j in nu]
    qd16 = [(units[j]["q"] * egc[j]).astype(BF16) for j in nu]
    kst16 = [(units[j]["k"] * jnp.exp(units[j]["gl"] - units[j]["gc"])).astype(BF16) for j in nu]
    state = {(d, h): s_ref[0, d, h] for d in range(2) for h in range(DN_HEADS)}
    for step in range(nblk):
        cur = [j for j in nu if units[j]["g"] == (step if units[j]["d"] == 0 else nblk - 1 - step)]
        s16 = {j: state[units[j]["d"], units[j]["h"]].astype(BF16) for j in cur}
        ws = {j: _mm(jnp.concatenate([sol[j][:, DN_HEAD_DIM:].astype(BF16), qd16[j]], axis=0), s16[j])
              for j in cur}
        vn16 = {j: (sol[j][:, :DN_HEAD_DIM] - ws[j][:c]).astype(BF16) for j in cur}
        for j in cur:
            u = units[j]
            u["o_ref"][0, u["rs"], u["hs"]] = ws[j][c:] + _mm(aqk16[j], vn16[j])
        for j in cur:
            u = units[j]
            state[u["d"], u["h"]] = (state[u["d"], u["h"]] * jnp.exp(u["gl"])
                                     + lax.dot_general(kst16[j], vn16[j], (((0,), (0,)), ((), ())),
                                                       preferred_element_type=F32))
    for (d, h), val in state.items():
        s_ref[0, d, h] = val


def _scan_call(q, k, v, gb, s0):
    b, s, _ = q.shape
    c = min(SCAN_CHUNK, s)
    blk = min(SCAN_BLOCK, s)
    nb = s // blk
    nlev = int(math.log2(c))
    fwd = lambda n: pl.BlockSpec((1, blk, n), lambda bi, i: (bi, i, 0))
    bwd = lambda n: pl.BlockSpec((1, blk, n), lambda bi, i: (bi, nb - 1 - i, 0))
    st = pl.BlockSpec((1, 2, DN_HEADS, DN_HEAD_DIM, DN_HEAD_DIM), lambda bi, i: (bi, 0, 0, 0, 0))
    return pl.pallas_call(
        functools.partial(_scan_kernel, c=c),
        grid=(b, nb),
        in_specs=[fwd(DN_WIDTH), fwd(DN_WIDTH), fwd(DN_WIDTH), fwd(LANES),
                  bwd(DN_WIDTH), bwd(DN_WIDTH), bwd(DN_WIDTH), bwd(LANES), st],
        out_specs=[fwd(DN_WIDTH), bwd(DN_WIDTH), st],
        out_shape=[jax.ShapeDtypeStruct((b, s, DN_WIDTH), F32)] * 2
        + [jax.ShapeDtypeStruct(s0.shape, F32)],
        scratch_shapes=[pltpu.VMEM((nlev + 1, c, c), BF16), pltpu.VMEM((2, c, c), F32)],
        compiler_params=_params(("parallel", "arbitrary")),
        name="scan",
    )(q, k, v, gb, q, k, v, gb, s0)


def _window_offsets(w):
    return -(w // 2), w - 1 - (w // 2)


def _pool_inv_counts(rows, width):
    t = jnp.arange(rows * width, dtype=jnp.int32)
    r, c = t // width, t % width
    cols = []
    for w in POOL_WINDOWS:
        lo, hi = _window_offsets(w)
        cnt_r = jnp.minimum(r + hi + 1, rows) - jnp.maximum(r + lo, 0)
        cnt_c = jnp.minimum(c + hi + 1, width) - jnp.maximum(c + lo, 0)
        inv = 1.0 / (cnt_r * cnt_c).astype(F32)
        cols.append(jnp.broadcast_to(inv[:, None], (rows * width, POOL_GROUP_DIM)))
    return jnp.concatenate(cols, axis=1)


def _window_sums(x, unit, lev_small):
    n = x.shape[0]

    def shifted(a, k):
        return pltpu.roll(a, (k * unit) % n, axis=0)

    sums = {1: x + shifted(x, 1)}
    for lev in range(2, lev_small + 2):
        step = 2 ** (lev - 2)
        sums[lev] = shifted(sums[lev - 1], step) + shifted(sums[lev - 1], -step)
    return sums[lev_small], sums[lev_small + 1]


def _pool_tile(ext, inv_cnt, width):
    halo = SUBLANES * width
    tm = ext.shape[0] - 2 * halo
    pad = SUBLANES
    outs = []
    for slab in range(POOL_WIDTH // LANES):
        lanes = slice(slab * LANES, (slab + 1) * LANES)
        e = ext[:, lanes]
        small_lane = lax.broadcasted_iota(jnp.int32, (tm, LANES), 1) < POOL_GROUP_DIM
        levels = int(math.log2(POOL_WINDOWS[2 * slab]))
        rs, rl = _window_sums(e, width, levels)
        m1 = jnp.where(small_lane, rs[halo:halo + tm], rl[halo:halo + tm])
        if width > 1:
            zeros = jnp.zeros((pad, LANES), F32)
            pieces = []
            for r in range(tm // width):
                pieces += [zeros, m1[r * width:(r + 1) * width], zeros]
            cs, cl = _window_sums(jnp.concatenate(pieces, axis=0), 1, levels)
            stride = width + 2 * pad
            pick = lambda a: jnp.concatenate(
                [a[r * stride + pad:r * stride + pad + width] for r in range(tm // width)], axis=0)
            m1 = jnp.where(small_lane, pick(cs), pick(cl))
        outs.append(m1 * inv_cnt[:, lanes] - e[halo:halo + tm])
    return jnp.concatenate(outs, axis=1)


def _mix_kernel(x_ref, xp_ref, xn_ref, sh_ref, sc_ref, gt_ref, g_ref, of_ref, ob_ref,
                pin_ref, pinp_ref, pinn_ref, icnt_ref,
                wz_ref, wsc_ref, wgate_ref, dng_ref, wpool_ref, pscale_ref, scw_ref,
                wa_ref, wb_ref, wc_ref, wo_ref, out_ref, *, width):
    i = pl.program_id(1)
    last = pl.num_programs(1) - 1
    tm, d = x_ref.shape[1], x_ref.shape[2]
    xm = x_ref[0]
    h_ext = _halo_modulate(x_ref, xp_ref, xn_ref, g_ref[...], sh_ref[0], sc_ref[0])
    hb = h_ext[SUBLANES:SUBLANES + tm].astype(BF16)
    z = _dot(hb, wz_ref[...])
    o = of_ref[0] + ob_ref[0]
    parts = []
    for h in range(DN_HEADS):
        lo = h * DN_HEAD_DIM
        parts.append(_rms(o[:, lo:lo + DN_HEAD_DIM]) * dng_ref[...] * _silu(z[:, lo:lo + DN_HEAD_DIM]))
    y_a = _dot(jnp.concatenate(parts, axis=1), wa_ref[...])
    keep_prev, keep_next = jnp.where(i > 0, 1.0, 0.0), jnp.where(i < last, 1.0, 0.0)
    pool_ext = jnp.concatenate([pinp_ref[0] * keep_prev, pin_ref[0], pinn_ref[0] * keep_next], axis=0)
    pd = _pool_tile(pool_ext, icnt_ref[...], width)
    y_b = _dot(_dot(pd, wpool_ref[...]) * pscale_ref[...], wb_ref[...])
    psc = _dot(h_ext, wsc_ref[...])
    conv = _conv3_ext(psc[:, 2 * SC_WIDTH:] * psc[:, :SC_WIDTH], scw_ref, keep_prev, keep_next)
    y_c = _dot(psc[SUBLANES:SUBLANES + tm, SC_WIDTH:2 * SC_WIDTH] * conv, wc_ref[...])
    gates = jax.nn.sigmoid(_dot(hb, wgate_ref[...]))
    y = gates[:, :d] * y_a + gates[:, d:2 * d] * y_b + gates[:, 2 * d:] * y_c
    out_ref[0] = xm + gt_ref[0] * _dot(y, wo_ref[...])


def _mix_call(x, shift, scale, gate, norm_g, o_f, o_b, pin, inv_cnt, width, consts):
    b, s, d = x.shape
    tm = min(TOKEN_TILE, s)
    nt = s // tm
    r8 = tm // SUBLANES
    nb8 = s // SUBLANES
    halo = SUBLANES * width
    assert tm % halo == 0 and s % tm == 0
    hb, nh = tm // halo, s // halo
    tok = lambda n: pl.BlockSpec((1, tm, n), lambda bi, i: (bi, i, 0))
    vec = pl.BlockSpec((1, 1, d), lambda bi, i: (bi, 0, 0))
    return pl.pallas_call(
        functools.partial(_mix_kernel, width=width),
        grid=(b, nt),
        in_specs=[tok(d),
                  pl.BlockSpec((1, SUBLANES, d), lambda bi, i: (bi, jnp.maximum(i * r8 - 1, 0), 0)),
                  pl.BlockSpec((1, SUBLANES, d), lambda bi, i: (bi, jnp.minimum((i + 1) * r8, nb8 - 1), 0)),
                  vec, vec, vec, _const_spec((1, d)),
                  tok(DN_WIDTH), tok(DN_WIDTH), tok(POOL_WIDTH),
                  pl.BlockSpec((1, halo, POOL_WIDTH), lambda bi, i: (bi, jnp.maximum(i * hb - 1, 0), 0)),
                  pl.BlockSpec((1, halo, POOL_WIDTH), lambda bi, i: (bi, jnp.minimum((i + 1) * hb, nh - 1), 0)),
                  pl.BlockSpec((tm, POOL_WIDTH), lambda bi, i: (i, 0))]
        + [_const_spec(w.shape) for w in consts],
        out_specs=tok(d),
        out_shape=jax.ShapeDtypeStruct((b, s, d), F32),
        compiler_params=_params(("parallel", "parallel")),
        name="mix",
    )(x, x, x, shift, scale, gate, norm_g, o_f, o_b, pin, pin, pin, inv_cnt, *consts)


def _ffn_kernel(x_ref, sh_ref, sc_ref, gt_ref, g_ref, wgu_ref, wdown_ref, gf_ref, out_ref, *, final):
    dff = wdown_ref.shape[0]
    xm = x_ref[0]
    hb = _modulate(xm, g_ref[...], sh_ref[0], sc_ref[0]).astype(BF16)
    gu = _dot(hb, wgu_ref[...])
    act = _silu(gu[:, :dff]) * gu[:, dff:]
    r = xm + gt_ref[0] * _dot(act, wdown_ref[...])
    if final:
        r = _rms(r) * gf_ref[...]
    out_ref[0] = r


def _ffn_call(x, shift, scale, gate, norm_g, w_gu, w_down, final_g, final):
    b, s, d = x.shape
    tm = min(TOKEN_TILE, s)
    tok = pl.BlockSpec((1, tm, d), lambda bi, i: (bi, i, 0))
    vec = pl.BlockSpec((1, 1, d), lambda bi, i: (bi, 0, 0))
    return pl.pallas_call(
        functools.partial(_ffn_kernel, final=final),
        grid=(b, s // tm),
        in_specs=[tok, vec, vec, vec, _const_spec((1, d)), _const_spec(w_gu.shape),
                  _const_spec(w_down.shape), _const_spec((1, d))],
        out_specs=tok,
        out_shape=jax.ShapeDtypeStruct((b, s, d), F32),
        compiler_params=_params(("parallel", "parallel")),
        name="ffn",
    )(x, shift, scale, gate, norm_g, w_gu, w_down, final_g)


def _block_diag(w):
    g, ci, co = w.shape
    out = jnp.zeros((g * ci, g * co), w.dtype)
    for j in range(g):
        out = out.at[j * ci:(j + 1) * ci, j * co:(j + 1) * co].set(w[j])
    return out


def kernel(x, c, ctx, c_ctx, w_ada, b_ada, norm1_g, norm2_g, w_in, dn_conv_w, dn_a_log, dn_dt_bias,
           dn_norm_g, pool_w, pool_scale, sc_conv_w, w_br_a, w_br_b, w_br_c, w_o, w_gu, w_down,
           final_norm_g):
    bn, seq, d = x.shape
    depth = w_ada.shape[0]
    rows = seq // GRID_W
    off_z = 3 * DN_WIDTH
    off_a = off_z + DN_WIDTH
    off_pool = off_a + 4 * DN_HEADS
    off_sc = off_pool + POOL_WIDTH
    off_gate = off_sc + 3 * SC_WIDTH

    n_c = -(-(bn + 1) // SUBLANES) * SUBLANES
    cs = jnp.concatenate([c, c_ctx[None], jnp.zeros((n_c - bn - 1, d), F32)], axis=0)
    mod = _ada_call(cs, w_ada, b_ada)

    s0 = jnp.zeros((bn, 2, DN_HEADS, DN_HEAD_DIM, DN_HEAD_DIM), F32)
    icnt_lat = _pool_inv_counts(rows, GRID_W)
    icnt_ctx = _pool_inv_counts(ctx.shape[1], 1)
    final_g = final_norm_g.reshape(1, d)
    for l in range(depth):
        wl = w_in[l]
        w_qkv = wl[:, :off_z].astype(BF16)
        w_z = wl[:, off_z:off_a].astype(BF16)
        w_ab = jnp.pad(wl[:, off_a:off_pool], ((0, 0), (0, LANES - 4 * DN_HEADS))).astype(BF16)
        w_pool = wl[:, off_pool:off_sc].astype(BF16)
        w_sc = wl[:, off_sc:off_gate].astype(BF16)
        w_gate = wl[:, off_gate:].astype(BF16)
        alog = jnp.pad(dn_a_log[l].reshape(1, -1), ((0, 0), (0, LANES - 2 * DN_HEADS)))
        dtb = jnp.pad(dn_dt_bias[l].reshape(1, -1), ((0, 0), (0, LANES - 2 * DN_HEADS)))
        n1 = norm1_g[l].reshape(1, d)
        n2 = norm2_g[l].reshape(1, d)
        mix_consts = (w_z, w_sc, w_gate, dn_norm_g[l].reshape(1, -1),
                      _block_diag(pool_w[l]).astype(BF16), pool_scale[l].reshape(1, -1), sc_conv_w[l],
                      w_br_a[l].astype(BF16), w_br_b[l].astype(BF16), w_br_c[l].astype(BF16),
                      w_o[l].astype(BF16))
        wgu = w_gu[l].astype(BF16)
        wdn = w_down[l].astype(BF16)
        lat = [mod[l, :bn, j * d:(j + 1) * d][:, None, :] for j in range(6)]
        cxm = [jnp.broadcast_to(mod[l, bn:bn + 1, j * d:(j + 1) * d][None], (bn, 1, d)) for j in range(6)]

        q, k, v, gb, pin = _proj_call(ctx, cxm[0], cxm[1], n1, w_qkv, w_ab, w_pool, dn_conv_w[l], alog, dtb)
        o_f, o_b, s_ctx = _scan_call(q, k, v, gb, s0)
        if l < depth - 1:
            ctx = _mix_call(ctx, cxm[0], cxm[1], cxm[2], n1, o_f, o_b, pin, icnt_ctx, 1, mix_consts)
            ctx = _ffn_call(ctx, cxm[3], cxm[4], cxm[5], n2, wgu, wdn, final_g, False)

        q, k, v, gb, pin = _proj_call(x, lat[0], lat[1], n1, w_qkv, w_ab, w_pool, dn_conv_w[l], alog, dtb)
        o_f, o_b, _ = _scan_call(q, k, v, gb, s_ctx)
        x = _mix_call(x, lat[0], lat[1], lat[2], n1, o_f, o_b, pin, icnt_lat, GRID_W, mix_consts)
        x = _ffn_call(x, lat[3], lat[4], lat[5], n2, wgu, wdn, final_g, l == depth - 1)
    return x
```

```python
import functools
import math

import jax
import jax.numpy as jnp
from jax import lax
from jax.experimental import pallas as pl
from jax.experimental.pallas import tpu as pltpu

F32 = jnp.float32
BF16 = jnp.bfloat16

EPS = 1e-6
GRID_W = 64
DN_HEADS = 4
DN_HEAD_DIM = 128
DN_WIDTH = DN_HEADS * DN_HEAD_DIM
POOL_WINDOWS = (2, 4, 8, 16)
POOL_GROUP_DIM = 64
POOL_WIDTH = POOL_GROUP_DIM * len(POOL_WINDOWS)
SC_WIDTH = 256
N_BRANCH = 3

LANES = 128
SUBLANES = 8
BF16_ROWS = 16
SCAN_CHUNK = 128
SCAN_BLOCK = 512
TOKEN_TILE = 512
VMEM_LIMIT = 56 * 1024 * 1024
NEG_BIG = -1e30


def _dot(a, b):
    return jnp.dot(a.astype(BF16), b.astype(BF16), preferred_element_type=F32)


def _dot_nt(a, b):
    return lax.dot_general(a.astype(BF16), b.astype(BF16), (((1,), (1,)), ((), ())),
                           preferred_element_type=F32)


def _dot_tn(a, b):
    return lax.dot_general(a.astype(BF16), b.astype(BF16), (((0,), (0,)), ((), ())),
                           preferred_element_type=F32)


def _silu(x):
    return x * jax.nn.sigmoid(x)


def _rms(x):
    return x * lax.rsqrt(jnp.mean(x * x, axis=-1, keepdims=True) + EPS)


def _modulate(x, g, shift, scale):
    return (_rms(x) * g) * (1.0 + scale) + shift


def _halo_modulate(x_ref, xp_ref, xn_ref, g, shift, scale):
    x_ext = jnp.concatenate([xp_ref[0], x_ref[0], xn_ref[0]], axis=0)
    return _modulate(x_ext, g, shift, scale)


def _conv3_ext(p_ext, w_ref, keep_prev, keep_next):
    n = p_ext.shape[0] - 2 * SUBLANES
    pe = jnp.concatenate([p_ext[:SUBLANES] * keep_prev, p_ext[SUBLANES:SUBLANES + n],
                          p_ext[SUBLANES + n:] * keep_next], axis=0)
    return (pe[SUBLANES - 1:SUBLANES - 1 + n] * w_ref[0:1] + pe[SUBLANES:SUBLANES + n] * w_ref[1:2]
            + pe[SUBLANES + 1:SUBLANES + 1 + n] * w_ref[2:3])


def _const_spec(shape):
    nd = len(shape)
    return pl.BlockSpec(shape, lambda *_: (0,) * nd, pipeline_mode=pl.Buffered(1))


def _params(sem):
    return pltpu.CompilerParams(dimension_semantics=sem, vmem_limit_bytes=VMEM_LIMIT)


def _ada_kernel(c_ref, w_ref, b_ref, o_ref):
    a = _silu(c_ref[...])
    w = w_ref[0]
    a_hi, w_hi = a.astype(BF16), w.astype(BF16)
    a_lo = (a - a_hi.astype(F32)).astype(BF16)
    w_lo = (w - w_hi.astype(F32)).astype(BF16)
    mm = functools.partial(jnp.dot, preferred_element_type=F32)
    o_ref[0] = (mm(a_hi, w_hi) + (mm(a_lo, w_hi) + mm(a_hi, w_lo))) + b_ref[0]


def _ada_call(cs, w_ada, b_ada):
    nl, d, n6 = w_ada.shape
    rows = cs.shape[0]
    tn = n6 // 4
    return pl.pallas_call(
        _ada_kernel,
        grid=(nl, n6 // tn),
        in_specs=[pl.BlockSpec((rows, d), lambda l, j: (0, 0)),
                  pl.BlockSpec((1, d, tn), lambda l, j: (l, 0, j)),
                  pl.BlockSpec((1, 1, tn), lambda l, j: (l, 0, j))],
        out_specs=pl.BlockSpec((1, rows, tn), lambda l, j: (l, 0, j)),
        out_shape=jax.ShapeDtypeStruct((nl, rows, n6), F32),
        compiler_params=_params(("parallel", "parallel")),
        name="ada",
    )(cs, w_ada, b_ada.reshape(nl, 1, n6))


def _proj_kernel(x_ref, xp_ref, xn_ref, sh_ref, sc_ref, g_ref, wqkv_ref, wab_ref, wpool_ref,
                 cw_ref, alog_ref, dtb_ref, qkv_ref, gb_ref, pin_ref):
    i = pl.program_id(1)
    last = pl.num_programs(1) - 1
    tm = x_ref.shape[1]
    h_ext = _halo_modulate(x_ref, xp_ref, xn_ref, g_ref[...], sh_ref[0], sc_ref[0])
    hb = h_ext[SUBLANES:SUBLANES + tm].astype(BF16)
    p_ext = _dot(h_ext, wqkv_ref[...])
    a = _silu(_conv3_ext(p_ext, cw_ref, jnp.where(i > 0, 1.0, 0.0), jnp.where(i < last, 1.0, 0.0)))
    for h in range(DN_HEADS):
        lo = h * DN_HEAD_DIM
        qh = a[:, lo:lo + DN_HEAD_DIM]
        kh = a[:, DN_WIDTH + lo:DN_WIDTH + lo + DN_HEAD_DIM]
        qkv_ref[0, :, lo:lo + DN_HEAD_DIM] = (
            qh * lax.rsqrt(jnp.sum(qh * qh, axis=-1, keepdims=True) + EPS) * (DN_HEAD_DIM ** -0.5)
        ).astype(BF16)
        qkv_ref[0, :, DN_WIDTH + lo:DN_WIDTH + lo + DN_HEAD_DIM] = (
            kh * lax.rsqrt(jnp.sum(kh * kh, axis=-1, keepdims=True) + EPS)).astype(BF16)
    qkv_ref[0, :, 2 * DN_WIDTH:] = a[:, 2 * DN_WIDTH:].astype(BF16)
    pab = _dot(hb, wab_ref[...])
    z = pab + dtb_ref[...]
    softplus = jnp.maximum(z, 0.0) + jnp.log(1.0 + jnp.exp(-jnp.abs(z)))
    gdec = -jnp.exp(alog_ref[...]) * softplus
    lane = lax.broadcasted_iota(jnp.int32, pab.shape, 1)
    gb_ref[0] = jnp.where(lane < 2 * DN_HEADS, gdec,
                          jnp.where(lane < 4 * DN_HEADS, jax.nn.sigmoid(pab), 0.0))
    pin_ref[0] = _dot(hb, wpool_ref[...])


def _proj_call(x, shift, scale, norm_g, w_qkv, w_ab, w_pool, conv_w, alog, dtb):
    b, s, d = x.shape
    tm = min(TOKEN_TILE, s)
    nt = s // tm
    r8 = tm // SUBLANES
    nb8 = s // SUBLANES
    tok = lambda n: pl.BlockSpec((1, tm, n), lambda bi, i: (bi, i, 0))
    vec = pl.BlockSpec((1, 1, d), lambda bi, i: (bi, 0, 0))
    return pl.pallas_call(
        _proj_kernel,
        grid=(b, nt),
        in_specs=[tok(d),
                  pl.BlockSpec((1, SUBLANES, d), lambda bi, i: (bi, jnp.maximum(i * r8 - 1, 0), 0)),
                  pl.BlockSpec((1, SUBLANES, d), lambda bi, i: (bi, jnp.minimum((i + 1) * r8, nb8 - 1), 0)),
                  vec, vec, _const_spec((1, d)),
                  _const_spec(w_qkv.shape), _const_spec(w_ab.shape), _const_spec(w_pool.shape),
                  _const_spec(conv_w.shape), _const_spec(alog.shape), _const_spec(dtb.shape)],
        out_specs=[tok(3 * DN_WIDTH), tok(LANES), tok(POOL_WIDTH)],
        out_shape=[jax.ShapeDtypeStruct((b, s, 3 * DN_WIDTH), BF16),
                   jax.ShapeDtypeStruct((b, s, LANES), F32), jax.ShapeDtypeStruct((b, s, POOL_WIDTH), F32)],
        compiler_params=_params(("parallel", "parallel")),
        name="proj_in",
    )(x, x, x, shift, scale, norm_g, w_qkv, w_ab, w_pool, conv_w, alog, dtb)


def _prefix_sum_rows(x):
    n = x.shape[0]
    r = lax.broadcasted_iota(jnp.int32, x.shape, 0)
    s = 1
    while s < n:
        x = x + jnp.where(r >= s, pltpu.roll(x, s, axis=0), 0.0)
        s *= 2
    return x


def _mm(a, b):
    return jnp.dot(a, b, preferred_element_type=F32)


def _scan_kernel(qkvf_ref, gf_ref, qkvb_ref, gbk_ref, s0_ref, of_ref, ob_ref, s_ref, msk_ref, tri_ref, *, c):
    nlev = msk_ref.shape[0] - 1
    i = pl.program_id(1)

    @pl.when(i == 0)
    def _():
        s_ref[...] = s0_ref[...]
        row = lax.broadcasted_iota(jnp.int32, (c, c), 0)
        col = lax.broadcasted_iota(jnp.int32, (c, c), 1)

        def same(sh):
            return jnp.right_shift(row, sh) == jnp.right_shift(col, sh)

        one = jnp.ones((c, c), F32)
        zero = jnp.zeros((c, c), F32)
        msk_ref[0] = jnp.where(same(1), jnp.where(row == col, zero, one), zero).astype(BF16)
        for j in range(1, nlev):
            msk_ref[j] = jnp.where(same(j + 1), jnp.where(same(j), zero, one), zero).astype(BF16)
        msk_ref[nlev] = jnp.where(row == col, one, zero).astype(BF16)
        tri_ref[0] = jnp.where(row >= col, 0.0, NEG_BIG)
        tri_ref[1] = jnp.where(row <= col, 0.0, NEG_BIG)

    nblk = qkvf_ref.shape[1] // c
    units = []
    for d, (qkv_ref, gb_ref, o_ref) in enumerate(((qkvf_ref, gf_ref, of_ref), (qkvb_ref, gbk_ref, ob_ref))):
        for g in range(nblk):
            rs = slice(g * c, (g + 1) * c)
            gb = gb_ref[0, rs, :]
            cs = _prefix_sum_rows(gb)
            tot = cs[c - 1:c, :]
            if d == 1:
                cs = tot - cs + gb
            cs_t = cs.T
            for h in range(DN_HEADS):
                gi = d * DN_HEADS + h
                bi = 2 * DN_HEADS + gi
                lo = h * DN_HEAD_DIM
                hs = slice(lo, lo + DN_HEAD_DIM)
                q16 = qkv_ref[0, rs, hs]
                k16 = qkv_ref[0, rs, DN_WIDTH + lo:DN_WIDTH + lo + DN_HEAD_DIM]
                v16 = qkv_ref[0, rs, 2 * DN_WIDTH + lo:2 * DN_WIDTH + lo + DN_HEAD_DIM]
                units.append(dict(d=d, h=h, g=g, rs=rs, hs=hs, o_ref=o_ref, gc=cs[:, gi:gi + 1],
                                  gc_row=cs_t[gi:gi + 1, :], gl=tot[:, gi:gi + 1], beta=gb[:, bi:bi + 1],
                                  q16=q16, k16=k16, q=q16.astype(F32), k=k16.astype(F32),
                                  v=v16.astype(F32)))
    nu = range(len(units))
    dec = [jnp.exp((u["gc"] - u["gc_row"]) + tri_ref[u["d"]]) for u in units]
    egc = [jnp.exp(u["gc"]) for u in units]
    k16 = [u["k16"] for u in units]
    kbeta = [u["k"] * u["beta"] for u in units]
    kq = [_dot_nt(jnp.concatenate([kbeta[j].astype(BF16), units[j]["q16"]], axis=0), k16[j])
          for j in nu]
    a16 = [(kq[j][:c] * dec[j]).astype(BF16) for j in nu]
    aqk16 = [(kq[j][c:] * dec[j]).astype(BF16) for j in nu]
    eye16 = msk_ref[nlev]
    t16 = [eye16 - a16[j] * msk_ref[0] for j in nu]
    for lev in range(1, nlev):
        b = 2 ** lev
        if b < BF16_ROWS:
            p16 = [_mm(t16[j], a16[j] * msk_ref[lev]).astype(BF16) for j in nu]
            t16 = [t16[j] - _mm(p16[j], t16[j]).astype(BF16) for j in nu]
            continue
        upd = [[r0 for r0 in range(0, c, b) if (r0 // b) % 2 == 1 - units[j]["d"]] for j in nu]
        t_sel = [jnp.concatenate([t16[j][r0:r0 + b] for r0 in upd[j]], axis=0) for j in nu]
        p16 = [_mm(t_sel[j], a16[j] * msk_ref[lev]).astype(BF16) for j in nu]
        t_new = [t_sel[j] - _mm(p16[j], t16[j]).astype(BF16) for j in nu]
        t16 = [jnp.concatenate([t_new[j][upd[j].index(r0) * b:(upd[j].index(r0) + 1) * b] if r0 in upd[j]
                                else t16[j][r0:r0 + b] for r0 in range(0, c, b)], axis=0) for j in nu]
    rhs = [jnp.concatenate([units[j]["v"] * units[j]["beta"], kbeta[j] * egc[j]], axis=1) for j in nu]
    sol = [rhs[j] + _mm(t16[j] - eye16, rhs[j].astype(BF16)) for j in nu]
    qd16 = [(units[j]["q"] * egc[j]).astype(BF16) for j in nu]
    kst16 = [(units[j]["k"] * jnp.exp(units[j]["gl"] - units[j]["gc"])).astype(BF16) for j in nu]
    state = {(d, h): s_ref[0, d, h] for d in range(2) for h in range(DN_HEADS)}
    for step in range(nblk):
        cur = [j for j in nu if units[j]["g"] == (step if units[j]["d"] == 0 else nblk - 1 - step)]
        s16 = {j: state[units[j]["d"], units[j]["h"]].astype(BF16) for j in cur}
        ws = {j: _mm(jnp.concatenate([sol[j][:, DN_HEAD_DIM:].astype(BF16), qd16[j]], axis=0), s16[j])
              for j in cur}
        vn16 = {j: (sol[j][:, :DN_HEAD_DIM] - ws[j][:c]).astype(BF16) for j in cur}
        for j in cur:
            u = units[j]
            u["o_ref"][0, u["rs"], u["hs"]] = (ws[j][c:] + _mm(aqk16[j], vn16[j])).astype(BF16)
        for j in cur:
            u = units[j]
            state[u["d"], u["h"]] = (state[u["d"], u["h"]] * jnp.exp(u["gl"])
                                     + lax.dot_general(kst16[j], vn16[j], (((0,), (0,)), ((), ())),
                                                       preferred_element_type=F32))
    for (d, h), val in state.items():
        s_ref[0, d, h] = val


def _scan_call(qkv, gb, s0):
    b, s, _ = qkv.shape
    c = min(SCAN_CHUNK, s)
    blk = min(SCAN_BLOCK, s)
    nb = s // blk
    nlev = int(math.log2(c))
    fwd = lambda n: pl.BlockSpec((1, blk, n), lambda bi, i: (bi, i, 0))
    bwd = lambda n: pl.BlockSpec((1, blk, n), lambda bi, i: (bi, nb - 1 - i, 0))
    st = pl.BlockSpec((1, 2, DN_HEADS, DN_HEAD_DIM, DN_HEAD_DIM), lambda bi, i: (bi, 0, 0, 0, 0))
    return pl.pallas_call(
        functools.partial(_scan_kernel, c=c),
        grid=(b, nb),
        in_specs=[fwd(3 * DN_WIDTH), fwd(LANES), bwd(3 * DN_WIDTH), bwd(LANES), st],
        out_specs=[fwd(DN_WIDTH), bwd(DN_WIDTH), st],
        out_shape=[jax.ShapeDtypeStruct((b, s, DN_WIDTH), BF16)] * 2
        + [jax.ShapeDtypeStruct(s0.shape, F32)],
        scratch_shapes=[pltpu.VMEM((nlev + 1, c, c), BF16), pltpu.VMEM((2, c, c), F32)],
        compiler_params=_params(("parallel", "arbitrary")),
        name="scan",
    )(qkv, gb, qkv, gb, s0)


def _window_offsets(w):
    return -(w // 2), w - 1 - (w // 2)


def _pool_inv_counts(rows, width):
    t = jnp.arange(rows * width, dtype=jnp.int32)
    r, c = t // width, t % width
    cols = []
    for w in POOL_WINDOWS:
        lo, hi = _window_offsets(w)
        cnt_r = jnp.minimum(r + hi + 1, rows) - jnp.maximum(r + lo, 0)
        cnt_c = jnp.minimum(c + hi + 1, width) - jnp.maximum(c + lo, 0)
        inv = 1.0 / (cnt_r * cnt_c).astype(F32)
        cols.append(jnp.broadcast_to(inv[:, None], (rows * width, POOL_GROUP_DIM)))
    return jnp.concatenate(cols, axis=1)


def _window_sums(x, unit, lev_small):
    n = x.shape[0]

    def shifted(a, k):
        return pltpu.roll(a, (k * unit) % n, axis=0)

    sums = {1: x + shifted(x, 1)}
    for lev in range(2, lev_small + 2):
        step = 2 ** (lev - 2)
        sums[lev] = shifted(sums[lev - 1], step) + shifted(sums[lev - 1], -step)
    return sums[lev_small], sums[lev_small + 1]


def _pool_tile(ext, inv_cnt, width):
    halo = SUBLANES * width
    tm = ext.shape[0] - 2 * halo
    pad = SUBLANES
    outs = []
    for slab in range(POOL_WIDTH // LANES):
        lanes = slice(slab * LANES, (slab + 1) * LANES)
        e = ext[:, lanes]
        small_lane = lax.broadcasted_iota(jnp.int32, (tm, LANES), 1) < POOL_GROUP_DIM
        levels = int(math.log2(POOL_WINDOWS[2 * slab]))
        rs, rl = _window_sums(e, width, levels)
        m1 = jnp.where(small_lane, rs[halo:halo + tm], rl[halo:halo + tm])
        if width > 1:
            zeros = jnp.zeros((pad, LANES), F32)
            pieces = []
            for r in range(tm // width):
                pieces += [zeros, m1[r * width:(r + 1) * width], zeros]
            cs, cl = _window_sums(jnp.concatenate(pieces, axis=0), 1, levels)
            stride = width + 2 * pad
            pick = lambda a: jnp.concatenate(
                [a[r * stride + pad:r * stride + pad + width] for r in range(tm // width)], axis=0)
            m1 = jnp.where(small_lane, pick(cs), pick(cl))
        outs.append(m1 * inv_cnt[:, lanes] - e[halo:halo + tm])
    return jnp.concatenate(outs, axis=1)


def _mix_kernel(x_ref, xp_ref, xn_ref, sh_ref, sc_ref, gt_ref, g_ref, of_ref, ob_ref,
                pin_ref, pinp_ref, pinn_ref, icnt_ref,
                wz_ref, wsc_ref, wgate_ref, dng_ref, wpool_ref, pscale_ref, scw_ref,
                wa_ref, wb_ref, wc_ref, wo_ref, out_ref, *, width):
    i = pl.program_id(1)
    last = pl.num_programs(1) - 1
    tm, d = x_ref.shape[1], x_ref.shape[2]
    xm = x_ref[0]
    h_ext = _halo_modulate(x_ref, xp_ref, xn_ref, g_ref[...], sh_ref[0], sc_ref[0])
    hb = h_ext[SUBLANES:SUBLANES + tm].astype(BF16)
    z = _dot(hb, wz_ref[...])
    o = of_ref[0].astype(F32) + ob_ref[0].astype(F32)
    parts = []
    for h in range(DN_HEADS):
        lo = h * DN_HEAD_DIM
        parts.append(_rms(o[:, lo:lo + DN_HEAD_DIM]) * dng_ref[...] * _silu(z[:, lo:lo + DN_HEAD_DIM]))
    y_a = _dot(jnp.concatenate(parts, axis=1), wa_ref[...])
    keep_prev, keep_next = jnp.where(i > 0, 1.0, 0.0), jnp.where(i < last, 1.0, 0.0)
    pool_ext = jnp.concatenate([pinp_ref[0] * keep_prev, pin_ref[0], pinn_ref[0] * keep_next], axis=0)
    pd = _pool_tile(pool_ext, icnt_ref[...], width)
    y_b = _dot(_dot(pd, wpool_ref[...]) * pscale_ref[...], wb_ref[...])
    psc = _dot(h_ext, wsc_ref[...])
    conv = _conv3_ext(psc[:, 2 * SC_WIDTH:] * psc[:, :SC_WIDTH], scw_ref, keep_prev, keep_next)
    y_c = _dot(psc[SUBLANES:SUBLANES + tm, SC_WIDTH:2 * SC_WIDTH] * conv, wc_ref[...])
    gates = jax.nn.sigmoid(_dot(hb, wgate_ref[...]))
    y = gates[:, :d] * y_a + gates[:, d:2 * d] * y_b + gates[:, 2 * d:] * y_c
    out_ref[0] = xm + gt_ref[0] * _dot(y, wo_ref[...])


def _mix_call(x, shift, scale, gate, norm_g, o_f, o_b, pin, inv_cnt, width, consts):
    b, s, d = x.shape
    tm = min(TOKEN_TILE, s)
    nt = s // tm
    r8 = tm // SUBLANES
    nb8 = s // SUBLANES
    halo = SUBLANES * width
    assert tm % halo == 0 and s % tm == 0
    hb, nh = tm // halo, s // halo
    tok = lambda n: pl.BlockSpec((1, tm, n), lambda bi, i: (bi, i, 0))
    vec = pl.BlockSpec((1, 1, d), lambda bi, i: (bi, 0, 0))
    return pl.pallas_call(
        functools.partial(_mix_kernel, width=width),
        grid=(b, nt),
        in_specs=[tok(d),
                  pl.BlockSpec((1, SUBLANES, d), lambda bi, i: (bi, jnp.maximum(i * r8 - 1, 0), 0)),
                  pl.BlockSpec((1, SUBLANES, d), lambda bi, i: (bi, jnp.minimum((i + 1) * r8, nb8 - 1), 0)),
                  vec, vec, vec, _const_spec((1, d)),
                  tok(DN_WIDTH), tok(DN_WIDTH), tok(POOL_WIDTH),
                  pl.BlockSpec((1, halo, POOL_WIDTH), lambda bi, i: (bi, jnp.maximum(i * hb - 1, 0), 0)),
                  pl.BlockSpec((1, halo, POOL_WIDTH), lambda bi, i: (bi, jnp.minimum((i + 1) * hb, nh - 1), 0)),
                  pl.BlockSpec((tm, POOL_WIDTH), lambda bi, i: (i, 0))]
        + [_const_spec(w.shape) for w in consts],
        out_specs=tok(d),
        out_shape=jax.ShapeDtypeStruct((b, s, d), F32),
        compiler_params=_params(("parallel", "parallel")),
        name="mix",
    )(x, x, x, shift, scale, gate, norm_g, o_f, o_b, pin, pin, pin, inv_cnt, *consts)


def _ffn_kernel(x_ref, sh_ref, sc_ref, gt_ref, g_ref, wgu_ref, wdown_ref, gf_ref, out_ref, *, final):
    dff = wdown_ref.shape[0]
    xm = x_ref[0]
    hb = _modulate(xm, g_ref[...], sh_ref[0], sc_ref[0]).astype(BF16)
    gu = _dot(hb, wgu_ref[...])
    act = _silu(gu[:, :dff]) * gu[:, dff:]
    r = xm + gt_ref[0] * _dot(act, wdown_ref[...])
    if final:
        r = _rms(r) * gf_ref[...]
    out_ref[0] = r


def _ffn_call(x, shift, scale, gate, norm_g, w_gu, w_down, final_g, final):
    b, s, d = x.shape
    tm = min(TOKEN_TILE, s)
    tok = pl.BlockSpec((1, tm, d), lambda bi, i: (bi, i, 0))
    vec = pl.BlockSpec((1, 1, d), lambda bi, i: (bi, 0, 0))
    return pl.pallas_call(
        functools.partial(_ffn_kernel, final=final),
        grid=(b, s // tm),
        in_specs=[tok, vec, vec, vec, _const_spec((1, d)), _const_spec(w_gu.shape),
                  _const_spec(w_down.shape), _const_spec((1, d))],
        out_specs=tok,
        out_shape=jax.ShapeDtypeStruct((b, s, d), F32),
        compiler_params=_params(("parallel", "parallel")),
        name="ffn",
    )(x, shift, scale, gate, norm_g, w_gu, w_down, final_g)


def _block_diag(w):
    g, ci, co = w.shape
    out = jnp.zeros((g * ci, g * co), w.dtype)
    for j in range(g):
        out = out.at[j * ci:(j + 1) * ci, j * co:(j + 1) * co].set(w[j])
    return out


def kernel(x, c, ctx, c_ctx, w_ada, b_ada, norm1_g, norm2_g, w_in, dn_conv_w, dn_a_log, dn_dt_bias,
           dn_norm_g, pool_w, pool_scale, sc_conv_w, w_br_a, w_br_b, w_br_c, w_o, w_gu, w_down,
           final_norm_g):
    bn, seq, d = x.shape
    depth = w_ada.shape[0]
    rows = seq // GRID_W
    off_z = 3 * DN_WIDTH
    off_a = off_z + DN_WIDTH
    off_pool = off_a + 4 * DN_HEADS
    off_sc = off_pool + POOL_WIDTH
    off_gate = off_sc + 3 * SC_WIDTH

    n_c = -(-(bn + 1) // SUBLANES) * SUBLANES
    cs = jnp.concatenate([c, c_ctx[None], jnp.zeros((n_c - bn - 1, d), F32)], axis=0)
    mod = _ada_call(cs, w_ada, b_ada)

    s0 = jnp.zeros((bn, 2, DN_HEADS, DN_HEAD_DIM, DN_HEAD_DIM), F32)
    icnt_lat = _pool_inv_counts(rows, GRID_W)
    icnt_ctx = _pool_inv_counts(ctx.shape[1], 1)
    final_g = final_norm_g.reshape(1, d)
    for l in range(depth):
        wl = w_in[l]
        w_qkv = wl[:, :off_z].astype(BF16)
        w_z = wl[:, off_z:off_a].astype(BF16)
        w_ab = jnp.pad(wl[:, off_a:off_pool], ((0, 0), (0, LANES - 4 * DN_HEADS))).astype(BF16)
        w_pool = wl[:, off_pool:off_sc].astype(BF16)
        w_sc = wl[:, off_sc:off_gate].astype(BF16)
        w_gate = wl[:, off_gate:].astype(BF16)
        alog = jnp.pad(dn_a_log[l].reshape(1, -1), ((0, 0), (0, LANES - 2 * DN_HEADS)))
        dtb = jnp.pad(dn_dt_bias[l].reshape(1, -1), ((0, 0), (0, LANES - 2 * DN_HEADS)))
        n1 = norm1_g[l].reshape(1, d)
        n2 = norm2_g[l].reshape(1, d)
        mix_consts = (w_z, w_sc, w_gate, dn_norm_g[l].reshape(1, -1),
                      _block_diag(pool_w[l]).astype(BF16), pool_scale[l].reshape(1, -1), sc_conv_w[l],
                      w_br_a[l].astype(BF16), w_br_b[l].astype(BF16), w_br_c[l].astype(BF16),
                      w_o[l].astype(BF16))
        wgu = w_gu[l].astype(BF16)
        wdn = w_down[l].astype(BF16)
        lat = [mod[l, :bn, j * d:(j + 1) * d][:, None, :] for j in range(6)]
        cxm = [jnp.broadcast_to(mod[l, bn:bn + 1, j * d:(j + 1) * d][None], (bn, 1, d)) for j in range(6)]

        qkv, gb, pin = _proj_call(ctx, cxm[0], cxm[1], n1, w_qkv, w_ab, w_pool, dn_conv_w[l], alog, dtb)
        o_f, o_b, s_ctx = _scan_call(qkv, gb, s0)
        if l < depth - 1:
            ctx = _mix_call(ctx, cxm[0], cxm[1], cxm[2], n1, o_f, o_b, pin, icnt_ctx, 1, mix_consts)
            ctx = _ffn_call(ctx, cxm[3], cxm[4], cxm[5], n2, wgu, wdn, final_g, False)

        qkv, gb, pin = _proj_call(x, lat[0], lat[1], n1, w_qkv, w_ab, w_pool, dn_conv_w[l], alog, dtb)
        o_f, o_b, _ = _scan_call(qkv, gb, s_ctx)
        x = _mix_call(x, lat[0], lat[1], lat[2], n1, o_f, o_b, pin, icnt_lat, GRID_W, mix_consts)
        x = _ffn_call(x, lat[3], lat[4], lat[5], n2, wgu, wdn, final_g, l == depth - 1)
    return x
```

```python
import functools
import math

import jax
import jax.numpy as jnp
from jax import lax
from jax.experimental import pallas as pl
from jax.experimental.pallas import tpu as pltpu

F32 = jnp.float32
BF16 = jnp.bfloat16

EPS = 1e-6
GRID_W = 64
DN_HEADS = 4
DN_HEAD_DIM = 128
DN_WIDTH = DN_HEADS * DN_HEAD_DIM
POOL_WINDOWS = (2, 4, 8, 16)
POOL_GROUP_DIM = 64
POOL_WIDTH = POOL_GROUP_DIM * len(POOL_WINDOWS)
SC_WIDTH = 256
N_BRANCH = 3

LANES = 128
SUBLANES = 8
BF16_ROWS = 16
SCAN_CHUNK = 128
SCAN_BLOCK = 512
TOKEN_TILE = 512
VMEM_LIMIT = 56 * 1024 * 1024
NEG_BIG = -1e30


def _dot(a, b):
    return jnp.dot(a.astype(BF16), b.astype(BF16), preferred_element_type=F32)


def _dot_nt(a, b):
    return lax.dot_general(a.astype(BF16), b.astype(BF16), (((1,), (1,)), ((), ())),
                           preferred_element_type=F32)


def _dot_tn(a, b):
    return lax.dot_general(a.astype(BF16), b.astype(BF16), (((0,), (0,)), ((), ())),
                           preferred_element_type=F32)


def _sigmoid(x):
    return 0.5 + 0.5 * jnp.tanh(0.5 * x)


def _silu(x):
    hx = 0.5 * x
    return hx + hx * jnp.tanh(hx)


def _rms(x):
    return x * lax.rsqrt(jnp.mean(x * x, axis=-1, keepdims=True) + EPS)


def _modulate(x, g, shift, scale):
    return (_rms(x) * g) * (1.0 + scale) + shift


def _halo_modulate(x_ref, xp_ref, xn_ref, g, shift, scale):
    x_ext = jnp.concatenate([xp_ref[0], x_ref[0], xn_ref[0]], axis=0)
    return _modulate(x_ext, g, shift, scale)


def _conv3_ext(p_ext, w_ref, keep_prev, keep_next):
    n = p_ext.shape[0] - 2 * SUBLANES
    pe = jnp.concatenate([p_ext[:SUBLANES] * keep_prev, p_ext[SUBLANES:SUBLANES + n],
                          p_ext[SUBLANES + n:] * keep_next], axis=0)
    return (pe[SUBLANES - 1:SUBLANES - 1 + n] * w_ref[0:1] + pe[SUBLANES:SUBLANES + n] * w_ref[1:2]
            + pe[SUBLANES + 1:SUBLANES + 1 + n] * w_ref[2:3])


def _const_spec(shape):
    nd = len(shape)
    return pl.BlockSpec(shape, lambda *_: (0,) * nd, pipeline_mode=pl.Buffered(1))


def _params(sem):
    return pltpu.CompilerParams(dimension_semantics=sem, vmem_limit_bytes=VMEM_LIMIT)


def _ada_kernel(c_ref, w_ref, b_ref, o_ref):
    a = _silu(c_ref[...])
    w = w_ref[0]
    a_hi, w_hi = a.astype(BF16), w.astype(BF16)
    a_lo = (a - a_hi.astype(F32)).astype(BF16)
    w_lo = (w - w_hi.astype(F32)).astype(BF16)
    mm = functools.partial(jnp.dot, preferred_element_type=F32)
    o_ref[0] = (mm(a_hi, w_hi) + (mm(a_lo, w_hi) + mm(a_hi, w_lo))) + b_ref[0]


def _ada_call(cs, w_ada, b_ada):
    nl, d, n6 = w_ada.shape
    rows = cs.shape[0]
    tn = n6 // 4
    return pl.pallas_call(
        _ada_kernel,
        grid=(nl, n6 // tn),
        in_specs=[pl.BlockSpec((rows, d), lambda l, j: (0, 0)),
                  pl.BlockSpec((1, d, tn), lambda l, j: (l, 0, j)),
                  pl.BlockSpec((1, 1, tn), lambda l, j: (l, 0, j))],
        out_specs=pl.BlockSpec((1, rows, tn), lambda l, j: (l, 0, j)),
        out_shape=jax.ShapeDtypeStruct((nl, rows, n6), F32),
        compiler_params=_params(("parallel", "parallel")),
        name="ada",
    )(cs, w_ada, b_ada.reshape(nl, 1, n6))


def _proj_kernel(x_ref, xp_ref, xn_ref, sh_ref, sc_ref, g_ref, wqkv_ref, wab_ref, wpool_ref,
                 cw_ref, alog_ref, dtb_ref, qkv_ref, gb_ref, pin_ref, pbuf_ref, abuf_ref):
    i = pl.program_id(1)
    last = pl.num_programs(1) - 1
    tm = x_ref.shape[1]
    m = tm // SUBLANES
    h_ext = _halo_modulate(x_ref, xp_ref, xn_ref, g_ref[...], sh_ref[0], sc_ref[0])
    hb = h_ext[SUBLANES:SUBLANES + tm].astype(BF16)
    hb_ext = h_ext.astype(BF16)
    ncol = 3 * DN_WIDTH // LANES
    keep_prev, keep_next = jnp.where(i > 0, 1.0, 0.0), jnp.where(i < last, 1.0, 0.0)
    for j in range(ncol):
        cols = slice(j * LANES, (j + 1) * LANES)
        if j % 2 == 0:
            p2 = _mm(hb_ext, wqkv_ref[:, j * LANES:(j + 2) * LANES])
            pbuf_ref[j] = p2[:, :LANES]
            pbuf_ref[j + 1] = p2[:, LANES:]
        w0, w1, w2 = cw_ref[0:1, cols], cw_ref[1:2, cols], cw_ref[2:3, cols]
        blk = [pbuf_ref[j, pl.ds(SUBLANES + r, m, stride=SUBLANES), :] for r in range(SUBLANES)]
        before = pbuf_ref[j, SUBLANES - 1:SUBLANES, :] * keep_prev
        after = pbuf_ref[j, SUBLANES + tm:SUBLANES + tm + 1, :] * keep_next
        prev0 = jnp.concatenate([before, blk[SUBLANES - 1][:m - 1]], axis=0)
        next7 = jnp.concatenate([blk[0][1:], after], axis=0)
        for r in range(SUBLANES):
            prev = blk[r - 1] if r else prev0
            nxt = blk[r + 1] if r < SUBLANES - 1 else next7
            a = _silu(prev * w0 + blk[r] * w1 + nxt * w2)
            if j < 2 * DN_HEADS:
                a = a * lax.rsqrt(jnp.sum(a * a, axis=-1, keepdims=True) + EPS)
            if j < DN_HEADS:
                a = a * (DN_HEAD_DIM ** -0.5)
            abuf_ref[j, pl.ds(r, m, stride=SUBLANES), :] = a
    for j in range(ncol):
        qkv_ref[0, :, j * LANES:(j + 1) * LANES] = abuf_ref[j].astype(BF16)
    pab = _dot(hb, wab_ref[...])
    z = pab + dtb_ref[...]
    softplus = jnp.maximum(z, 0.0) + jnp.log(1.0 + jnp.exp(-jnp.abs(z)))
    gdec = -jnp.exp(alog_ref[...]) * softplus
    lane = lax.broadcasted_iota(jnp.int32, pab.shape, 1)
    gb_ref[0] = jnp.where(lane < 2 * DN_HEADS, gdec,
                          jnp.where(lane < 4 * DN_HEADS, _sigmoid(pab), 0.0))
    pin_ref[0] = _dot(hb, wpool_ref[...])


def _proj_call(x, shift, scale, norm_g, w_qkv, w_ab, w_pool, conv_w, alog, dtb):
    b, s, d = x.shape
    tm = min(TOKEN_TILE, s)
    nt = s // tm
    r8 = tm // SUBLANES
    nb8 = s // SUBLANES
    tok = lambda n: pl.BlockSpec((1, tm, n), lambda bi, i: (bi, i, 0))
    vec = pl.BlockSpec((1, 1, d), lambda bi, i: (bi, 0, 0))
    return pl.pallas_call(
        _proj_kernel,
        grid=(b, nt),
        in_specs=[tok(d),
                  pl.BlockSpec((1, SUBLANES, d), lambda bi, i: (bi, jnp.maximum(i * r8 - 1, 0), 0)),
                  pl.BlockSpec((1, SUBLANES, d), lambda bi, i: (bi, jnp.minimum((i + 1) * r8, nb8 - 1), 0)),
                  vec, vec, _const_spec((1, d)),
                  _const_spec(w_qkv.shape), _const_spec(w_ab.shape), _const_spec(w_pool.shape),
                  _const_spec(conv_w.shape), _const_spec(alog.shape), _const_spec(dtb.shape)],
        out_specs=[tok(3 * DN_WIDTH), tok(LANES), tok(POOL_WIDTH)],
        out_shape=[jax.ShapeDtypeStruct((b, s, 3 * DN_WIDTH), BF16),
                   jax.ShapeDtypeStruct((b, s, LANES), F32), jax.ShapeDtypeStruct((b, s, POOL_WIDTH), F32)],
        scratch_shapes=[pltpu.VMEM((3 * DN_WIDTH // LANES, tm + 2 * SUBLANES, LANES), F32),
                        pltpu.VMEM((3 * DN_WIDTH // LANES, tm, LANES), F32)],
        compiler_params=_params(("parallel", "parallel")),
        name="proj_in",
    )(x, x, x, shift, scale, norm_g, w_qkv, w_ab, w_pool, conv_w, alog, dtb)


def _prefix_sum_rows(x):
    n = x.shape[0]
    r = lax.broadcasted_iota(jnp.int32, x.shape, 0)
    s = 1
    while s < n:
        x = x + jnp.where(r >= s, pltpu.roll(x, s, axis=0), 0.0)
        s *= 2
    return x


def _mm(a, b):
    return jnp.dot(a, b, preferred_element_type=F32)


def _scan_kernel(qkvf_ref, gf_ref, qkvb_ref, gbk_ref, s0_ref, of_ref, ob_ref, s_ref, msk_ref, tri_ref, *, c):
    nlev = msk_ref.shape[0] - 1
    i = pl.program_id(1)

    @pl.when(i == 0)
    def _():
        s_ref[...] = s0_ref[...]
        row = lax.broadcasted_iota(jnp.int32, (c, c), 0)
        col = lax.broadcasted_iota(jnp.int32, (c, c), 1)

        def same(sh):
            return jnp.right_shift(row, sh) == jnp.right_shift(col, sh)

        one = jnp.ones((c, c), F32)
        zero = jnp.zeros((c, c), F32)
        msk_ref[0] = jnp.where(same(1), jnp.where(row == col, zero, one), zero).astype(BF16)
        for j in range(1, nlev):
            msk_ref[j] = jnp.where(same(j + 1), jnp.where(same(j), zero, one), zero).astype(BF16)
        msk_ref[nlev] = jnp.where(row == col, one, zero).astype(BF16)
        tri_ref[0] = jnp.where(row >= col, 0.0, NEG_BIG)
        tri_ref[1] = jnp.where(row <= col, 0.0, NEG_BIG)

    nblk = qkvf_ref.shape[1] // c
    units = []
    for d, (qkv_ref, gb_ref, o_ref) in enumerate(((qkvf_ref, gf_ref, of_ref), (qkvb_ref, gbk_ref, ob_ref))):
        for g in range(nblk):
            rs = slice(g * c, (g + 1) * c)
            gb = gb_ref[0, rs, :]
            cs = _prefix_sum_rows(gb)
            tot = cs[c - 1:c, :]
            if d == 1:
                cs = tot - cs + gb
            cs_t = cs.T
            for h in range(DN_HEADS):
                gi = d * DN_HEADS + h
                bi = 2 * DN_HEADS + gi
                lo = h * DN_HEAD_DIM
                hs = slice(lo, lo + DN_HEAD_DIM)
                q16 = qkv_ref[0, rs, hs]
                k16 = qkv_ref[0, rs, DN_WIDTH + lo:DN_WIDTH + lo + DN_HEAD_DIM]
                v16 = qkv_ref[0, rs, 2 * DN_WIDTH + lo:2 * DN_WIDTH + lo + DN_HEAD_DIM]
                units.append(dict(d=d, h=h, g=g, rs=rs, hs=hs, o_ref=o_ref, gc=cs[:, gi:gi + 1],
                                  gc_row=cs_t[gi:gi + 1, :], gl=tot[:, gi:gi + 1], beta=gb[:, bi:bi + 1],
                                  q16=q16, k16=k16, q=q16.astype(F32), k=k16.astype(F32),
                                  v=v16.astype(F32)))
    nu = range(len(units))
    dec = [jnp.exp((u["gc"] - u["gc_row"]) + tri_ref[u["d"]]) for u in units]
    egc = [jnp.exp(u["gc"]) for u in units]
    k16 = [u["k16"] for u in units]
    kbeta = [u["k"] * u["beta"] for u in units]
    kq = [_dot_nt(jnp.concatenate([kbeta[j].astype(BF16), units[j]["q16"]], axis=0), k16[j])
          for j in nu]
    a16 = [(kq[j][:c] * dec[j]).astype(BF16) for j in nu]
    aqk16 = [(kq[j][c:] * dec[j]).astype(BF16) for j in nu]
    eye16 = msk_ref[nlev]
    t16 = [eye16 - a16[j] * msk_ref[0] for j in nu]
    for lev in range(1, nlev):
        b = 2 ** lev
        if b < BF16_ROWS:
            p16 = [_mm(t16[j], a16[j] * msk_ref[lev]).astype(BF16) for j in nu]
            t16 = [t16[j] - _mm(p16[j], t16[j]).astype(BF16) for j in nu]
            continue
        upd = [[r0 for r0 in range(0, c, b) if (r0 // b) % 2 == 1 - units[j]["d"]] for j in nu]
        t_sel = [jnp.concatenate([t16[j][r0:r0 + b] for r0 in upd[j]], axis=0) for j in nu]
        p16 = [_mm(t_sel[j], a16[j] * msk_ref[lev]).astype(BF16) for j in nu]
        t_new = [t_sel[j] - _mm(p16[j], t16[j]).astype(BF16) for j in nu]
        t16 = [jnp.concatenate([t_new[j][upd[j].index(r0) * b:(upd[j].index(r0) + 1) * b] if r0 in upd[j]
                                else t16[j][r0:r0 + b] for r0 in range(0, c, b)], axis=0) for j in nu]
    rhs = [jnp.concatenate([units[j]["v"] * units[j]["beta"], kbeta[j] * egc[j]], axis=1) for j in nu]
    sol = [rhs[j] + _mm(t16[j] - eye16, rhs[j].astype(BF16)) for j in nu]
    qd16 = [(units[j]["q"] * egc[j]).astype(BF16) for j in nu]
    kst16 = [(units[j]["k"] * jnp.exp(units[j]["gl"] - units[j]["gc"])).astype(BF16) for j in nu]
    state = {(d, h): s_ref[0, d, h] for d in range(2) for h in range(DN_HEADS)}
    for step in range(nblk):
        cur = [j for j in nu if units[j]["g"] == (step if units[j]["d"] == 0 else nblk - 1 - step)]
        s16 = {j: state[units[j]["d"], units[j]["h"]].astype(BF16) for j in cur}
        ws = {j: _mm(jnp.concatenate([sol[j][:, DN_HEAD_DIM:].astype(BF16), qd16[j]], axis=0), s16[j])
              for j in cur}
        vn16 = {j: (sol[j][:, :DN_HEAD_DIM] - ws[j][:c]).astype(BF16) for j in cur}
        for j in cur:
            u = units[j]
            u["o_ref"][0, u["rs"], u["hs"]] = (ws[j][c:] + _mm(aqk16[j], vn16[j])).astype(BF16)
        for j in cur:
            u = units[j]
            state[u["d"], u["h"]] = (state[u["d"], u["h"]] * jnp.exp(u["gl"])
                                     + lax.dot_general(kst16[j], vn16[j], (((0,), (0,)), ((), ())),
                                                       preferred_element_type=F32))
    for (d, h), val in state.items():
        s_ref[0, d, h] = val


def _scan_call(qkv, gb, s0):
    b, s, _ = qkv.shape
    c = min(SCAN_CHUNK, s)
    blk = min(SCAN_BLOCK, s)
    nb = s // blk
    nlev = int(math.log2(c))
    fwd = lambda n: pl.BlockSpec((1, blk, n), lambda bi, i: (bi, i, 0))
    bwd = lambda n: pl.BlockSpec((1, blk, n), lambda bi, i: (bi, nb - 1 - i, 0))
    st = pl.BlockSpec((1, 2, DN_HEADS, DN_HEAD_DIM, DN_HEAD_DIM), lambda bi, i: (bi, 0, 0, 0, 0))
    return pl.pallas_call(
        functools.partial(_scan_kernel, c=c),
        grid=(b, nb),
        in_specs=[fwd(3 * DN_WIDTH), fwd(LANES), bwd(3 * DN_WIDTH), bwd(LANES), st],
        out_specs=[fwd(DN_WIDTH), bwd(DN_WIDTH), st],
        out_shape=[jax.ShapeDtypeStruct((b, s, DN_WIDTH), BF16)] * 2
        + [jax.ShapeDtypeStruct(s0.shape, F32)],
        scratch_shapes=[pltpu.VMEM((nlev + 1, c, c), BF16), pltpu.VMEM((2, c, c), F32)],
        compiler_params=_params(("parallel", "arbitrary")),
        name="scan",
    )(qkv, gb, qkv, gb, s0)


def _window_offsets(w):
    return -(w // 2), w - 1 - (w // 2)


def _pool_inv_counts(rows, width):
    t = jnp.arange(rows * width, dtype=jnp.int32)
    r, c = t // width, t % width
    cols = []
    for w in POOL_WINDOWS:
        lo, hi = _window_offsets(w)
        cnt_r = jnp.minimum(r + hi + 1, rows) - jnp.maximum(r + lo, 0)
        cnt_c = jnp.minimum(c + hi + 1, width) - jnp.maximum(c + lo, 0)
        inv = 1.0 / (cnt_r * cnt_c).astype(F32)
        cols.append(jnp.broadcast_to(inv[:, None], (rows * width, POOL_GROUP_DIM)))
    return jnp.concatenate(cols, axis=1)


def _window_sums(x, unit, lev_small):
    n = x.shape[0]

    def shifted(a, k):
        return pltpu.roll(a, (k * unit) % n, axis=0)

    sums = {1: x + shifted(x, 1)}
    for lev in range(2, lev_small + 2):
        step = 2 ** (lev - 2)
        sums[lev] = shifted(sums[lev - 1], step) + shifted(sums[lev - 1], -step)
    return sums[lev_small], sums[lev_small + 1]


def _pool_tile(ext, inv_cnt, width):
    halo = SUBLANES * width
    tm = ext.shape[0] - 2 * halo
    pad = SUBLANES
    outs = []
    for slab in range(POOL_WIDTH // LANES):
        lanes = slice(slab * LANES, (slab + 1) * LANES)
        e = ext[:, lanes]
        small_lane = lax.broadcasted_iota(jnp.int32, (tm, LANES), 1) < POOL_GROUP_DIM
        levels = int(math.log2(POOL_WINDOWS[2 * slab]))
        rs, rl = _window_sums(e, width, levels)
        m1 = jnp.where(small_lane, rs[halo:halo + tm], rl[halo:halo + tm])
        if width > 1:
            zeros = jnp.zeros((pad, LANES), F32)
            pieces = []
            for r in range(tm // width):
                pieces += [zeros, m1[r * width:(r + 1) * width], zeros]
            cs, cl = _window_sums(jnp.concatenate(pieces, axis=0), 1, levels)
            stride = width + 2 * pad
            pick = lambda a: jnp.concatenate(
                [a[r * stride + pad:r * stride + pad + width] for r in range(tm // width)], axis=0)
            m1 = jnp.where(small_lane, pick(cs), pick(cl))
        outs.append(m1 * inv_cnt[:, lanes] - e[halo:halo + tm])
    return jnp.concatenate(outs, axis=1)


def _mix_kernel(x_ref, xp_ref, xn_ref, sh_ref, sc_ref, gt_ref, g_ref, of_ref, ob_ref,
                pin_ref, pinp_ref, pinn_ref, icnt_ref,
                wz_ref, wsc_ref, wgate_ref, dng_ref, wpool_ref, pscale_ref, scw_ref,
                wa_ref, wb_ref, wc_ref, wo_ref, out_ref, *, width):
    i = pl.program_id(1)
    last = pl.num_programs(1) - 1
    tm, d = x_ref.shape[1], x_ref.shape[2]
    xm = x_ref[0]
    h_ext = _halo_modulate(x_ref, xp_ref, xn_ref, g_ref[...], sh_ref[0], sc_ref[0])
    hb = h_ext[SUBLANES:SUBLANES + tm].astype(BF16)
    z = _dot(hb, wz_ref[...])
    o = of_ref[0].astype(F32) + ob_ref[0].astype(F32)
    parts = []
    for h in range(DN_HEADS):
        lo = h * DN_HEAD_DIM
        parts.append(_rms(o[:, lo:lo + DN_HEAD_DIM]) * dng_ref[...] * _silu(z[:, lo:lo + DN_HEAD_DIM]))
    y_a = _dot(jnp.concatenate(parts, axis=1), wa_ref[...])
    keep_prev, keep_next = jnp.where(i > 0, 1.0, 0.0), jnp.where(i < last, 1.0, 0.0)
    pool_ext = jnp.concatenate([pinp_ref[0] * keep_prev, pin_ref[0], pinn_ref[0] * keep_next], axis=0)
    pd = _pool_tile(pool_ext, icnt_ref[...], width)
    y_b = _dot(_dot(pd, wpool_ref[...]) * pscale_ref[...], wb_ref[...])
    psc = _dot(h_ext, wsc_ref[...])
    conv = _conv3_ext(psc[:, 2 * SC_WIDTH:] * psc[:, :SC_WIDTH], scw_ref, keep_prev, keep_next)
    y_c = _dot(psc[SUBLANES:SUBLANES + tm, SC_WIDTH:2 * SC_WIDTH] * conv, wc_ref[...])
    gates = _sigmoid(_dot(hb, wgate_ref[...]))
    y = gates[:, :d] * y_a + gates[:, d:2 * d] * y_b + gates[:, 2 * d:] * y_c
    out_ref[0] = xm + gt_ref[0] * _dot(y, wo_ref[...])


def _mix_call(x, shift, scale, gate, norm_g, o_f, o_b, pin, inv_cnt, width, consts):
    b, s, d = x.shape
    tm = min(TOKEN_TILE, s)
    nt = s // tm
    r8 = tm // SUBLANES
    nb8 = s // SUBLANES
    halo = SUBLANES * width
    assert tm % halo == 0 and s % tm == 0
    hb, nh = tm // halo, s // halo
    tok = lambda n: pl.BlockSpec((1, tm, n), lambda bi, i: (bi, i, 0))
    vec = pl.BlockSpec((1, 1, d), lambda bi, i: (bi, 0, 0))
    return pl.pallas_call(
        functools.partial(_mix_kernel, width=width),
        grid=(b, nt),
        in_specs=[tok(d),
                  pl.BlockSpec((1, SUBLANES, d), lambda bi, i: (bi, jnp.maximum(i * r8 - 1, 0), 0)),
                  pl.BlockSpec((1, SUBLANES, d), lambda bi, i: (bi, jnp.minimum((i + 1) * r8, nb8 - 1), 0)),
                  vec, vec, vec, _const_spec((1, d)),
                  tok(DN_WIDTH), tok(DN_WIDTH), tok(POOL_WIDTH),
                  pl.BlockSpec((1, halo, POOL_WIDTH), lambda bi, i: (bi, jnp.maximum(i * hb - 1, 0), 0)),
                  pl.BlockSpec((1, halo, POOL_WIDTH), lambda bi, i: (bi, jnp.minimum((i + 1) * hb, nh - 1), 0)),
                  pl.BlockSpec((tm, POOL_WIDTH), lambda bi, i: (i, 0))]
        + [_const_spec(w.shape) for w in consts],
        out_specs=tok(d),
        out_shape=jax.ShapeDtypeStruct((b, s, d), F32),
        compiler_params=_params(("parallel", "parallel")),
        name="mix",
    )(x, x, x, shift, scale, gate, norm_g, o_f, o_b, pin, pin, pin, inv_cnt, *consts)


def _ffn_kernel(x_ref, sh_ref, sc_ref, gt_ref, g_ref, wgu_ref, wdown_ref, gf_ref, out_ref, *, final):
    dff = wdown_ref.shape[0]
    xm = x_ref[0]
    hb = _modulate(xm, g_ref[...], sh_ref[0], sc_ref[0]).astype(BF16)
    gu = _dot(hb, wgu_ref[...])
    act = _silu(gu[:, :dff]) * gu[:, dff:]
    r = xm + gt_ref[0] * _dot(act, wdown_ref[...])
    if final:
        r = _rms(r) * gf_ref[...]
    out_ref[0] = r


def _ffn_call(x, shift, scale, gate, norm_g, w_gu, w_down, final_g, final):
    b, s, d = x.shape
    tm = min(TOKEN_TILE, s)
    tok = pl.BlockSpec((1, tm, d), lambda bi, i: (bi, i, 0))
    vec = pl.BlockSpec((1, 1, d), lambda bi, i: (bi, 0, 0))
    return pl.pallas_call(
        functools.partial(_ffn_kernel, final=final),
        grid=(b, s // tm),
        in_specs=[tok, vec, vec, vec, _const_spec((1, d)), _const_spec(w_gu.shape),
                  _const_spec(w_down.shape), _const_spec((1, d))],
        out_specs=tok,
        out_shape=jax.ShapeDtypeStruct((b, s, d), F32),
        compiler_params=_params(("parallel", "parallel")),
        name="ffn",
    )(x, shift, scale, gate, norm_g, w_gu, w_down, final_g)


def _block_diag(w):
    g, ci, co = w.shape
    out = jnp.zeros((g * ci, g * co), w.dtype)
    for j in range(g):
        out = out.at[j * ci:(j + 1) * ci, j * co:(j + 1) * co].set(w[j])
    return out


def kernel(x, c, ctx, c_ctx, w_ada, b_ada, norm1_g, norm2_g, w_in, dn_conv_w, dn_a_log, dn_dt_bias,
           dn_norm_g, pool_w, pool_scale, sc_conv_w, w_br_a, w_br_b, w_br_c, w_o, w_gu, w_down,
           final_norm_g):
    bn, seq, d = x.shape
    depth = w_ada.shape[0]
    rows = seq // GRID_W
    off_z = 3 * DN_WIDTH
    off_a = off_z + DN_WIDTH
    off_pool = off_a + 4 * DN_HEADS
    off_sc = off_pool + POOL_WIDTH
    off_gate = off_sc + 3 * SC_WIDTH

    n_c = -(-(bn + 1) // SUBLANES) * SUBLANES
    cs = jnp.concatenate([c, c_ctx[None], jnp.zeros((n_c - bn - 1, d), F32)], axis=0)
    mod = _ada_call(cs, w_ada, b_ada)

    s0 = jnp.zeros((bn, 2, DN_HEADS, DN_HEAD_DIM, DN_HEAD_DIM), F32)
    icnt_lat = _pool_inv_counts(rows, GRID_W)
    icnt_ctx = _pool_inv_counts(ctx.shape[1], 1)
    final_g = final_norm_g.reshape(1, d)
    for l in range(depth):
        wl = w_in[l]
        w_qkv = wl[:, :off_z].astype(BF16)
        w_z = wl[:, off_z:off_a].astype(BF16)
        w_ab = jnp.pad(wl[:, off_a:off_pool], ((0, 0), (0, LANES - 4 * DN_HEADS))).astype(BF16)
        w_pool = wl[:, off_pool:off_sc].astype(BF16)
        w_sc = wl[:, off_sc:off_gate].astype(BF16)
        w_gate = wl[:, off_gate:].astype(BF16)
        alog = jnp.pad(dn_a_log[l].reshape(1, -1), ((0, 0), (0, LANES - 2 * DN_HEADS)))
        dtb = jnp.pad(dn_dt_bias[l].reshape(1, -1), ((0, 0), (0, LANES - 2 * DN_HEADS)))
        n1 = norm1_g[l].reshape(1, d)
        n2 = norm2_g[l].reshape(1, d)
        mix_consts = (w_z, w_sc, w_gate, dn_norm_g[l].reshape(1, -1),
                      _block_diag(pool_w[l]).astype(BF16), pool_scale[l].reshape(1, -1), sc_conv_w[l],
                      w_br_a[l].astype(BF16), w_br_b[l].astype(BF16), w_br_c[l].astype(BF16),
                      w_o[l].astype(BF16))
        wgu = w_gu[l].astype(BF16)
        wdn = w_down[l].astype(BF16)
        lat = [mod[l, :bn, j * d:(j + 1) * d][:, None, :] for j in range(6)]
        cxm = [jnp.broadcast_to(mod[l, bn:bn + 1, j * d:(j + 1) * d][None], (bn, 1, d)) for j in range(6)]

        qkv, gb, pin = _proj_call(ctx, cxm[0], cxm[1], n1, w_qkv, w_ab, w_pool, dn_conv_w[l], alog, dtb)
        o_f, o_b, s_ctx = _scan_call(qkv, gb, s0)
        if l < depth - 1:
            ctx = _mix_call(ctx, cxm[0], cxm[1], cxm[2], n1, o_f, o_b, pin, icnt_ctx, 1, mix_consts)
            ctx = _ffn_call(ctx, cxm[3], cxm[4], cxm[5], n2, wgu, wdn, final_g, False)

        qkv, gb, pin = _proj_call(x, lat[0], lat[1], n1, w_qkv, w_ab, w_pool, dn_conv_w[l], alog, dtb)
        o_f, o_b, _ = _scan_call(qkv, gb, s_ctx)
        x = _mix_call(x, lat[0], lat[1], lat[2], n1, o_f, o_b, pin, icnt_lat, GRID_W, mix_consts)
        x = _ffn_call(x, lat[3], lat[4], lat[5], n2, wgu, wdn, final_g, l == depth - 1)
    return x
```

```python
import functools
import math

import jax
import jax.numpy as jnp
from jax import lax
from jax.experimental import pallas as pl
from jax.experimental.pallas import tpu as pltpu

F32 = jnp.float32
BF16 = jnp.bfloat16

EPS = 1e-6
GRID_W = 64
DN_HEADS = 4
DN_HEAD_DIM = 128
DN_WIDTH = DN_HEADS * DN_HEAD_DIM
POOL_WINDOWS = (2, 4, 8, 16)
POOL_GROUP_DIM = 64
POOL_WIDTH = POOL_GROUP_DIM * len(POOL_WINDOWS)
SC_WIDTH = 256
N_BRANCH = 3

LANES = 128
SUBLANES = 8
BF16_ROWS = 16
SCAN_CHUNK = 128
SCAN_BLOCK = 512
TOKEN_TILE = 512
FFN_SUBTILES = 2
MIX_SUBTILES = 2
VMEM_LIMIT = 56 * 1024 * 1024
NEG_BIG = -1e30


def _dot(a, b):
    return jnp.dot(a.astype(BF16), b.astype(BF16), preferred_element_type=F32)


def _dot_nt(a, b):
    return lax.dot_general(a.astype(BF16), b.astype(BF16), (((1,), (1,)), ((), ())),
                           preferred_element_type=F32)


def _dot_tn(a, b):
    return lax.dot_general(a.astype(BF16), b.astype(BF16), (((0,), (0,)), ((), ())),
                           preferred_element_type=F32)


def _sigmoid(x):
    return 0.5 + 0.5 * jnp.tanh(0.5 * x)


def _silu_of_half(hx):
    return hx + hx * jnp.tanh(hx)


def _silu(x):
    return _silu_of_half(0.5 * x)


def _rms(x):
    return x * lax.rsqrt(jnp.mean(x * x, axis=-1, keepdims=True) + EPS)


def _modulate(x, g, shift, scale):
    return _rms(x) * (g * (1.0 + scale)) + shift


def _halo_modulate(x_ref, xp_ref, xn_ref, g, shift, scale):
    x_ext = jnp.concatenate([xp_ref[0], x_ref[0], xn_ref[0]], axis=0)
    return _modulate(x_ext, g, shift, scale)


def _conv3_ext(p_ext, w_ref, keep_prev, keep_next):
    n = p_ext.shape[0] - 2 * SUBLANES
    pe = jnp.concatenate([p_ext[:SUBLANES] * keep_prev, p_ext[SUBLANES:SUBLANES + n],
                          p_ext[SUBLANES + n:] * keep_next], axis=0)
    return (pe[SUBLANES - 1:SUBLANES - 1 + n] * w_ref[0:1] + pe[SUBLANES:SUBLANES + n] * w_ref[1:2]
            + pe[SUBLANES + 1:SUBLANES + 1 + n] * w_ref[2:3])


def _const_spec(shape):
    nd = len(shape)
    return pl.BlockSpec(shape, lambda *_: (0,) * nd, pipeline_mode=pl.Buffered(1))


def _params(sem):
    return pltpu.CompilerParams(dimension_semantics=sem, vmem_limit_bytes=VMEM_LIMIT)


def _ada_kernel(c_ref, w_ref, b_ref, o_ref):
    a = _silu(c_ref[...])
    w = w_ref[0]
    a_hi, w_hi = a.astype(BF16), w.astype(BF16)
    a_lo = (a - a_hi.astype(F32)).astype(BF16)
    w_lo = (w - w_hi.astype(F32)).astype(BF16)
    mm = functools.partial(jnp.dot, preferred_element_type=F32)
    o_ref[0] = (mm(a_hi, w_hi) + (mm(a_lo, w_hi) + mm(a_hi, w_lo))) + b_ref[0]


def _ada_call(cs, w_ada, b_ada):
    nl, d, n6 = w_ada.shape
    rows = cs.shape[0]
    tn = n6 // 4
    return pl.pallas_call(
        _ada_kernel,
        grid=(nl, n6 // tn),
        in_specs=[pl.BlockSpec((rows, d), lambda l, j: (0, 0)),
                  pl.BlockSpec((1, d, tn), lambda l, j: (l, 0, j)),
                  pl.BlockSpec((1, 1, tn), lambda l, j: (l, 0, j))],
        out_specs=pl.BlockSpec((1, rows, tn), lambda l, j: (l, 0, j)),
        out_shape=jax.ShapeDtypeStruct((nl, rows, n6), F32),
        compiler_params=_params(("parallel", "parallel")),
        name="ada",
    )(cs, w_ada, b_ada.reshape(nl, 1, n6))


def _proj_kernel(x_ref, xp_ref, xn_ref, sh_ref, sc_ref, g_ref, wqkv_ref, wab_ref, wpool_ref,
                 cw_ref, alog_ref, dtb_ref, qkv_ref, gb_ref, pin_ref, pbuf_ref, abuf_ref):
    i = pl.program_id(1)
    last = pl.num_programs(1) - 1
    tm = x_ref.shape[1]
    m = tm // SUBLANES
    h_ext = _halo_modulate(x_ref, xp_ref, xn_ref, g_ref[...], sh_ref[0], sc_ref[0])
    hb_ext = h_ext.astype(BF16)
    ncol = 3 * DN_WIDTH // LANES
    keep_prev, keep_next = jnp.where(i > 0, 1.0, 0.0), jnp.where(i < last, 1.0, 0.0)
    for j in range(ncol):
        cols = slice(j * LANES, (j + 1) * LANES)
        if j % 2 == 0:
            p2 = _mm(hb_ext, wqkv_ref[:, j * LANES:(j + 2) * LANES])
            pbuf_ref[j] = p2[:, :LANES]
            pbuf_ref[j + 1] = p2[:, LANES:]
        w0, w1, w2 = 0.5 * cw_ref[0:1, cols], 0.5 * cw_ref[1:2, cols], 0.5 * cw_ref[2:3, cols]
        blk = [pbuf_ref[j, pl.ds(SUBLANES + r, m, stride=SUBLANES), :] for r in range(SUBLANES)]
        before = pbuf_ref[j, SUBLANES - 1:SUBLANES, :] * keep_prev
        after = pbuf_ref[j, SUBLANES + tm:SUBLANES + tm + 1, :] * keep_next
        prev0 = jnp.concatenate([before, blk[SUBLANES - 1][:m - 1]], axis=0)
        next7 = jnp.concatenate([blk[0][1:], after], axis=0)
        for r in range(SUBLANES):
            prev = blk[r - 1] if r else prev0
            nxt = blk[r + 1] if r < SUBLANES - 1 else next7
            a = _silu_of_half(prev * w0 + blk[r] * w1 + nxt * w2)
            if j < 2 * DN_HEADS:
                a = a * lax.rsqrt(jnp.sum(a * a, axis=-1, keepdims=True) + EPS)
            if j < DN_HEADS:
                a = a * (DN_HEAD_DIM ** -0.5)
            abuf_ref[j, pl.ds(r, m, stride=SUBLANES), :] = a
    for j in range(ncol):
        qkv_ref[0, :, j * LANES:(j + 1) * LANES] = abuf_ref[j].astype(BF16)
    pab = _mm(hb_ext, wab_ref[...])[SUBLANES:SUBLANES + tm]
    z = pab + dtb_ref[...]
    softplus = jnp.maximum(z, 0.0) + jnp.log(1.0 + jnp.exp(-jnp.abs(z)))
    gdec = -jnp.exp(alog_ref[...]) * softplus
    lane = lax.broadcasted_iota(jnp.int32, pab.shape, 1)
    gb_ref[0] = jnp.where(lane < 2 * DN_HEADS, gdec,
                          jnp.where(lane < 4 * DN_HEADS, _sigmoid(pab), 0.0))
    pin_ref[0] = _mm(hb_ext, wpool_ref[...])[SUBLANES:SUBLANES + tm]


def _proj_call(x, shift, scale, norm_g, w_qkv, w_ab, w_pool, conv_w, alog, dtb):
    b, s, d = x.shape
    tm = min(TOKEN_TILE, s)
    nt = s // tm
    r8 = tm // SUBLANES
    nb8 = s // SUBLANES
    tok = lambda n: pl.BlockSpec((1, tm, n), lambda bi, i: (bi, i, 0))
    vec = pl.BlockSpec((1, 1, d), lambda bi, i: (bi, 0, 0))
    return pl.pallas_call(
        _proj_kernel,
        grid=(b, nt),
        in_specs=[tok(d),
                  pl.BlockSpec((1, SUBLANES, d), lambda bi, i: (bi, jnp.maximum(i * r8 - 1, 0), 0)),
                  pl.BlockSpec((1, SUBLANES, d), lambda bi, i: (bi, jnp.minimum((i + 1) * r8, nb8 - 1), 0)),
                  vec, vec, _const_spec((1, d)),
                  _const_spec(w_qkv.shape), _const_spec(w_ab.shape), _const_spec(w_pool.shape),
                  _const_spec(conv_w.shape), _const_spec(alog.shape), _const_spec(dtb.shape)],
        out_specs=[tok(3 * DN_WIDTH), tok(LANES), tok(POOL_WIDTH)],
        out_shape=[jax.ShapeDtypeStruct((b, s, 3 * DN_WIDTH), BF16),
                   jax.ShapeDtypeStruct((b, s, LANES), F32), jax.ShapeDtypeStruct((b, s, POOL_WIDTH), F32)],
        scratch_shapes=[pltpu.VMEM((3 * DN_WIDTH // LANES, tm + 2 * SUBLANES, LANES), F32),
                        pltpu.VMEM((3 * DN_WIDTH // LANES, tm, LANES), F32)],
        compiler_params=_params(("parallel", "parallel")),
        name="proj_in",
    )(x, x, x, shift, scale, norm_g, w_qkv, w_ab, w_pool, conv_w, alog, dtb)


def _prefix_sum_rows(x):
    n = x.shape[0]
    r = lax.broadcasted_iota(jnp.int32, x.shape, 0)
    s = 1
    while s < n:
        x = x + jnp.where(r >= s, pltpu.roll(x, s, axis=0), 0.0)
        s *= 2
    return x


def _mm(a, b):
    return jnp.dot(a, b, preferred_element_type=F32)


def _scan_kernel(qkvf_ref, gf_ref, qkvb_ref, gbk_ref, s0_ref, of_ref, ob_ref, s_ref, msk_ref, tri_ref, *, c):
    nlev = msk_ref.shape[0] - 1
    i = pl.program_id(1)

    @pl.when(i == 0)
    def _():
        s_ref[...] = s0_ref[...]
        row = lax.broadcasted_iota(jnp.int32, (c, c), 0)
        col = lax.broadcasted_iota(jnp.int32, (c, c), 1)

        def same(sh):
            return jnp.right_shift(row, sh) == jnp.right_shift(col, sh)

        one = jnp.ones((c, c), F32)
        zero = jnp.zeros((c, c), F32)
        msk_ref[0] = jnp.where(same(1), jnp.where(row == col, zero, one), zero).astype(BF16)
        for j in range(1, nlev):
            msk_ref[j] = jnp.where(same(j + 1), jnp.where(same(j), zero, one), zero).astype(BF16)
        msk_ref[nlev] = jnp.where(row == col, one, zero).astype(BF16)
        tri_ref[0] = jnp.where(row >= col, 0.0, NEG_BIG)
        tri_ref[1] = jnp.where(row <= col, 0.0, NEG_BIG)

    nblk = qkvf_ref.shape[1] // c
    units = []
    for d, (qkv_ref, gb_ref, o_ref) in enumerate(((qkvf_ref, gf_ref, of_ref), (qkvb_ref, gbk_ref, ob_ref))):
        for g in range(nblk):
            rs = slice(g * c, (g + 1) * c)
            gb = gb_ref[0, rs, :]
            cs = _prefix_sum_rows(gb)
            tot = cs[c - 1:c, :]
            if d == 1:
                cs = tot - cs + gb
            cs_t = cs.T
            for h in range(DN_HEADS):
                gi = d * DN_HEADS + h
                bi = 2 * DN_HEADS + gi
                lo = h * DN_HEAD_DIM
                hs = slice(lo, lo + DN_HEAD_DIM)
                q16 = qkv_ref[0, rs, hs]
                k16 = qkv_ref[0, rs, DN_WIDTH + lo:DN_WIDTH + lo + DN_HEAD_DIM]
                v16 = qkv_ref[0, rs, 2 * DN_WIDTH + lo:2 * DN_WIDTH + lo + DN_HEAD_DIM]
                units.append(dict(d=d, h=h, g=g, rs=rs, hs=hs, o_ref=o_ref, gc=cs[:, gi:gi + 1],
                                  gc_row=cs_t[gi:gi + 1, :], gl=tot[:, gi:gi + 1], beta=gb[:, bi:bi + 1],
                                  q16=q16, k16=k16, q=q16.astype(F32), k=k16.astype(F32),
                                  v=v16.astype(F32)))
    nu = range(len(units))
    dec = [jnp.exp((u["gc"] - u["gc_row"]) + tri_ref[u["d"]]) for u in units]
    egc = [jnp.exp(u["gc"]) for u in units]
    k16 = [u["k16"] for u in units]
    kbeta = [u["k"] * u["beta"] for u in units]
    kq = [_dot_nt(jnp.concatenate([kbeta[j].astype(BF16), units[j]["q16"]], axis=0), k16[j])
          for j in nu]
    a16 = [(kq[j][:c] * dec[j]).astype(BF16) for j in nu]
    aqk16 = [(kq[j][c:] * dec[j]).astype(BF16) for j in nu]
    eye16 = msk_ref[nlev]
    t16 = [eye16 - a16[j] * msk_ref[0] for j in nu]
    for lev in range(1, nlev):
        b = 2 ** lev
        if b < BF16_ROWS:
            p16 = [_mm(t16[j], a16[j] * msk_ref[lev]).astype(BF16) for j in nu]
            t16 = [t16[j] - _mm(p16[j], t16[j]).astype(BF16) for j in nu]
            continue
        upd = [[r0 for r0 in range(0, c, b) if (r0 // b) % 2 == 1 - units[j]["d"]] for j in nu]
        t_sel = [jnp.concatenate([t16[j][r0:r0 + b] for r0 in upd[j]], axis=0) for j in nu]
        p16 = [_mm(t_sel[j], a16[j] * msk_ref[lev]).astype(BF16) for j in nu]
        t_new = [t_sel[j] - _mm(p16[j], t16[j]).astype(BF16) for j in nu]
        t16 = [jnp.concatenate([t_new[j][upd[j].index(r0) * b:(upd[j].index(r0) + 1) * b] if r0 in upd[j]
                                else t16[j][r0:r0 + b] for r0 in range(0, c, b)], axis=0) for j in nu]
    rhs = [jnp.concatenate([units[j]["v"] * units[j]["beta"], kbeta[j] * egc[j]], axis=1) for j in nu]
    sol = [rhs[j] + _mm(t16[j] - eye16, rhs[j].astype(BF16)) for j in nu]
    qd16 = [(units[j]["q"] * egc[j]).astype(BF16) for j in nu]
    kst16 = [(units[j]["k"] * jnp.exp(units[j]["gl"] - units[j]["gc"])).astype(BF16) for j in nu]
    state = {(d, h): s_ref[0, d, h] for d in range(2) for h in range(DN_HEADS)}
    for step in range(nblk):
        cur = [j for j in nu if units[j]["g"] == (step if units[j]["d"] == 0 else nblk - 1 - step)]
        s16 = {j: state[units[j]["d"], units[j]["h"]].astype(BF16) for j in cur}
        ws = {j: _mm(jnp.concatenate([sol[j][:, DN_HEAD_DIM:].astype(BF16), qd16[j]], axis=0), s16[j])
              for j in cur}
        vn16 = {j: (sol[j][:, :DN_HEAD_DIM] - ws[j][:c]).astype(BF16) for j in cur}
        for j in cur:
            u = units[j]
            u["o_ref"][0, u["rs"], u["hs"]] = (ws[j][c:] + _mm(aqk16[j], vn16[j])).astype(BF16)
        for j in cur:
            u = units[j]
            state[u["d"], u["h"]] = (state[u["d"], u["h"]] * jnp.exp(u["gl"])
                                     + lax.dot_general(kst16[j], vn16[j], (((0,), (0,)), ((), ())),
                                                       preferred_element_type=F32))
    for (d, h), val in state.items():
        s_ref[0, d, h] = val


def _scan_call(qkv, gb, s0):
    b, s, _ = qkv.shape
    c = min(SCAN_CHUNK, s)
    blk = min(SCAN_BLOCK, s)
    nb = s // blk
    nlev = int(math.log2(c))
    fwd = lambda n: pl.BlockSpec((1, blk, n), lambda bi, i: (bi, i, 0))
    bwd = lambda n: pl.BlockSpec((1, blk, n), lambda bi, i: (bi, nb - 1 - i, 0))
    st = pl.BlockSpec((1, 2, DN_HEADS, DN_HEAD_DIM, DN_HEAD_DIM), lambda bi, i: (bi, 0, 0, 0, 0))
    return pl.pallas_call(
        functools.partial(_scan_kernel, c=c),
        grid=(b, nb),
        in_specs=[fwd(3 * DN_WIDTH), fwd(LANES), bwd(3 * DN_WIDTH), bwd(LANES), st],
        out_specs=[fwd(DN_WIDTH), bwd(DN_WIDTH), st],
        out_shape=[jax.ShapeDtypeStruct((b, s, DN_WIDTH), BF16)] * 2
        + [jax.ShapeDtypeStruct(s0.shape, F32)],
        scratch_shapes=[pltpu.VMEM((nlev + 1, c, c), BF16), pltpu.VMEM((2, c, c), F32)],
        compiler_params=_params(("parallel", "arbitrary")),
        name="scan",
    )(qkv, gb, qkv, gb, s0)


def _window_offsets(w):
    return -(w // 2), w - 1 - (w // 2)


def _pool_inv_counts(rows, width):
    t = jnp.arange(rows * width, dtype=jnp.int32)
    r, c = t // width, t % width
    cols = []
    for w in POOL_WINDOWS:
        lo, hi = _window_offsets(w)
        cnt_r = jnp.minimum(r + hi + 1, rows) - jnp.maximum(r + lo, 0)
        cnt_c = jnp.minimum(c + hi + 1, width) - jnp.maximum(c + lo, 0)
        inv = 1.0 / (cnt_r * cnt_c).astype(F32)
        cols.append(jnp.broadcast_to(inv[:, None], (rows * width, POOL_GROUP_DIM)))
    return jnp.concatenate(cols, axis=1)


def _window_sums(x, unit, lev_small):
    n = x.shape[0]

    def shifted(a, k):
        return pltpu.roll(a, (k * unit) % n, axis=0)

    sums = {1: x + shifted(x, 1)}
    for lev in range(2, lev_small + 2):
        step = 2 ** (lev - 2)
        sums[lev] = shifted(sums[lev - 1], step) + shifted(sums[lev - 1], -step)
    return sums[lev_small], sums[lev_small + 1]


def _pool_tile(ext, inv_cnt, width):
    halo = SUBLANES * width
    tm = ext.shape[0] - 2 * halo
    pad = SUBLANES
    outs = []
    for slab in range(POOL_WIDTH // LANES):
        lanes = slice(slab * LANES, (slab + 1) * LANES)
        e = ext[:, lanes]
        small_lane = lax.broadcasted_iota(jnp.int32, (tm, LANES), 1) < POOL_GROUP_DIM
        levels = int(math.log2(POOL_WINDOWS[2 * slab]))
        rs, rl = _window_sums(e, width, levels)
        m1 = jnp.where(small_lane, rs[halo:halo + tm], rl[halo:halo + tm])
        if width > 1:
            zeros = jnp.zeros((pad, LANES), F32)
            pieces = []
            for r in range(tm // width):
                pieces += [zeros, m1[r * width:(r + 1) * width], zeros]
            cs, cl = _window_sums(jnp.concatenate(pieces, axis=0), 1, levels)
            stride = width + 2 * pad
            pick = lambda a: jnp.concatenate(
                [a[r * stride + pad:r * stride + pad + width] for r in range(tm // width)], axis=0)
            m1 = jnp.where(small_lane, pick(cs), pick(cl))
        outs.append(m1 * inv_cnt[:, lanes] - e[halo:halo + tm])
    return jnp.concatenate(outs, axis=1)


def _mix_kernel(x_ref, xp_ref, xn_ref, sh_ref, sc_ref, gt_ref, g_ref, of_ref, ob_ref,
                pin_ref, pinp_ref, pinn_ref, icnt_ref,
                wz_ref, wsc_ref, wgate_ref, dng_ref, wpool_ref, pscale_ref, scw_ref,
                wa_ref, wb_ref, wc_ref, wo_ref, out_ref, *, width):
    i = pl.program_id(1)
    last = pl.num_programs(1) - 1
    tm, d = x_ref.shape[1], x_ref.shape[2]
    h_ext = _halo_modulate(x_ref, xp_ref, xn_ref, g_ref[...], sh_ref[0], sc_ref[0])
    keep_prev, keep_next = jnp.where(i > 0, 1.0, 0.0), jnp.where(i < last, 1.0, 0.0)
    pool_ext = jnp.concatenate([pinp_ref[0] * keep_prev, pin_ref[0], pinn_ref[0] * keep_next], axis=0)
    pd = _pool_tile(pool_ext, icnt_ref[...], width)
    psc = _dot(h_ext, wsc_ref[...])
    conv = _conv3_ext(psc[:, 2 * SC_WIDTH:] * psc[:, :SC_WIDTH], scw_ref, keep_prev, keep_next)
    yc_in = psc[SUBLANES:SUBLANES + tm, SC_WIDTH:2 * SC_WIDTH] * conv
    sub = tm // MIX_SUBTILES
    for t in range(MIX_SUBTILES):
        rows = slice(t * sub, (t + 1) * sub)
        hb = h_ext[SUBLANES + t * sub:SUBLANES + (t + 1) * sub].astype(BF16)
        z = _dot(hb, wz_ref[...])
        o = of_ref[0, rows, :].astype(F32) + ob_ref[0, rows, :].astype(F32)
        parts = []
        for h in range(DN_HEADS):
            lo = h * DN_HEAD_DIM
            parts.append(_rms(o[:, lo:lo + DN_HEAD_DIM]) * dng_ref[...] * _silu(z[:, lo:lo + DN_HEAD_DIM]))
        y_a = _dot(jnp.concatenate(parts, axis=1), wa_ref[...])
        y_b = _dot(_dot(pd[rows], wpool_ref[...]) * pscale_ref[...], wb_ref[...])
        y_c = _dot(yc_in[rows], wc_ref[...])
        gates = _sigmoid(_dot(hb, wgate_ref[...]))
        y = gates[:, :d] * y_a + gates[:, d:2 * d] * y_b + gates[:, 2 * d:] * y_c
        out_ref[0, rows, :] = x_ref[0, rows, :] + gt_ref[0] * _dot(y, wo_ref[...])


def _mix_call(x, shift, scale, gate, norm_g, o_f, o_b, pin, inv_cnt, width, consts):
    b, s, d = x.shape
    tm = min(TOKEN_TILE, s)
    nt = s // tm
    r8 = tm // SUBLANES
    nb8 = s // SUBLANES
    halo = SUBLANES * width
    assert tm % halo == 0 and s % tm == 0
    hb, nh = tm // halo, s // halo
    tok = lambda n: pl.BlockSpec((1, tm, n), lambda bi, i: (bi, i, 0))
    vec = pl.BlockSpec((1, 1, d), lambda bi, i: (bi, 0, 0))
    return pl.pallas_call(
        functools.partial(_mix_kernel, width=width),
        grid=(b, nt),
        in_specs=[tok(d),
                  pl.BlockSpec((1, SUBLANES, d), lambda bi, i: (bi, jnp.maximum(i * r8 - 1, 0), 0)),
                  pl.BlockSpec((1, SUBLANES, d), lambda bi, i: (bi, jnp.minimum((i + 1) * r8, nb8 - 1), 0)),
                  vec, vec, vec, _const_spec((1, d)),
                  tok(DN_WIDTH), tok(DN_WIDTH), tok(POOL_WIDTH),
                  pl.BlockSpec((1, halo, POOL_WIDTH), lambda bi, i: (bi, jnp.maximum(i * hb - 1, 0), 0)),
                  pl.BlockSpec((1, halo, POOL_WIDTH), lambda bi, i: (bi, jnp.minimum((i + 1) * hb, nh - 1), 0)),
                  pl.BlockSpec((tm, POOL_WIDTH), lambda bi, i: (i, 0))]
        + [_const_spec(w.shape) for w in consts],
        out_specs=tok(d),
        out_shape=jax.ShapeDtypeStruct((b, s, d), F32),
        compiler_params=_params(("parallel", "parallel")),
        name="mix",
    )(x, x, x, shift, scale, gate, norm_g, o_f, o_b, pin, pin, pin, inv_cnt, *consts)


def _ffn_kernel(x_ref, sh_ref, sc_ref, gt_ref, g_ref, wgu_ref, wdown_ref, gf_ref, out_ref, *, final):
    dff = wdown_ref.shape[0]
    tm = x_ref.shape[1]
    sub = tm // FFN_SUBTILES
    for t in range(FFN_SUBTILES):
        rows = slice(t * sub, (t + 1) * sub)
        xm = x_ref[0, rows, :]
        hb = _modulate(xm, g_ref[...], sh_ref[0], sc_ref[0]).astype(BF16)
        gu = _dot(hb, wgu_ref[...])
        act = _silu(gu[:, :dff]) * gu[:, dff:]
        r = xm + gt_ref[0] * _dot(act, wdown_ref[...])
        if final:
            r = _rms(r) * gf_ref[...]
        out_ref[0, rows, :] = r


def _ffn_call(x, shift, scale, gate, norm_g, w_gu, w_down, final_g, final):
    b, s, d = x.shape
    tm = min(TOKEN_TILE, s)
    tok = pl.BlockSpec((1, tm, d), lambda bi, i: (bi, i, 0))
    vec = pl.BlockSpec((1, 1, d), lambda bi, i: (bi, 0, 0))
    return pl.pallas_call(
        functools.partial(_ffn_kernel, final=final),
        grid=(b, s // tm),
        in_specs=[tok, vec, vec, vec, _const_spec((1, d)), _const_spec(w_gu.shape),
                  _const_spec(w_down.shape), _const_spec((1, d))],
        out_specs=tok,
        out_shape=jax.ShapeDtypeStruct((b, s, d), F32),
        compiler_params=_params(("parallel", "parallel")),
        name="ffn",
    )(x, shift, scale, gate, norm_g, w_gu, w_down, final_g)


def _block_diag(w):
    g, ci, co = w.shape
    out = jnp.zeros((g * ci, g * co), w.dtype)
    for j in range(g):
        out = out.at[j * ci:(j + 1) * ci, j * co:(j + 1) * co].set(w[j])
    return out


def kernel(x, c, ctx, c_ctx, w_ada, b_ada, norm1_g, norm2_g, w_in, dn_conv_w, dn_a_log, dn_dt_bias,
           dn_norm_g, pool_w, pool_scale, sc_conv_w, w_br_a, w_br_b, w_br_c, w_o, w_gu, w_down,
           final_norm_g):
    bn, seq, d = x.shape
    depth = w_ada.shape[0]
    rows = seq // GRID_W
    off_z = 3 * DN_WIDTH
    off_a = off_z + DN_WIDTH
    off_pool = off_a + 4 * DN_HEADS
    off_sc = off_pool + POOL_WIDTH
    off_gate = off_sc + 3 * SC_WIDTH

    n_c = -(-(bn + 1) // SUBLANES) * SUBLANES
    cs = jnp.concatenate([c, c_ctx[None], jnp.zeros((n_c - bn - 1, d), F32)], axis=0)
    mod = _ada_call(cs, w_ada, b_ada)

    s0 = jnp.zeros((bn, 2, DN_HEADS, DN_HEAD_DIM, DN_HEAD_DIM), F32)
    icnt_lat = _pool_inv_counts(rows, GRID_W)
    icnt_ctx = _pool_inv_counts(ctx.shape[1], 1)
    final_g = final_norm_g.reshape(1, d)
    for l in range(depth):
        wl = w_in[l]
        w_qkv = wl[:, :off_z].astype(BF16)
        w_z = wl[:, off_z:off_a].astype(BF16)
        w_ab = jnp.pad(wl[:, off_a:off_pool], ((0, 0), (0, LANES - 4 * DN_HEADS))).astype(BF16)
        w_pool = wl[:, off_pool:off_sc].astype(BF16)
        w_sc = wl[:, off_sc:off_gate].astype(BF16)
        w_gate = wl[:, off_gate:].astype(BF16)
        alog = jnp.pad(dn_a_log[l].reshape(1, -1), ((0, 0), (0, LANES - 2 * DN_HEADS)))
        dtb = jnp.pad(dn_dt_bias[l].reshape(1, -1), ((0, 0), (0, LANES - 2 * DN_HEADS)))
        n1 = norm1_g[l].reshape(1, d)
        n2 = norm2_g[l].reshape(1, d)
        mix_consts = (w_z, w_sc, w_gate, dn_norm_g[l].reshape(1, -1),
                      _block_diag(pool_w[l]).astype(BF16), pool_scale[l].reshape(1, -1), sc_conv_w[l],
                      w_br_a[l].astype(BF16), w_br_b[l].astype(BF16), w_br_c[l].astype(BF16),
                      w_o[l].astype(BF16))
        wgu = w_gu[l].astype(BF16)
        wdn = w_down[l].astype(BF16)
        lat = [mod[l, :bn, j * d:(j + 1) * d][:, None, :] for j in range(6)]
        cxm = [jnp.broadcast_to(mod[l, bn:bn + 1, j * d:(j + 1) * d][None], (bn, 1, d)) for j in range(6)]

        qkv, gb, pin = _proj_call(ctx, cxm[0], cxm[1], n1, w_qkv, w_ab, w_pool, dn_conv_w[l], alog, dtb)
        o_f, o_b, s_ctx = _scan_call(qkv, gb, s0)
        if l < depth - 1:
            ctx = _mix_call(ctx, cxm[0], cxm[1], cxm[2], n1, o_f, o_b, pin, icnt_ctx, 1, mix_consts)
            ctx = _ffn_call(ctx, cxm[3], cxm[4], cxm[5], n2, wgu, wdn, final_g, False)

        qkv, gb, pin = _proj_call(x, lat[0], lat[1], n1, w_qkv, w_ab, w_pool, dn_conv_w[l], alog, dtb)
        o_f, o_b, _ = _scan_call(qkv, gb, s_ctx)
        x = _mix_call(x, lat[0], lat[1], lat[2], n1, o_f, o_b, pin, icnt_lat, GRID_W, mix_consts)
        x = _ffn_call(x, lat[3], lat[4], lat[5], n2, wgu, wdn, final_g, l == depth - 1)
    return x
```

```python
import functools
import math

import jax
import jax.numpy as jnp
from jax import lax
from jax.experimental import pallas as pl
from jax.experimental.pallas import tpu as pltpu

F32 = jnp.float32
BF16 = jnp.bfloat16

EPS = 1e-6
GRID_W = 64
DN_HEADS = 4
DN_HEAD_DIM = 128
DN_WIDTH = DN_HEADS * DN_HEAD_DIM
POOL_WINDOWS = (2, 4, 8, 16)
POOL_GROUP_DIM = 64
POOL_WIDTH = POOL_GROUP_DIM * len(POOL_WINDOWS)
SC_WIDTH = 256
N_BRANCH = 3

LANES = 128
SUBLANES = 8
BF16_ROWS = 16
SCAN_CHUNK = 128
SCAN_BLOCK = 512
TOKEN_TILE = 512
PROJ_TILE = 1024
FFN_SUBTILES = 2
MIX_SUBTILES = 2
VMEM_LIMIT = 56 * 1024 * 1024
NEG_BIG = -1e30


def _dot(a, b):
    return jnp.dot(a.astype(BF16), b.astype(BF16), preferred_element_type=F32)


def _dot_nt(a, b):
    return lax.dot_general(a.astype(BF16), b.astype(BF16), (((1,), (1,)), ((), ())),
                           preferred_element_type=F32)


def _dot_tn(a, b):
    return lax.dot_general(a.astype(BF16), b.astype(BF16), (((0,), (0,)), ((), ())),
                           preferred_element_type=F32)


def _sigmoid(x):
    return 0.5 + 0.5 * jnp.tanh(0.5 * x)


def _silu_of_half(hx):
    return hx + hx * jnp.tanh(hx)


def _silu(x):
    return _silu_of_half(0.5 * x)


def _rms(x):
    return x * lax.rsqrt(jnp.mean(x * x, axis=-1, keepdims=True) + EPS)


def _modulate(x, g, shift, scale):
    return _rms(x) * (g * (1.0 + scale)) + shift


def _halo_modulate(x_ref, xp_ref, xn_ref, g, shift, scale):
    x_ext = jnp.concatenate([xp_ref[0], x_ref[0], xn_ref[0]], axis=0)
    return _modulate(x_ext, g, shift, scale)


def _conv3_ext(p_ext, w_ref, keep_prev, keep_next):
    n = p_ext.shape[0] - 2 * SUBLANES
    pe = jnp.concatenate([p_ext[:SUBLANES] * keep_prev, p_ext[SUBLANES:SUBLANES + n],
                          p_ext[SUBLANES + n:] * keep_next], axis=0)
    return (pe[SUBLANES - 1:SUBLANES - 1 + n] * w_ref[0:1] + pe[SUBLANES:SUBLANES + n] * w_ref[1:2]
            + pe[SUBLANES + 1:SUBLANES + 1 + n] * w_ref[2:3])


def _const_spec(shape):
    nd = len(shape)
    return pl.BlockSpec(shape, lambda *_: (0,) * nd, pipeline_mode=pl.Buffered(1))


def _params(sem):
    return pltpu.CompilerParams(dimension_semantics=sem, vmem_limit_bytes=VMEM_LIMIT)


def _ada_kernel(c_ref, w_ref, b_ref, o_ref):
    a = _silu(c_ref[...])
    w = w_ref[0]
    a_hi, w_hi = a.astype(BF16), w.astype(BF16)
    a_lo = (a - a_hi.astype(F32)).astype(BF16)
    w_lo = (w - w_hi.astype(F32)).astype(BF16)
    mm = functools.partial(jnp.dot, preferred_element_type=F32)
    o_ref[0] = (mm(a_hi, w_hi) + (mm(a_lo, w_hi) + mm(a_hi, w_lo))) + b_ref[0]


def _ada_call(cs, w_ada, b_ada):
    nl, d, n6 = w_ada.shape
    rows = cs.shape[0]
    tn = n6 // 4
    return pl.pallas_call(
        _ada_kernel,
        grid=(nl, n6 // tn),
        in_specs=[pl.BlockSpec((rows, d), lambda l, j: (0, 0)),
                  pl.BlockSpec((1, d, tn), lambda l, j: (l, 0, j)),
                  pl.BlockSpec((1, 1, tn), lambda l, j: (l, 0, j))],
        out_specs=pl.BlockSpec((1, rows, tn), lambda l, j: (l, 0, j)),
        out_shape=jax.ShapeDtypeStruct((nl, rows, n6), F32),
        compiler_params=_params(("parallel", "parallel")),
        name="ada",
    )(cs, w_ada, b_ada.reshape(nl, 1, n6))


def _proj_kernel(x_ref, xp_ref, xn_ref, sh_ref, sc_ref, g_ref, wqkv_ref, wab_ref, wpool_ref,
                 cw_ref, alog_ref, dtb_ref, qkv_ref, gb_ref, pin_ref, pbuf_ref, abuf_ref):
    i = pl.program_id(1)
    last = pl.num_programs(1) - 1
    tm = x_ref.shape[1]
    m = tm // SUBLANES
    h_ext = _halo_modulate(x_ref, xp_ref, xn_ref, g_ref[...], sh_ref[0], sc_ref[0])
    hb_ext = h_ext.astype(BF16)
    ncol = 3 * DN_WIDTH // LANES
    keep_prev, keep_next = jnp.where(i > 0, 1.0, 0.0), jnp.where(i < last, 1.0, 0.0)
    for j in range(ncol):
        cols = slice(j * LANES, (j + 1) * LANES)
        if j % 2 == 0:
            p2 = _mm(hb_ext, wqkv_ref[:, j * LANES:(j + 2) * LANES])
            pbuf_ref[j] = p2[:, :LANES]
            pbuf_ref[j + 1] = p2[:, LANES:]
        w0, w1, w2 = 0.5 * cw_ref[0:1, cols], 0.5 * cw_ref[1:2, cols], 0.5 * cw_ref[2:3, cols]
        blk = [pbuf_ref[j, pl.ds(SUBLANES + r, m, stride=SUBLANES), :] for r in range(SUBLANES)]
        before = pbuf_ref[j, SUBLANES - 1:SUBLANES, :] * keep_prev
        after = pbuf_ref[j, SUBLANES + tm:SUBLANES + tm + 1, :] * keep_next
        prev0 = jnp.concatenate([before, blk[SUBLANES - 1][:m - 1]], axis=0)
        next7 = jnp.concatenate([blk[0][1:], after], axis=0)
        for r in range(SUBLANES):
            prev = blk[r - 1] if r else prev0
            nxt = blk[r + 1] if r < SUBLANES - 1 else next7
            a = _silu_of_half(prev * w0 + blk[r] * w1 + nxt * w2)
            if j < 2 * DN_HEADS:
                a = a * lax.rsqrt(jnp.sum(a * a, axis=-1, keepdims=True) + EPS)
            if j < DN_HEADS:
                a = a * (DN_HEAD_DIM ** -0.5)
            abuf_ref[j, pl.ds(r, m, stride=SUBLANES), :] = a
    for j in range(ncol):
        qkv_ref[0, :, j * LANES:(j + 1) * LANES] = abuf_ref[j].astype(BF16)
    pab = _mm(hb_ext, wab_ref[...])[SUBLANES:SUBLANES + tm]
    z = pab + dtb_ref[...]
    softplus = jnp.maximum(z, 0.0) + jnp.log(1.0 + jnp.exp(-jnp.abs(z)))
    gdec = -jnp.exp(alog_ref[...]) * softplus
    lane = lax.broadcasted_iota(jnp.int32, pab.shape, 1)
    gb_ref[0] = jnp.where(lane < 2 * DN_HEADS, gdec,
                          jnp.where(lane < 4 * DN_HEADS, _sigmoid(pab), 0.0))
    pin_ref[0] = _mm(hb_ext, wpool_ref[...])[SUBLANES:SUBLANES + tm]


def _proj_call(x, shift, scale, norm_g, w_qkv, w_ab, w_pool, conv_w, alog, dtb):
    b, s, d = x.shape
    tm = min(PROJ_TILE, s)
    nt = s // tm
    r8 = tm // SUBLANES
    nb8 = s // SUBLANES
    tok = lambda n: pl.BlockSpec((1, tm, n), lambda bi, i: (bi, i, 0))
    vec = pl.BlockSpec((1, 1, d), lambda bi, i: (bi, 0, 0))
    return pl.pallas_call(
        _proj_kernel,
        grid=(b, nt),
        in_specs=[tok(d),
                  pl.BlockSpec((1, SUBLANES, d), lambda bi, i: (bi, jnp.maximum(i * r8 - 1, 0), 0)),
                  pl.BlockSpec((1, SUBLANES, d), lambda bi, i: (bi, jnp.minimum((i + 1) * r8, nb8 - 1), 0)),
                  vec, vec, _const_spec((1, d)),
                  _const_spec(w_qkv.shape), _const_spec(w_ab.shape), _const_spec(w_pool.shape),
                  _const_spec(conv_w.shape), _const_spec(alog.shape), _const_spec(dtb.shape)],
        out_specs=[tok(3 * DN_WIDTH), tok(LANES), tok(POOL_WIDTH)],
        out_shape=[jax.ShapeDtypeStruct((b, s, 3 * DN_WIDTH), BF16),
                   jax.ShapeDtypeStruct((b, s, LANES), F32), jax.ShapeDtypeStruct((b, s, POOL_WIDTH), F32)],
        scratch_shapes=[pltpu.VMEM((3 * DN_WIDTH // LANES, tm + 2 * SUBLANES, LANES), F32),
                        pltpu.VMEM((3 * DN_WIDTH // LANES, tm, LANES), F32)],
        compiler_params=_params(("parallel", "parallel")),
        name="proj_in",
    )(x, x, x, shift, scale, norm_g, w_qkv, w_ab, w_pool, conv_w, alog, dtb)


def _prefix_sum_rows(x):
    n = x.shape[0]
    r = lax.broadcasted_iota(jnp.int32, x.shape, 0)
    s = 1
    while s < n:
        x = x + jnp.where(r >= s, pltpu.roll(x, s, axis=0), 0.0)
        s *= 2
    return x


def _mm(a, b):
    return jnp.dot(a, b, preferred_element_type=F32)


def _scan_kernel(qkvf_ref, gf_ref, qkvb_ref, gbk_ref, s0_ref, of_ref, ob_ref, s_ref, msk_ref, tri_ref, *, c):
    nlev = msk_ref.shape[0] - 1
    i = pl.program_id(1)

    @pl.when(i == 0)
    def _():
        s_ref[...] = s0_ref[...]
        row = lax.broadcasted_iota(jnp.int32, (c, c), 0)
        col = lax.broadcasted_iota(jnp.int32, (c, c), 1)

        def same(sh):
            return jnp.right_shift(row, sh) == jnp.right_shift(col, sh)

        one = jnp.ones((c, c), F32)
        zero = jnp.zeros((c, c), F32)
        msk_ref[0] = jnp.where(same(1), jnp.where(row == col, zero, one), zero).astype(BF16)
        for j in range(1, nlev):
            msk_ref[j] = jnp.where(same(j + 1), jnp.where(same(j), zero, one), zero).astype(BF16)
        msk_ref[nlev] = jnp.where(row == col, one, zero).astype(BF16)
        tri_ref[0] = jnp.where(row >= col, 0.0, NEG_BIG)
        tri_ref[1] = jnp.where(row <= col, 0.0, NEG_BIG)

    nblk = qkvf_ref.shape[1] // c
    units = []
    for d, (qkv_ref, gb_ref, o_ref) in enumerate(((qkvf_ref, gf_ref, of_ref), (qkvb_ref, gbk_ref, ob_ref))):
        for g in range(nblk):
            rs = slice(g * c, (g + 1) * c)
            gb = gb_ref[0, rs, :]
            cs = _prefix_sum_rows(gb)
            tot = cs[c - 1:c, :]
            if d == 1:
                cs = tot - cs + gb
            cs_t = cs.T
            for h in range(DN_HEADS):
                gi = d * DN_HEADS + h
                bi = 2 * DN_HEADS + gi
                lo = h * DN_HEAD_DIM
                hs = slice(lo, lo + DN_HEAD_DIM)
                q16 = qkv_ref[0, rs, hs]
                k16 = qkv_ref[0, rs, DN_WIDTH + lo:DN_WIDTH + lo + DN_HEAD_DIM]
                v16 = qkv_ref[0, rs, 2 * DN_WIDTH + lo:2 * DN_WIDTH + lo + DN_HEAD_DIM]
                units.append(dict(d=d, h=h, g=g, rs=rs, hs=hs, o_ref=o_ref, gc=cs[:, gi:gi + 1],
                                  gc_row=cs_t[gi:gi + 1, :], gl=tot[:, gi:gi + 1], beta=gb[:, bi:bi + 1],
                                  q16=q16, k16=k16, q=q16.astype(F32), k=k16.astype(F32),
                                  v=v16.astype(F32)))
    nu = range(len(units))
    dec = [jnp.exp((u["gc"] - u["gc_row"]) + tri_ref[u["d"]]) for u in units]
    egc = [jnp.exp(u["gc"]) for u in units]
    k16 = [u["k16"] for u in units]
    kbeta = [u["k"] * u["beta"] for u in units]
    kq = [_dot_nt(jnp.concatenate([kbeta[j].astype(BF16), units[j]["q16"]], axis=0), k16[j])
          for j in nu]
    a16 = [(kq[j][:c] * dec[j]).astype(BF16) for j in nu]
    aqk16 = [(kq[j][c:] * dec[j]).astype(BF16) for j in nu]
    eye16 = msk_ref[nlev]
    t16 = [eye16 - a16[j] * msk_ref[0] for j in nu]
    for lev in range(1, nlev):
        b = 2 ** lev
        if b < BF16_ROWS:
            p16 = [_mm(t16[j], a16[j] * msk_ref[lev]).astype(BF16) for j in nu]
            t16 = [t16[j] - _mm(p16[j], t16[j]).astype(BF16) for j in nu]
            continue
        upd = [[r0 for r0 in range(0, c, b) if (r0 // b) % 2 == 1 - units[j]["d"]] for j in nu]
        t_sel = [jnp.concatenate([t16[j][r0:r0 + b] for r0 in upd[j]], axis=0) for j in nu]
        p16 = [_mm(t_sel[j], a16[j] * msk_ref[lev]).astype(BF16) for j in nu]
        t_new = [t_sel[j] - _mm(p16[j], t16[j]).astype(BF16) for j in nu]
        t16 = [jnp.concatenate([t_new[j][upd[j].index(r0) * b:(upd[j].index(r0) + 1) * b] if r0 in upd[j]
                                else t16[j][r0:r0 + b] for r0 in range(0, c, b)], axis=0) for j in nu]
    e16 = [t16[j] - eye16 for j in nu]
    vbeta = [units[j]["v"] * units[j]["beta"] for j in nu]
    kd16 = [(kbeta[j] * egc[j]).astype(BF16) for j in nu]
    qd16 = [(units[j]["q"] * egc[j]).astype(BF16) for j in nu]
    kst16 = [(units[j]["k"] * jnp.exp(units[j]["gl"] - units[j]["gc"])).astype(BF16) for j in nu]
    state = {(d, h): s_ref[0, d, h] for d in range(2) for h in range(DN_HEADS)}
    for step in range(nblk):
        cur = [j for j in nu if units[j]["g"] == (step if units[j]["d"] == 0 else nblk - 1 - step)]
        s16 = {j: state[units[j]["d"], units[j]["h"]].astype(BF16) for j in cur}
        ws = {j: _mm(jnp.concatenate([kd16[j], qd16[j]], axis=0), s16[j]) for j in cur}
        resid = {j: vbeta[j] - ws[j][:c] for j in cur}
        vn16 = {j: (resid[j] + _mm(e16[j], resid[j].astype(BF16))).astype(BF16) for j in cur}
        for j in cur:
            u = units[j]
            u["o_ref"][0, u["rs"], u["hs"]] = (ws[j][c:] + _mm(aqk16[j], vn16[j])).astype(BF16)
        for j in cur:
            u = units[j]
            state[u["d"], u["h"]] = (state[u["d"], u["h"]] * jnp.exp(u["gl"])
                                     + lax.dot_general(kst16[j], vn16[j], (((0,), (0,)), ((), ())),
                                                       preferred_element_type=F32))
    for (d, h), val in state.items():
        s_ref[0, d, h] = val


def _scan_call(qkv, gb, s0):
    b, s, _ = qkv.shape
    c = min(SCAN_CHUNK, s)
    blk = min(SCAN_BLOCK, s)
    nb = s // blk
    nlev = int(math.log2(c))
    fwd = lambda n: pl.BlockSpec((1, blk, n), lambda bi, i: (bi, i, 0))
    bwd = lambda n: pl.BlockSpec((1, blk, n), lambda bi, i: (bi, nb - 1 - i, 0))
    st = pl.BlockSpec((1, 2, DN_HEADS, DN_HEAD_DIM, DN_HEAD_DIM), lambda bi, i: (bi, 0, 0, 0, 0))
    return pl.pallas_call(
        functools.partial(_scan_kernel, c=c),
        grid=(b, nb),
        in_specs=[fwd(3 * DN_WIDTH), fwd(LANES), bwd(3 * DN_WIDTH), bwd(LANES), st],
        out_specs=[fwd(DN_WIDTH), bwd(DN_WIDTH), st],
        out_shape=[jax.ShapeDtypeStruct((b, s, DN_WIDTH), BF16)] * 2
        + [jax.ShapeDtypeStruct(s0.shape, F32)],
        scratch_shapes=[pltpu.VMEM((nlev + 1, c, c), BF16), pltpu.VMEM((2, c, c), F32)],
        compiler_params=_params(("parallel", "arbitrary")),
        name="scan",
    )(qkv, gb, qkv, gb, s0)


def _window_offsets(w):
    return -(w // 2), w - 1 - (w // 2)


def _pool_inv_counts(rows, width):
    t = jnp.arange(rows * width, dtype=jnp.int32)
    r, c = t // width, t % width
    cols = []
    for w in POOL_WINDOWS:
        lo, hi = _window_offsets(w)
        cnt_r = jnp.minimum(r + hi + 1, rows) - jnp.maximum(r + lo, 0)
        cnt_c = jnp.minimum(c + hi + 1, width) - jnp.maximum(c + lo, 0)
        inv = 1.0 / (cnt_r * cnt_c).astype(F32)
        cols.append(jnp.broadcast_to(inv[:, None], (rows * width, POOL_GROUP_DIM)))
    return jnp.concatenate(cols, axis=1)


def _window_sums(x, unit, lev_small):
    n = x.shape[0]

    def shifted(a, k):
        return pltpu.roll(a, (k * unit) % n, axis=0)

    sums = {1: x + shifted(x, 1)}
    for lev in range(2, lev_small + 2):
        step = 2 ** (lev - 2)
        sums[lev] = shifted(sums[lev - 1], step) + shifted(sums[lev - 1], -step)
    return sums[lev_small], sums[lev_small + 1]


def _pool_tile(ext, inv_cnt, width):
    halo = SUBLANES * width
    tm = ext.shape[0] - 2 * halo
    pad = SUBLANES
    outs = []
    for slab in range(POOL_WIDTH // LANES):
        lanes = slice(slab * LANES, (slab + 1) * LANES)
        e = ext[:, lanes]
        small_lane = lax.broadcasted_iota(jnp.int32, (tm, LANES), 1) < POOL_GROUP_DIM
        levels = int(math.log2(POOL_WINDOWS[2 * slab]))
        rs, rl = _window_sums(e, width, levels)
        m1 = jnp.where(small_lane, rs[halo:halo + tm], rl[halo:halo + tm])
        if width > 1:
            zeros = jnp.zeros((pad, LANES), F32)
            pieces = []
            for r in range(tm // width):
                pieces += [zeros, m1[r * width:(r + 1) * width], zeros]
            cs, cl = _window_sums(jnp.concatenate(pieces, axis=0), 1, levels)
            stride = width + 2 * pad
            pick = lambda a: jnp.concatenate(
                [a[r * stride + pad:r * stride + pad + width] for r in range(tm // width)], axis=0)
            m1 = jnp.where(small_lane, pick(cs), pick(cl))
        outs.append(m1 * inv_cnt[:, lanes] - e[halo:halo + tm])
    return jnp.concatenate(outs, axis=1)


def _mix_kernel(x_ref, xp_ref, xn_ref, sh_ref, sc_ref, gt_ref, g_ref, of_ref, ob_ref,
                pin_ref, pinp_ref, pinn_ref, icnt_ref,
                wz_ref, wsc_ref, wgate_ref, dng_ref, wpool_ref, pscale_ref, scw_ref,
                wa_ref, wb_ref, wc_ref, wo_ref, out_ref, *, width):
    i = pl.program_id(1)
    last = pl.num_programs(1) - 1
    tm, d = x_ref.shape[1], x_ref.shape[2]
    h_ext = _halo_modulate(x_ref, xp_ref, xn_ref, g_ref[...], sh_ref[0], sc_ref[0])
    keep_prev, keep_next = jnp.where(i > 0, 1.0, 0.0), jnp.where(i < last, 1.0, 0.0)
    pool_ext = jnp.concatenate([pinp_ref[0] * keep_prev, pin_ref[0], pinn_ref[0] * keep_next], axis=0)
    pd = _pool_tile(pool_ext, icnt_ref[...], width)
    psc = _dot(h_ext, wsc_ref[...])
    conv = _conv3_ext(psc[:, 2 * SC_WIDTH:] * psc[:, :SC_WIDTH], scw_ref, keep_prev, keep_next)
    yc_in = psc[SUBLANES:SUBLANES + tm, SC_WIDTH:2 * SC_WIDTH] * conv
    sub = tm // MIX_SUBTILES
    for t in range(MIX_SUBTILES):
        rows = slice(t * sub, (t + 1) * sub)
        hb = h_ext[SUBLANES + t * sub:SUBLANES + (t + 1) * sub].astype(BF16)
        z = _dot(hb, wz_ref[...])
        o = of_ref[0, rows, :].astype(F32) + ob_ref[0, rows, :].astype(F32)
        parts = []
        for h in range(DN_HEADS):
            lo = h * DN_HEAD_DIM
            parts.append(_rms(o[:, lo:lo + DN_HEAD_DIM]) * dng_ref[...] * _silu(z[:, lo:lo + DN_HEAD_DIM]))
        y_a = _dot(jnp.concatenate(parts, axis=1), wa_ref[...])
        y_b = _dot(_dot(pd[rows], wpool_ref[...]) * pscale_ref[...], wb_ref[...])
        y_c = _dot(yc_in[rows], wc_ref[...])
        gates = _sigmoid(_dot(hb, wgate_ref[...]))
        y = gates[:, :d] * y_a + gates[:, d:2 * d] * y_b + gates[:, 2 * d:] * y_c
        out_ref[0, rows, :] = x_ref[0, rows, :] + gt_ref[0] * _dot(y, wo_ref[...])


def _mix_call(x, shift, scale, gate, norm_g, o_f, o_b, pin, inv_cnt, width, consts):
    b, s, d = x.shape
    tm = min(TOKEN_TILE, s)
    nt = s // tm
    r8 = tm // SUBLANES
    nb8 = s // SUBLANES
    halo = SUBLANES * width
    assert tm % halo == 0 and s % tm == 0
    hb, nh = tm // halo, s // halo
    tok = lambda n: pl.BlockSpec((1, tm, n), lambda bi, i: (bi, i, 0))
    vec = pl.BlockSpec((1, 1, d), lambda bi, i: (bi, 0, 0))
    return pl.pallas_call(
        functools.partial(_mix_kernel, width=width),
        grid=(b, nt),
        in_specs=[tok(d),
                  pl.BlockSpec((1, SUBLANES, d), lambda bi, i: (bi, jnp.maximum(i * r8 - 1, 0), 0)),
                  pl.BlockSpec((1, SUBLANES, d), lambda bi, i: (bi, jnp.minimum((i + 1) * r8, nb8 - 1), 0)),
                  vec, vec, vec, _const_spec((1, d)),
                  tok(DN_WIDTH), tok(DN_WIDTH), tok(POOL_WIDTH),
                  pl.BlockSpec((1, halo, POOL_WIDTH), lambda bi, i: (bi, jnp.maximum(i * hb - 1, 0), 0)),
                  pl.BlockSpec((1, halo, POOL_WIDTH), lambda bi, i: (bi, jnp.minimum((i + 1) * hb, nh - 1), 0)),
                  pl.BlockSpec((tm, POOL_WIDTH), lambda bi, i: (i, 0))]
        + [_const_spec(w.shape) for w in consts],
        out_specs=tok(d),
        out_shape=jax.ShapeDtypeStruct((b, s, d), F32),
        compiler_params=_params(("parallel", "parallel")),
        name="mix",
    )(x, x, x, shift, scale, gate, norm_g, o_f, o_b, pin, pin, pin, inv_cnt, *consts)


def _ffn_kernel(x_ref, sh_ref, sc_ref, gt_ref, g_ref, wgu_ref, wdown_ref, gf_ref, out_ref, *, final):
    dff = wdown_ref.shape[0]
    tm = x_ref.shape[1]
    sub = tm // FFN_SUBTILES
    for t in range(FFN_SUBTILES):
        rows = slice(t * sub, (t + 1) * sub)
        xm = x_ref[0, rows, :]
        hb = _modulate(xm, g_ref[...], sh_ref[0], sc_ref[0]).astype(BF16)
        gu = _dot(hb, wgu_ref[...])
        act = _silu(gu[:, :dff]) * gu[:, dff:]
        r = xm + gt_ref[0] * _dot(act, wdown_ref[...])
        if final:
            r = _rms(r) * gf_ref[...]
        out_ref[0, rows, :] = r


def _ffn_call(x, shift, scale, gate, norm_g, w_gu, w_down, final_g, final):
    b, s, d = x.shape
    tm = min(TOKEN_TILE, s)
    tok = pl.BlockSpec((1, tm, d), lambda bi, i: (bi, i, 0))
    vec = pl.BlockSpec((1, 1, d), lambda bi, i: (bi, 0, 0))
    return pl.pallas_call(
        functools.partial(_ffn_kernel, final=final),
        grid=(b, s // tm),
        in_specs=[tok, vec, vec, vec, _const_spec((1, d)), _const_spec(w_gu.shape),
                  _const_spec(w_down.shape), _const_spec((1, d))],
        out_specs=tok,
        out_shape=jax.ShapeDtypeStruct((b, s, d), F32),
        compiler_params=_params(("parallel", "parallel")),
        name="ffn",
    )(x, shift, scale, gate, norm_g, w_gu, w_down, final_g)


def _block_diag(w):
    g, ci, co = w.shape
    out = jnp.zeros((g * ci, g * co), w.dtype)
    for j in range(g):
        out = out.at[j * ci:(j + 1) * ci, j * co:(j + 1) * co].set(w[j])
    return out


def kernel(x, c, ctx, c_ctx, w_ada, b_ada, norm1_g, norm2_g, w_in, dn_conv_w, dn_a_log, dn_dt_bias,
           dn_norm_g, pool_w, pool_scale, sc_conv_w, w_br_a, w_br_b, w_br_c, w_o, w_gu, w_down,
           final_norm_g):
    bn, seq, d = x.shape
    depth = w_ada.shape[0]
    rows = seq // GRID_W
    off_z = 3 * DN_WIDTH
    off_a = off_z + DN_WIDTH
    off_pool = off_a + 4 * DN_HEADS
    off_sc = off_pool + POOL_WIDTH
    off_gate = off_sc + 3 * SC_WIDTH

    n_c = -(-(bn + 1) // SUBLANES) * SUBLANES
    cs = jnp.concatenate([c, c_ctx[None], jnp.zeros((n_c - bn - 1, d), F32)], axis=0)
    mod = _ada_call(cs, w_ada, b_ada)

    s0 = jnp.zeros((bn, 2, DN_HEADS, DN_HEAD_DIM, DN_HEAD_DIM), F32)
    icnt_lat = _pool_inv_counts(rows, GRID_W)
    icnt_ctx = _pool_inv_counts(ctx.shape[1], 1)
    final_g = final_norm_g.reshape(1, d)
    for l in range(depth):
        wl = w_in[l]
        w_qkv = wl[:, :off_z].astype(BF16)
        w_z = wl[:, off_z:off_a].astype(BF16)
        w_ab = jnp.pad(wl[:, off_a:off_pool], ((0, 0), (0, LANES - 4 * DN_HEADS))).astype(BF16)
        w_pool = wl[:, off_pool:off_sc].astype(BF16)
        w_sc = wl[:, off_sc:off_gate].astype(BF16)
        w_gate = wl[:, off_gate:].astype(BF16)
        alog = jnp.pad(dn_a_log[l].reshape(1, -1), ((0, 0), (0, LANES - 2 * DN_HEADS)))
        dtb = jnp.pad(dn_dt_bias[l].reshape(1, -1), ((0, 0), (0, LANES - 2 * DN_HEADS)))
        n1 = norm1_g[l].reshape(1, d)
        n2 = norm2_g[l].reshape(1, d)
        mix_consts = (w_z, w_sc, w_gate, dn_norm_g[l].reshape(1, -1),
                      _block_diag(pool_w[l]).astype(BF16), pool_scale[l].reshape(1, -1), sc_conv_w[l],
                      w_br_a[l].astype(BF16), w_br_b[l].astype(BF16), w_br_c[l].astype(BF16),
                      w_o[l].astype(BF16))
        wgu = w_gu[l].astype(BF16)
        wdn = w_down[l].astype(BF16)
        lat = [mod[l, :bn, j * d:(j + 1) * d][:, None, :] for j in range(6)]
        cxm = [jnp.broadcast_to(mod[l, bn:bn + 1, j * d:(j + 1) * d][None], (bn, 1, d)) for j in range(6)]

        qkv, gb, pin = _proj_call(ctx, cxm[0], cxm[1], n1, w_qkv, w_ab, w_pool, dn_conv_w[l], alog, dtb)
        o_f, o_b, s_ctx = _scan_call(qkv, gb, s0)
        if l < depth - 1:
            ctx = _mix_call(ctx, cxm[0], cxm[1], cxm[2], n1, o_f, o_b, pin, icnt_ctx, 1, mix_consts)
            ctx = _ffn_call(ctx, cxm[3], cxm[4], cxm[5], n2, wgu, wdn, final_g, False)

        qkv, gb, pin = _proj_call(x, lat[0], lat[1], n1, w_qkv, w_ab, w_pool, dn_conv_w[l], alog, dtb)
        o_f, o_b, _ = _scan_call(qkv, gb, s_ctx)
        x = _mix_call(x, lat[0], lat[1], lat[2], n1, o_f, o_b, pin, icnt_lat, GRID_W, mix_consts)
        x = _ffn_call(x, lat[3], lat[4], lat[5], n2, wgu, wdn, final_g, l == depth - 1)
    return x
```

```python
import functools
import math

import jax
import jax.numpy as jnp
from jax import lax
from jax.experimental import pallas as pl
from jax.experimental.pallas import tpu as pltpu

F32 = jnp.float32
BF16 = jnp.bfloat16

EPS = 1e-6
GRID_W = 64
DN_HEADS = 4
DN_HEAD_DIM = 128
DN_WIDTH = DN_HEADS * DN_HEAD_DIM
POOL_WINDOWS = (2, 4, 8, 16)
POOL_GROUP_DIM = 64
POOL_WIDTH = POOL_GROUP_DIM * len(POOL_WINDOWS)
SC_WIDTH = 256
N_BRANCH = 3

LANES = 128
SUBLANES = 8
BF16_ROWS = 16
SCAN_CHUNK = 128
SCAN_BLOCK = 128
TOKEN_TILE = 512
FFN_SUBTILES = 2
MIX_SUBTILES = 2
VMEM_LIMIT = 56 * 1024 * 1024
NEG_BIG = -1e30


def _dot(a, b):
    return jnp.dot(a.astype(BF16), b.astype(BF16), preferred_element_type=F32)


def _dot_nt(a, b):
    return lax.dot_general(a.astype(BF16), b.astype(BF16), (((1,), (1,)), ((), ())),
                           preferred_element_type=F32)


def _dot_tn(a, b):
    return lax.dot_general(a.astype(BF16), b.astype(BF16), (((0,), (0,)), ((), ())),
                           preferred_element_type=F32)


def _sigmoid(x):
    return 0.5 + 0.5 * jnp.tanh(0.5 * x)


def _silu_of_half(hx):
    return hx + hx * jnp.tanh(hx)


def _silu(x):
    return _silu_of_half(0.5 * x)


def _rms(x):
    return x * lax.rsqrt(jnp.mean(x * x, axis=-1, keepdims=True) + EPS)


def _modulate(x, g, shift, scale):
    return _rms(x) * (g * (1.0 + scale)) + shift


def _halo_modulate(x_ref, xp_ref, xn_ref, g, shift, scale):
    x_ext = jnp.concatenate([xp_ref[0], x_ref[0], xn_ref[0]], axis=0)
    return _modulate(x_ext, g, shift, scale)


def _conv3_ext(p_ext, w_ref, keep_prev, keep_next):
    n = p_ext.shape[0] - 2 * SUBLANES
    pe = jnp.concatenate([p_ext[:SUBLANES] * keep_prev, p_ext[SUBLANES:SUBLANES + n],
                          p_ext[SUBLANES + n:] * keep_next], axis=0)
    return (pe[SUBLANES - 1:SUBLANES - 1 + n] * w_ref[0:1] + pe[SUBLANES:SUBLANES + n] * w_ref[1:2]
            + pe[SUBLANES + 1:SUBLANES + 1 + n] * w_ref[2:3])


def _const_spec(shape):
    nd = len(shape)
    return pl.BlockSpec(shape, lambda *_: (0,) * nd, pipeline_mode=pl.Buffered(1))


def _params(sem):
    return pltpu.CompilerParams(dimension_semantics=sem, vmem_limit_bytes=VMEM_LIMIT)


def _ada_kernel(c_ref, w_ref, b_ref, o_ref):
    a = _silu(c_ref[...])
    w = w_ref[0]
    a_hi, w_hi = a.astype(BF16), w.astype(BF16)
    a_lo = (a - a_hi.astype(F32)).astype(BF16)
    w_lo = (w - w_hi.astype(F32)).astype(BF16)
    mm = functools.partial(jnp.dot, preferred_element_type=F32)
    o_ref[0] = (mm(a_hi, w_hi) + (mm(a_lo, w_hi) + mm(a_hi, w_lo))) + b_ref[0]


def _ada_call(cs, w_ada, b_ada):
    nl, d, n6 = w_ada.shape
    rows = cs.shape[0]
    tn = n6 // 4
    return pl.pallas_call(
        _ada_kernel,
        grid=(nl, n6 // tn),
        in_specs=[pl.BlockSpec((rows, d), lambda l, j: (0, 0)),
                  pl.BlockSpec((1, d, tn), lambda l, j: (l, 0, j)),
                  pl.BlockSpec((1, 1, tn), lambda l, j: (l, 0, j))],
        out_specs=pl.BlockSpec((1, rows, tn), lambda l, j: (l, 0, j)),
        out_shape=jax.ShapeDtypeStruct((nl, rows, n6), F32),
        compiler_params=_params(("parallel", "parallel")),
        name="ada",
    )(cs, w_ada, b_ada.reshape(nl, 1, n6))


def _proj_kernel(x_ref, xp_ref, xn_ref, sh_ref, sc_ref, g_ref, wqkv_ref, wab_ref, wpool_ref,
                 cw_ref, alog_ref, dtb_ref, qkv_ref, gb_ref, pin_ref, pbuf_ref, abuf_ref):
    i = pl.program_id(1)
    last = pl.num_programs(1) - 1
    tm = x_ref.shape[1]
    m = tm // SUBLANES
    h_ext = _halo_modulate(x_ref, xp_ref, xn_ref, g_ref[...], sh_ref[0], sc_ref[0])
    hb_ext = h_ext.astype(BF16)
    ncol = 3 * DN_WIDTH // LANES
    keep_prev, keep_next = jnp.where(i > 0, 1.0, 0.0), jnp.where(i < last, 1.0, 0.0)
    for j in range(ncol):
        cols = slice(j * LANES, (j + 1) * LANES)
        if j % 2 == 0:
            p2 = _mm(hb_ext, wqkv_ref[:, j * LANES:(j + 2) * LANES])
            pbuf_ref[j] = p2[:, :LANES]
            pbuf_ref[j + 1] = p2[:, LANES:]
        w0, w1, w2 = 0.5 * cw_ref[0:1, cols], 0.5 * cw_ref[1:2, cols], 0.5 * cw_ref[2:3, cols]
        blk = [pbuf_ref[j, pl.ds(SUBLANES + r, m, stride=SUBLANES), :] for r in range(SUBLANES)]
        before = pbuf_ref[j, SUBLANES - 1:SUBLANES, :] * keep_prev
        after = pbuf_ref[j, SUBLANES + tm:SUBLANES + tm + 1, :] * keep_next
        prev0 = jnp.concatenate([before, blk[SUBLANES - 1][:m - 1]], axis=0)
        next7 = jnp.concatenate([blk[0][1:], after], axis=0)
        for r in range(SUBLANES):
            prev = blk[r - 1] if r else prev0
            nxt = blk[r + 1] if r < SUBLANES - 1 else next7
            a = _silu_of_half(prev * w0 + blk[r] * w1 + nxt * w2)
            if j < 2 * DN_HEADS:
                a = a * lax.rsqrt(jnp.sum(a * a, axis=-1, keepdims=True) + EPS)
            if j < DN_HEADS:
                a = a * (DN_HEAD_DIM ** -0.5)
            abuf_ref[j, pl.ds(r, m, stride=SUBLANES), :] = a
    for j in range(ncol):
        qkv_ref[0, :, j * LANES:(j + 1) * LANES] = abuf_ref[j].astype(BF16)
    pab = _mm(hb_ext, wab_ref[...])[SUBLANES:SUBLANES + tm]
    z = pab + dtb_ref[...]
    softplus = jnp.maximum(z, 0.0) + jnp.log(1.0 + jnp.exp(-jnp.abs(z)))
    gdec = -jnp.exp(alog_ref[...]) * softplus
    lane = lax.broadcasted_iota(jnp.int32, pab.shape, 1)
    gb_ref[0] = jnp.where(lane < 2 * DN_HEADS, gdec,
                          jnp.where(lane < 4 * DN_HEADS, _sigmoid(pab), 0.0))
    pin_ref[0] = _mm(hb_ext, wpool_ref[...])[SUBLANES:SUBLANES + tm]


def _proj_call(x, shift, scale, norm_g, w_qkv, w_ab, w_pool, conv_w, alog, dtb):
    b, s, d = x.shape
    tm = min(TOKEN_TILE, s)
    nt = s // tm
    r8 = tm // SUBLANES
    nb8 = s // SUBLANES
    tok = lambda n: pl.BlockSpec((1, tm, n), lambda bi, i: (bi, i, 0))
    vec = pl.BlockSpec((1, 1, d), lambda bi, i: (bi, 0, 0))
    return pl.pallas_call(
        _proj_kernel,
        grid=(b, nt),
        in_specs=[tok(d),
                  pl.BlockSpec((1, SUBLANES, d), lambda bi, i: (bi, jnp.maximum(i * r8 - 1, 0), 0)),
                  pl.BlockSpec((1, SUBLANES, d), lambda bi, i: (bi, jnp.minimum((i + 1) * r8, nb8 - 1), 0)),
                  vec, vec, _const_spec((1, d)),
                  _const_spec(w_qkv.shape), _const_spec(w_ab.shape), _const_spec(w_pool.shape),
                  _const_spec(conv_w.shape), _const_spec(alog.shape), _const_spec(dtb.shape)],
        out_specs=[tok(3 * DN_WIDTH), tok(LANES), tok(POOL_WIDTH)],
        out_shape=[jax.ShapeDtypeStruct((b, s, 3 * DN_WIDTH), BF16),
                   jax.ShapeDtypeStruct((b, s, LANES), F32), jax.ShapeDtypeStruct((b, s, POOL_WIDTH), F32)],
        scratch_shapes=[pltpu.VMEM((3 * DN_WIDTH // LANES, tm + 2 * SUBLANES, LANES), F32),
                        pltpu.VMEM((3 * DN_WIDTH // LANES, tm, LANES), F32)],
        compiler_params=_params(("parallel", "parallel")),
        name="proj_in",
    )(x, x, x, shift, scale, norm_g, w_qkv, w_ab, w_pool, conv_w, alog, dtb)


def _prefix_sum_rows(x):
    n = x.shape[0]
    r = lax.broadcasted_iota(jnp.int32, x.shape, 0)
    s = 1
    while s < n:
        x = x + jnp.where(r >= s, pltpu.roll(x, s, axis=0), 0.0)
        s *= 2
    return x


def _mm(a, b):
    return jnp.dot(a, b, preferred_element_type=F32)


def _scan_kernel(qkvf_ref, gf_ref, qkvb_ref, gbk_ref, s0_ref, of_ref, ob_ref, s_ref, msk_ref, tri_ref, *, c):
    nlev = msk_ref.shape[0] - 1
    i = pl.program_id(1)

    @pl.when(i == 0)
    def _():
        s_ref[...] = s0_ref[...]
        row = lax.broadcasted_iota(jnp.int32, (c, c), 0)
        col = lax.broadcasted_iota(jnp.int32, (c, c), 1)

        def same(sh):
            return jnp.right_shift(row, sh) == jnp.right_shift(col, sh)

        one = jnp.ones((c, c), F32)
        zero = jnp.zeros((c, c), F32)
        msk_ref[0] = jnp.where(same(1), jnp.where(row == col, zero, one), zero).astype(BF16)
        for j in range(1, nlev):
            msk_ref[j] = jnp.where(same(j + 1), jnp.where(same(j), zero, one), zero).astype(BF16)
        msk_ref[nlev] = jnp.where(row == col, one, zero).astype(BF16)
        tri_ref[0] = jnp.where(row >= col, 0.0, NEG_BIG)
        tri_ref[1] = jnp.where(row <= col, 0.0, NEG_BIG)

    nbat, nblk = qkvf_ref.shape[0], qkvf_ref.shape[1] // c
    units = []
    for d, (qkv_ref, gb_ref, o_ref) in enumerate(((qkvf_ref, gf_ref, of_ref), (qkvb_ref, gbk_ref, ob_ref))):
        for bx, g in [(bx, g) for bx in range(nbat) for g in range(nblk)]:
            rs = slice(g * c, (g + 1) * c)
            gb = gb_ref[bx, rs, :]
            cs = _prefix_sum_rows(gb)
            tot = cs[c - 1:c, :]
            if d == 1:
                cs = tot - cs + gb
            cs_t = cs.T
            for h in range(DN_HEADS):
                gi = d * DN_HEADS + h
                bi = 2 * DN_HEADS + gi
                lo = h * DN_HEAD_DIM
                hs = slice(lo, lo + DN_HEAD_DIM)
                q16 = qkv_ref[bx, rs, hs]
                k16 = qkv_ref[bx, rs, DN_WIDTH + lo:DN_WIDTH + lo + DN_HEAD_DIM]
                v16 = qkv_ref[bx, rs, 2 * DN_WIDTH + lo:2 * DN_WIDTH + lo + DN_HEAD_DIM]
                units.append(dict(bx=bx, d=d, h=h, g=g, rs=rs, hs=hs, o_ref=o_ref, gc=cs[:, gi:gi + 1],
                                  gc_row=cs_t[gi:gi + 1, :], gl=tot[:, gi:gi + 1], beta=gb[:, bi:bi + 1],
                                  q16=q16, k16=k16, q=q16.astype(F32), k=k16.astype(F32),
                                  v=v16.astype(F32)))
    nu = range(len(units))
    dec = [jnp.exp((u["gc"] - u["gc_row"]) + tri_ref[u["d"]]) for u in units]
    egc = [jnp.exp(u["gc"]) for u in units]
    k16 = [u["k16"] for u in units]
    kbeta = [u["k"] * u["beta"] for u in units]
    kq = [_dot_nt(jnp.concatenate([kbeta[j].astype(BF16), units[j]["q16"]], axis=0), k16[j])
          for j in nu]
    a16 = [(kq[j][:c] * dec[j]).astype(BF16) for j in nu]
    aqk16 = [(kq[j][c:] * dec[j]).astype(BF16) for j in nu]
    eye16 = msk_ref[nlev]
    t16 = [eye16 - a16[j] * msk_ref[0] for j in nu]
    for lev in range(1, nlev):
        b = 2 ** lev
        if b < BF16_ROWS:
            p16 = [_mm(t16[j], a16[j] * msk_ref[lev]).astype(BF16) for j in nu]
            t16 = [t16[j] - _mm(p16[j], t16[j]).astype(BF16) for j in nu]
            continue
        upd = [[r0 for r0 in range(0, c, b) if (r0 // b) % 2 == 1 - units[j]["d"]] for j in nu]
        t_sel = [jnp.concatenate([t16[j][r0:r0 + b] for r0 in upd[j]], axis=0) for j in nu]
        p16 = [_mm(t_sel[j], a16[j] * msk_ref[lev]).astype(BF16) for j in nu]
        t_new = [t_sel[j] - _mm(p16[j], t16[j]).astype(BF16) for j in nu]
        t16 = [jnp.concatenate([t_new[j][upd[j].index(r0) * b:(upd[j].index(r0) + 1) * b] if r0 in upd[j]
                                else t16[j][r0:r0 + b] for r0 in range(0, c, b)], axis=0) for j in nu]
    rhs = [jnp.concatenate([units[j]["v"] * units[j]["beta"], kbeta[j] * egc[j]], axis=1) for j in nu]
    sol = [rhs[j] + _mm(t16[j] - eye16, rhs[j].astype(BF16)) for j in nu]
    qd16 = [(units[j]["q"] * egc[j]).astype(BF16) for j in nu]
    kst16 = [(units[j]["k"] * jnp.exp(units[j]["gl"] - units[j]["gc"])).astype(BF16) for j in nu]
    state = {(bx, d, h): s_ref[bx, d, h] for bx in range(nbat) for d in range(2) for h in range(DN_HEADS)}
    for step in range(nblk):
        cur = [j for j in nu if units[j]["g"] == (step if units[j]["d"] == 0 else nblk - 1 - step)]
        s16 = {j: state[units[j]["bx"], units[j]["d"], units[j]["h"]].astype(BF16) for j in cur}
        ws = {j: _mm(jnp.concatenate([sol[j][:, DN_HEAD_DIM:].astype(BF16), qd16[j]], axis=0), s16[j])
              for j in cur}
        vn16 = {j: (sol[j][:, :DN_HEAD_DIM] - ws[j][:c]).astype(BF16) for j in cur}
        for j in cur:
            u = units[j]
            u["o_ref"][u["bx"], u["rs"], u["hs"]] = (ws[j][c:] + _mm(aqk16[j], vn16[j])).astype(BF16)
        for j in cur:
            u = units[j]
            key = (u["bx"], u["d"], u["h"])
            state[key] = (state[key] * jnp.exp(u["gl"])
                          + lax.dot_general(kst16[j], vn16[j], (((0,), (0,)), ((), ())),
                                            preferred_element_type=F32))
    for (bx, d, h), val in state.items():
        s_ref[bx, d, h] = val


def _scan_call(qkv, gb, s0):
    b, s, _ = qkv.shape
    c = min(SCAN_CHUNK, s)
    blk = min(SCAN_BLOCK, s)
    nb = s // blk
    nlev = int(math.log2(c))
    fwd = lambda n: pl.BlockSpec((b, blk, n), lambda bi, i: (bi, i, 0))
    bwd = lambda n: pl.BlockSpec((b, blk, n), lambda bi, i: (bi, nb - 1 - i, 0))
    st = pl.BlockSpec((b, 2, DN_HEADS, DN_HEAD_DIM, DN_HEAD_DIM), lambda bi, i: (bi, 0, 0, 0, 0))
    return pl.pallas_call(
        functools.partial(_scan_kernel, c=c),
        grid=(1, nb),
        in_specs=[fwd(3 * DN_WIDTH), fwd(LANES), bwd(3 * DN_WIDTH), bwd(LANES), st],
        out_specs=[fwd(DN_WIDTH), bwd(DN_WIDTH), st],
        out_shape=[jax.ShapeDtypeStruct((b, s, DN_WIDTH), BF16)] * 2
        + [jax.ShapeDtypeStruct(s0.shape, F32)],
        scratch_shapes=[pltpu.VMEM((nlev + 1, c, c), BF16), pltpu.VMEM((2, c, c), F32)],
        compiler_params=_params(("parallel", "arbitrary")),
        name="scan",
    )(qkv, gb, qkv, gb, s0)


def _window_offsets(w):
    return -(w // 2), w - 1 - (w // 2)


def _pool_inv_counts(rows, width):
    t = jnp.arange(rows * width, dtype=jnp.int32)
    r, c = t // width, t % width
    cols = []
    for w in POOL_WINDOWS:
        lo, hi = _window_offsets(w)
        cnt_r = jnp.minimum(r + hi + 1, rows) - jnp.maximum(r + lo, 0)
        cnt_c = jnp.minimum(c + hi + 1, width) - jnp.maximum(c + lo, 0)
        inv = 1.0 / (cnt_r * cnt_c).astype(F32)
        cols.append(jnp.broadcast_to(inv[:, None], (rows * width, POOL_GROUP_DIM)))
    return jnp.concatenate(cols, axis=1)


def _window_sums(x, unit, lev_small):
    n = x.shape[0]

    def shifted(a, k):
        return pltpu.roll(a, (k * unit) % n, axis=0)

    sums = {1: x + shifted(x, 1)}
    for lev in range(2, lev_small + 2):
        step = 2 ** (lev - 2)
        sums[lev] = shifted(sums[lev - 1], step) + shifted(sums[lev - 1], -step)
    return sums[lev_small], sums[lev_small + 1]


def _pool_tile(ext, inv_cnt, width):
    halo = SUBLANES * width
    tm = ext.shape[0] - 2 * halo
    pad = SUBLANES
    outs = []
    for slab in range(POOL_WIDTH // LANES):
        lanes = slice(slab * LANES, (slab + 1) * LANES)
        e = ext[:, lanes]
        small_lane = lax.broadcasted_iota(jnp.int32, (tm, LANES), 1) < POOL_GROUP_DIM
        levels = int(math.log2(POOL_WINDOWS[2 * slab]))
        rs, rl = _window_sums(e, width, levels)
        m1 = jnp.where(small_lane, rs[halo:halo + tm], rl[halo:halo + tm])
        if width > 1:
            zeros = jnp.zeros((pad, LANES), F32)
            pieces = []
            for r in range(tm // width):
                pieces += [zeros, m1[r * width:(r + 1) * width], zeros]
            cs, cl = _window_sums(jnp.concatenate(pieces, axis=0), 1, levels)
            stride = width + 2 * pad
            pick = lambda a: jnp.concatenate(
                [a[r * stride + pad:r * stride + pad + width] for r in range(tm // width)], axis=0)
            m1 = jnp.where(small_lane, pick(cs), pick(cl))
        outs.append(m1 * inv_cnt[:, lanes] - e[halo:halo + tm])
    return jnp.concatenate(outs, axis=1)


def _mix_kernel(x_ref, xp_ref, xn_ref, sh_ref, sc_ref, gt_ref, g_ref, of_ref, ob_ref,
                pin_ref, pinp_ref, pinn_ref, icnt_ref,
                wz_ref, wsc_ref, wgate_ref, dng_ref, wpool_ref, pscale_ref, scw_ref,
                wa_ref, wb_ref, wc_ref, wo_ref, out_ref, *, width):
    i = pl.program_id(1)
    last = pl.num_programs(1) - 1
    tm, d = x_ref.shape[1], x_ref.shape[2]
    h_ext = _halo_modulate(x_ref, xp_ref, xn_ref, g_ref[...], sh_ref[0], sc_ref[0])
    keep_prev, keep_next = jnp.where(i > 0, 1.0, 0.0), jnp.where(i < last, 1.0, 0.0)
    pool_ext = jnp.concatenate([pinp_ref[0] * keep_prev, pin_ref[0], pinn_ref[0] * keep_next], axis=0)
    pd = _pool_tile(pool_ext, icnt_ref[...], width)
    psc = _dot(h_ext, wsc_ref[...])
    conv = _conv3_ext(psc[:, 2 * SC_WIDTH:] * psc[:, :SC_WIDTH], scw_ref, keep_prev, keep_next)
    yc_in = psc[SUBLANES:SUBLANES + tm, SC_WIDTH:2 * SC_WIDTH] * conv
    sub = tm // MIX_SUBTILES
    for t in range(MIX_SUBTILES):
        rows = slice(t * sub, (t + 1) * sub)
        hb = h_ext[SUBLANES + t * sub:SUBLANES + (t + 1) * sub].astype(BF16)
        z = _dot(hb, wz_ref[...])
        o = of_ref[0, rows, :].astype(F32) + ob_ref[0, rows, :].astype(F32)
        parts = []
        for h in range(DN_HEADS):
            lo = h * DN_HEAD_DIM
            parts.append(_rms(o[:, lo:lo + DN_HEAD_DIM]) * dng_ref[...] * _silu(z[:, lo:lo + DN_HEAD_DIM]))
        y_a = _dot(jnp.concatenate(parts, axis=1), wa_ref[...])
        y_b = _dot(_dot(pd[rows], wpool_ref[...]) * pscale_ref[...], wb_ref[...])
        y_c = _dot(yc_in[rows], wc_ref[...])
        gates = _sigmoid(_dot(hb, wgate_ref[...]))
        y = gates[:, :d] * y_a + gates[:, d:2 * d] * y_b + gates[:, 2 * d:] * y_c
        out_ref[0, rows, :] = x_ref[0, rows, :] + gt_ref[0] * _dot(y, wo_ref[...])


def _mix_call(x, shift, scale, gate, norm_g, o_f, o_b, pin, inv_cnt, width, consts):
    b, s, d = x.shape
    tm = min(TOKEN_TILE, s)
    nt = s // tm
    r8 = tm // SUBLANES
    nb8 = s // SUBLANES
    halo = SUBLANES * width
    assert tm % halo == 0 and s % tm == 0
    hb, nh = tm // halo, s // halo
    tok = lambda n: pl.BlockSpec((1, tm, n), lambda bi, i: (bi, i, 0))
    vec = pl.BlockSpec((1, 1, d), lambda bi, i: (bi, 0, 0))
    return pl.pallas_call(
        functools.partial(_mix_kernel, width=width),
        grid=(b, nt),
        in_specs=[tok(d),
                  pl.BlockSpec((1, SUBLANES, d), lambda bi, i: (bi, jnp.maximum(i * r8 - 1, 0), 0)),
                  pl.BlockSpec((1, SUBLANES, d), lambda bi, i: (bi, jnp.minimum((i + 1) * r8, nb8 - 1), 0)),
                  vec, vec, vec, _const_spec((1, d)),
                  tok(DN_WIDTH), tok(DN_WIDTH), tok(POOL_WIDTH),
                  pl.BlockSpec((1, halo, POOL_WIDTH), lambda bi, i: (bi, jnp.maximum(i * hb - 1, 0), 0)),
                  pl.BlockSpec((1, halo, POOL_WIDTH), lambda bi, i: (bi, jnp.minimum((i + 1) * hb, nh - 1), 0)),
                  pl.BlockSpec((tm, POOL_WIDTH), lambda bi, i: (i, 0))]
        + [_const_spec(w.shape) for w in consts],
        out_specs=tok(d),
        out_shape=jax.ShapeDtypeStruct((b, s, d), F32),
        compiler_params=_params(("parallel", "parallel")),
        name="mix",
    )(x, x, x, shift, scale, gate, norm_g, o_f, o_b, pin, pin, pin, inv_cnt, *consts)


def _ffn_kernel(x_ref, sh_ref, sc_ref, gt_ref, g_ref, wgu_ref, wdown_ref, gf_ref, out_ref, *, final):
    dff = wdown_ref.shape[0]
    tm = x_ref.shape[1]
    sub = tm // FFN_SUBTILES
    for t in range(FFN_SUBTILES):
        rows = slice(t * sub, (t + 1) * sub)
        xm = x_ref[0, rows, :]
        hb = _modulate(xm, g_ref[...], sh_ref[0], sc_ref[0]).astype(BF16)
        gu = _dot(hb, wgu_ref[...])
        act = _silu(gu[:, :dff]) * gu[:, dff:]
        r = xm + gt_ref[0] * _dot(act, wdown_ref[...])
        if final:
            r = _rms(r) * gf_ref[...]
        out_ref[0, rows, :] = r


def _ffn_call(x, shift, scale, gate, norm_g, w_gu, w_down, final_g, final):
    b, s, d = x.shape
    tm = min(TOKEN_TILE, s)
    tok = pl.BlockSpec((1, tm, d), lambda bi, i: (bi, i, 0))
    vec = pl.BlockSpec((1, 1, d), lambda bi, i: (bi, 0, 0))
    return pl.pallas_call(
        functools.partial(_ffn_kernel, final=final),
        grid=(b, s // tm),
        in_specs=[tok, vec, vec, vec, _const_spec((1, d)), _const_spec(w_gu.shape),
                  _const_spec(w_down.shape), _const_spec((1, d))],
        out_specs=tok,
        out_shape=jax.ShapeDtypeStruct((b, s, d), F32),
        compiler_params=_params(("parallel", "parallel")),
        name="ffn",
    )(x, shift, scale, gate, norm_g, w_gu, w_down, final_g)


def _block_diag(w):
    g, ci, co = w.shape
    out = jnp.zeros((g * ci, g * co), w.dtype)
    for j in range(g):
        out = out.at[j * ci:(j + 1) * ci, j * co:(j + 1) * co].set(w[j])
    return out


def kernel(x, c, ctx, c_ctx, w_ada, b_ada, norm1_g, norm2_g, w_in, dn_conv_w, dn_a_log, dn_dt_bias,
           dn_norm_g, pool_w, pool_scale, sc_conv_w, w_br_a, w_br_b, w_br_c, w_o, w_gu, w_down,
           final_norm_g):
    bn, seq, d = x.shape
    depth = w_ada.shape[0]
    rows = seq // GRID_W
    off_z = 3 * DN_WIDTH
    off_a = off_z + DN_WIDTH
    off_pool = off_a + 4 * DN_HEADS
    off_sc = off_pool + POOL_WIDTH
    off_gate = off_sc + 3 * SC_WIDTH

    n_c = -(-(bn + 1) // SUBLANES) * SUBLANES
    cs = jnp.concatenate([c, c_ctx[None], jnp.zeros((n_c - bn - 1, d), F32)], axis=0)
    mod = _ada_call(cs, w_ada, b_ada)

    s0 = jnp.zeros((bn, 2, DN_HEADS, DN_HEAD_DIM, DN_HEAD_DIM), F32)
    icnt_lat = _pool_inv_counts(rows, GRID_W)
    icnt_ctx = _pool_inv_counts(ctx.shape[1], 1)
    final_g = final_norm_g.reshape(1, d)
    for l in range(depth):
        wl = w_in[l]
        w_qkv = wl[:, :off_z].astype(BF16)
        w_z = wl[:, off_z:off_a].astype(BF16)
        w_ab = jnp.pad(wl[:, off_a:off_pool], ((0, 0), (0, LANES - 4 * DN_HEADS))).astype(BF16)
        w_pool = wl[:, off_pool:off_sc].astype(BF16)
        w_sc = wl[:, off_sc:off_gate].astype(BF16)
        w_gate = wl[:, off_gate:].astype(BF16)
        alog = jnp.pad(dn_a_log[l].reshape(1, -1), ((0, 0), (0, LANES - 2 * DN_HEADS)))
        dtb = jnp.pad(dn_dt_bias[l].reshape(1, -1), ((0, 0), (0, LANES - 2 * DN_HEADS)))
        n1 = norm1_g[l].reshape(1, d)
        n2 = norm2_g[l].reshape(1, d)
        mix_consts = (w_z, w_sc, w_gate, dn_norm_g[l].reshape(1, -1),
                      _block_diag(pool_w[l]).astype(BF16), pool_scale[l].reshape(1, -1), sc_conv_w[l],
                      w_br_a[l].astype(BF16), w_br_b[l].astype(BF16), w_br_c[l].astype(BF16),
                      w_o[l].astype(BF16))
        wgu = w_gu[l].astype(BF16)
        wdn = w_down[l].astype(BF16)
        lat = [mod[l, :bn, j * d:(j + 1) * d][:, None, :] for j in range(6)]
        cxm = [jnp.broadcast_to(mod[l, bn:bn + 1, j * d:(j + 1) * d][None], (bn, 1, d)) for j in range(6)]

        qkv, gb, pin = _proj_call(ctx, cxm[0], cxm[1], n1, w_qkv, w_ab, w_pool, dn_conv_w[l], alog, dtb)
        o_f, o_b, s_ctx = _scan_call(qkv, gb, s0)
        if l < depth - 1:
            ctx = _mix_call(ctx, cxm[0], cxm[1], cxm[2], n1, o_f, o_b, pin, icnt_ctx, 1, mix_consts)
            ctx = _ffn_call(ctx, cxm[3], cxm[4], cxm[5], n2, wgu, wdn, final_g, False)

        qkv, gb, pin = _proj_call(x, lat[0], lat[1], n1, w_qkv, w_ab, w_pool, dn_conv_w[l], alog, dtb)
        o_f, o_b, _ = _scan_call(qkv, gb, s_ctx)
        x = _mix_call(x, lat[0], lat[1], lat[2], n1, o_f, o_b, pin, icnt_lat, GRID_W, mix_consts)
        x = _ffn_call(x, lat[3], lat[4], lat[5], n2, wgu, wdn, final_g, l == depth - 1)
    return x
```

```python
import functools
import math

import jax
import jax.numpy as jnp
from jax import lax
from jax.experimental import pallas as pl
from jax.experimental.pallas import tpu as pltpu

F32 = jnp.float32
BF16 = jnp.bfloat16

EPS = 1e-6
GRID_W = 64
DN_HEADS = 4
DN_HEAD_DIM = 128
DN_WIDTH = DN_HEADS * DN_HEAD_DIM
POOL_WINDOWS = (2, 4, 8, 16)
POOL_GROUP_DIM = 64
POOL_WIDTH = POOL_GROUP_DIM * len(POOL_WINDOWS)
SC_WIDTH = 256
N_BRANCH = 3

LANES = 128
SUBLANES = 8
BF16_ROWS = 16
SCAN_CHUNK = 128
SCAN_BLOCK = 128
TOKEN_TILE = 512
FFN_SUBTILES = 2
MIX_SUBTILES = 2
VMEM_LIMIT = 56 * 1024 * 1024
NEG_BIG = -1e30


def _dot(a, b):
    return jnp.dot(a.astype(BF16), b.astype(BF16), preferred_element_type=F32)


def _dot_nt(a, b):
    return lax.dot_general(a.astype(BF16), b.astype(BF16), (((1,), (1,)), ((), ())),
                           preferred_element_type=F32)


def _dot_tn(a, b):
    return lax.dot_general(a.astype(BF16), b.astype(BF16), (((0,), (0,)), ((), ())),
                           preferred_element_type=F32)


def _sigmoid(x):
    return 0.5 + 0.5 * jnp.tanh(0.5 * x)


def _silu_of_half(hx):
    return hx + hx * jnp.tanh(hx)


def _silu(x):
    return _silu_of_half(0.5 * x)


def _rms(x):
    return x * lax.rsqrt(jnp.mean(x * x, axis=-1, keepdims=True) + EPS)


def _modulate(x, g, shift, scale):
    return _rms(x) * (g * (1.0 + scale)) + shift


def _halo_modulate(x_ref, xp_ref, xn_ref, g, shift, scale):
    x_ext = jnp.concatenate([xp_ref[0], x_ref[0], xn_ref[0]], axis=0)
    return _modulate(x_ext, g, shift, scale)


def _conv3_ext(p_ext, w_ref, keep_prev, keep_next):
    n = p_ext.shape[0] - 2 * SUBLANES
    pe = jnp.concatenate([p_ext[:SUBLANES] * keep_prev, p_ext[SUBLANES:SUBLANES + n],
                          p_ext[SUBLANES + n:] * keep_next], axis=0)
    return (pe[SUBLANES - 1:SUBLANES - 1 + n] * w_ref[0:1] + pe[SUBLANES:SUBLANES + n] * w_ref[1:2]
            + pe[SUBLANES + 1:SUBLANES + 1 + n] * w_ref[2:3])


def _const_spec(shape):
    nd = len(shape)
    return pl.BlockSpec(shape, lambda *_: (0,) * nd, pipeline_mode=pl.Buffered(1))


def _params(sem):
    return pltpu.CompilerParams(dimension_semantics=sem, vmem_limit_bytes=VMEM_LIMIT)


def _ada_kernel(c_ref, w_ref, b_ref, o_ref):
    a = _silu(c_ref[...])
    w = w_ref[0]
    a_hi, w_hi = a.astype(BF16), w.astype(BF16)
    a_lo = (a - a_hi.astype(F32)).astype(BF16)
    w_lo = (w - w_hi.astype(F32)).astype(BF16)
    mm = functools.partial(jnp.dot, preferred_element_type=F32)
    o_ref[0] = (mm(a_hi, w_hi) + (mm(a_lo, w_hi) + mm(a_hi, w_lo))) + b_ref[0]


def _ada_call(cs, w_ada, b_ada):
    nl, d, n6 = w_ada.shape
    rows = cs.shape[0]
    tn = n6 // 4
    return pl.pallas_call(
        _ada_kernel,
        grid=(nl, n6 // tn),
        in_specs=[pl.BlockSpec((rows, d), lambda l, j: (0, 0)),
                  pl.BlockSpec((1, d, tn), lambda l, j: (l, 0, j)),
                  pl.BlockSpec((1, 1, tn), lambda l, j: (l, 0, j))],
        out_specs=pl.BlockSpec((1, rows, tn), lambda l, j: (l, 0, j)),
        out_shape=jax.ShapeDtypeStruct((nl, rows, n6), F32),
        compiler_params=_params(("parallel", "parallel")),
        name="ada",
    )(cs, w_ada, b_ada.reshape(nl, 1, n6))


def _proj_kernel(x_ref, xp_ref, xn_ref, sh_ref, sc_ref, g_ref, wqkv_ref, wab_ref, wpool_ref,
                 cw_ref, alog_ref, dtb_ref, qkv_ref, gb_ref, pin_ref, pbuf_ref, abuf_ref):
    i = pl.program_id(1)
    last = pl.num_programs(1) - 1
    tm = x_ref.shape[1]
    m = tm // SUBLANES
    h_ext = _halo_modulate(x_ref, xp_ref, xn_ref, g_ref[...], sh_ref[0], sc_ref[0])
    hb_ext = h_ext.astype(BF16)
    ncol = 3 * DN_WIDTH // LANES
    keep_prev, keep_next = jnp.where(i > 0, 1.0, 0.0), jnp.where(i < last, 1.0, 0.0)
    for j in range(ncol):
        cols = slice(j * LANES, (j + 1) * LANES)
        if j % 2 == 0:
            p2 = _mm(hb_ext, wqkv_ref[:, j * LANES:(j + 2) * LANES])
            pbuf_ref[j] = p2[:, :LANES]
            pbuf_ref[j + 1] = p2[:, LANES:]
        w0, w1, w2 = 0.5 * cw_ref[0:1, cols], 0.5 * cw_ref[1:2, cols], 0.5 * cw_ref[2:3, cols]
        blk = [pbuf_ref[j, pl.ds(SUBLANES + r, m, stride=SUBLANES), :] for r in range(SUBLANES)]
        before = pbuf_ref[j, SUBLANES - 1:SUBLANES, :] * keep_prev
        after = pbuf_ref[j, SUBLANES + tm:SUBLANES + tm + 1, :] * keep_next
        prev0 = jnp.concatenate([before, blk[SUBLANES - 1][:m - 1]], axis=0)
        next7 = jnp.concatenate([blk[0][1:], after], axis=0)
        for r in range(SUBLANES):
            prev = blk[r - 1] if r else prev0
            nxt = blk[r + 1] if r < SUBLANES - 1 else next7
            a = _silu_of_half(prev * w0 + blk[r] * w1 + nxt * w2)
            if j < 2 * DN_HEADS:
                a = a * lax.rsqrt(jnp.sum(a * a, axis=-1, keepdims=True) + EPS)
            if j < DN_HEADS:
                a = a * (DN_HEAD_DIM ** -0.5)
            abuf_ref[j, pl.ds(r, m, stride=SUBLANES), :] = a
    for j in range(ncol):
        qkv_ref[0, :, j * LANES:(j + 1) * LANES] = abuf_ref[j].astype(BF16)
    pab = _mm(hb_ext, wab_ref[...])[SUBLANES:SUBLANES + tm]
    z = pab + dtb_ref[...]
    softplus = jnp.maximum(z, 0.0) + jnp.log(1.0 + jnp.exp(-jnp.abs(z)))
    gdec = -jnp.exp(alog_ref[...]) * softplus
    lane = lax.broadcasted_iota(jnp.int32, pab.shape, 1)
    gb_ref[0] = jnp.where(lane < 2 * DN_HEADS, gdec,
                          jnp.where(lane < 4 * DN_HEADS, _sigmoid(pab), 0.0))
    pin_ref[0] = _mm(hb_ext, wpool_ref[...])[SUBLANES:SUBLANES + tm]


def _proj_call(x, shift, scale, norm_g, w_qkv, w_ab, w_pool, conv_w, alog, dtb):
    b, s, d = x.shape
    tm = min(TOKEN_TILE, s)
    nt = s // tm
    r8 = tm // SUBLANES
    nb8 = s // SUBLANES
    tok = lambda n: pl.BlockSpec((1, tm, n), lambda bi, i: (bi, i, 0))
    vec = pl.BlockSpec((1, 1, d), lambda bi, i: (bi, 0, 0))
    return pl.pallas_call(
        _proj_kernel,
        grid=(b, nt),
        in_specs=[tok(d),
                  pl.BlockSpec((1, SUBLANES, d), lambda bi, i: (bi, jnp.maximum(i * r8 - 1, 0), 0)),
                  pl.BlockSpec((1, SUBLANES, d), lambda bi, i: (bi, jnp.minimum((i + 1) * r8, nb8 - 1), 0)),
                  vec, vec, _const_spec((1, d)),
                  _const_spec(w_qkv.shape), _const_spec(w_ab.shape), _const_spec(w_pool.shape),
                  _const_spec(conv_w.shape), _const_spec(alog.shape), _const_spec(dtb.shape)],
        out_specs=[tok(3 * DN_WIDTH), tok(LANES), tok(POOL_WIDTH)],
        out_shape=[jax.ShapeDtypeStruct((b, s, 3 * DN_WIDTH), BF16),
                   jax.ShapeDtypeStruct((b, s, LANES), F32), jax.ShapeDtypeStruct((b, s, POOL_WIDTH), F32)],
        scratch_shapes=[pltpu.VMEM((3 * DN_WIDTH // LANES, tm + 2 * SUBLANES, LANES), F32),
                        pltpu.VMEM((3 * DN_WIDTH // LANES, tm, LANES), F32)],
        compiler_params=_params(("parallel", "parallel")),
        name="proj_in",
    )(x, x, x, shift, scale, norm_g, w_qkv, w_ab, w_pool, conv_w, alog, dtb)


def _prefix_sum_rows(x):
    n = x.shape[0]
    r = lax.broadcasted_iota(jnp.int32, x.shape, 0)
    s = 1
    while s < n:
        x = x + jnp.where(r >= s, pltpu.roll(x, s, axis=0), 0.0)
        s *= 2
    return x


def _mm(a, b):
    return jnp.dot(a, b, preferred_element_type=F32)


def _scan_kernel(qkvf_ref, gf_ref, qkvb_ref, gbk_ref, s0_ref, of_ref, ob_ref, s_ref, msk_ref, tri_ref, *, c):
    nlev = msk_ref.shape[0] - 1
    i = pl.program_id(1)

    @pl.when(i == 0)
    def _():
        s_ref[...] = s0_ref[...]
        row = lax.broadcasted_iota(jnp.int32, (c, c), 0)
        col = lax.broadcasted_iota(jnp.int32, (c, c), 1)

        def same(sh):
            return jnp.right_shift(row, sh) == jnp.right_shift(col, sh)

        one = jnp.ones((c, c), F32)
        zero = jnp.zeros((c, c), F32)
        msk_ref[0] = jnp.where(same(1), jnp.where(row == col, zero, one), zero).astype(BF16)
        for j in range(1, nlev):
            msk_ref[j] = jnp.where(same(j + 1), jnp.where(same(j), zero, one), zero).astype(BF16)
        msk_ref[nlev] = jnp.where(row == col, one, zero).astype(BF16)
        tri_ref[0] = jnp.where(row >= col, 0.0, NEG_BIG)
        tri_ref[1] = jnp.where(row <= col, 0.0, NEG_BIG)

    nbat, nblk = qkvf_ref.shape[0], qkvf_ref.shape[1] // c
    units = []
    for d, (qkv_ref, gb_ref, o_ref) in enumerate(((qkvf_ref, gf_ref, of_ref), (qkvb_ref, gbk_ref, ob_ref))):
        for bx, g in [(bx, g) for bx in range(nbat) for g in range(nblk)]:
            rs = slice(g * c, (g + 1) * c)
            gb = gb_ref[bx, rs, :]
            cs = _prefix_sum_rows(gb)
            tot = cs[c - 1:c, :]
            if d == 1:
                cs = tot - cs + gb
            cs_t = cs.T
            for h in range(DN_HEADS):
                gi = d * DN_HEADS + h
                bi = 2 * DN_HEADS + gi
                lo = h * DN_HEAD_DIM
                hs = slice(lo, lo + DN_HEAD_DIM)
                q16 = qkv_ref[bx, rs, hs]
                k16 = qkv_ref[bx, rs, DN_WIDTH + lo:DN_WIDTH + lo + DN_HEAD_DIM]
                v16 = qkv_ref[bx, rs, 2 * DN_WIDTH + lo:2 * DN_WIDTH + lo + DN_HEAD_DIM]
                units.append(dict(bx=bx, d=d, h=h, g=g, rs=rs, hs=hs, o_ref=o_ref, gc=cs[:, gi:gi + 1],
                                  gc_row=cs_t[gi:gi + 1, :], gl=tot[:, gi:gi + 1], beta=gb[:, bi:bi + 1],
                                  q16=q16, k16=k16, q=q16.astype(F32), k=k16.astype(F32),
                                  v=v16.astype(F32)))
    nu = range(len(units))
    dec = [jnp.exp((u["gc"] - u["gc_row"]) + tri_ref[u["d"]]) for u in units]
    egc = [jnp.exp(u["gc"]) for u in units]
    k16 = [u["k16"] for u in units]
    kbeta = [u["k"] * u["beta"] for u in units]
    kq = [_dot_nt(jnp.concatenate([kbeta[j].astype(BF16), units[j]["q16"]], axis=0), k16[j])
          for j in nu]
    a16 = [(kq[j][:c] * dec[j]).astype(BF16) for j in nu]
    aqk16 = [(kq[j][c:] * dec[j]).astype(BF16) for j in nu]
    eye16 = msk_ref[nlev]
    t16 = [eye16 - a16[j] * msk_ref[0] for j in nu]
    for lev in range(1, nlev):
        b = 2 ** lev
        if b < BF16_ROWS:
            p16 = [_mm(t16[j], a16[j] * msk_ref[lev]).astype(BF16) for j in nu]
            t16 = [t16[j] - _mm(p16[j], t16[j]).astype(BF16) for j in nu]
            continue
        upd = [[r0 for r0 in range(0, c, b) if (r0 // b) % 2 == 1 - units[j]["d"]] for j in nu]
        t_sel = [jnp.concatenate([t16[j][r0:r0 + b] for r0 in upd[j]], axis=0) for j in nu]
        p16 = [_mm(t_sel[j], a16[j] * msk_ref[lev]).astype(BF16) for j in nu]
        t_new = [t_sel[j] - _mm(p16[j], t16[j]).astype(BF16) for j in nu]
        t16 = [jnp.concatenate([t_new[j][upd[j].index(r0) * b:(upd[j].index(r0) + 1) * b] if r0 in upd[j]
                                else t16[j][r0:r0 + b] for r0 in range(0, c, b)], axis=0) for j in nu]
    e16 = [t16[j] - eye16 for j in nu]
    vbeta = [units[j]["v"] * units[j]["beta"] for j in nu]
    kd16 = [(kbeta[j] * egc[j]).astype(BF16) for j in nu]
    qd16 = [(units[j]["q"] * egc[j]).astype(BF16) for j in nu]
    kst16 = [(units[j]["k"] * jnp.exp(units[j]["gl"] - units[j]["gc"])).astype(BF16) for j in nu]
    state = {(bx, d, h): s_ref[bx, d, h] for bx in range(nbat) for d in range(2) for h in range(DN_HEADS)}
    for step in range(nblk):
        cur = [j for j in nu if units[j]["g"] == (step if units[j]["d"] == 0 else nblk - 1 - step)]
        s16 = {j: state[units[j]["bx"], units[j]["d"], units[j]["h"]].astype(BF16) for j in cur}
        ws = {j: _mm(jnp.concatenate([kd16[j], qd16[j]], axis=0), s16[j]) for j in cur}
        resid = {j: vbeta[j] - ws[j][:c] for j in cur}
        vn16 = {j: (resid[j] + _mm(e16[j], resid[j].astype(BF16))).astype(BF16) for j in cur}
        for j in cur:
            u = units[j]
            u["o_ref"][u["bx"], u["rs"], u["hs"]] = (ws[j][c:] + _mm(aqk16[j], vn16[j])).astype(BF16)
        for j in cur:
            u = units[j]
            key = (u["bx"], u["d"], u["h"])
            state[key] = (state[key] * jnp.exp(u["gl"])
                          + lax.dot_general(kst16[j], vn16[j], (((0,), (0,)), ((), ())),
                                            preferred_element_type=F32))
    for (bx, d, h), val in state.items():
        s_ref[bx, d, h] = val


def _scan_call(qkv, gb, s0):
    b, s, _ = qkv.shape
    c = min(SCAN_CHUNK, s)
    blk = min(SCAN_BLOCK, s)
    nb = s // blk
    nlev = int(math.log2(c))
    fwd = lambda n: pl.BlockSpec((b, blk, n), lambda bi, i: (bi, i, 0))
    bwd = lambda n: pl.BlockSpec((b, blk, n), lambda bi, i: (bi, nb - 1 - i, 0))
    st = pl.BlockSpec((b, 2, DN_HEADS, DN_HEAD_DIM, DN_HEAD_DIM), lambda bi, i: (bi, 0, 0, 0, 0))
    return pl.pallas_call(
        functools.partial(_scan_kernel, c=c),
        grid=(1, nb),
        in_specs=[fwd(3 * DN_WIDTH), fwd(LANES), bwd(3 * DN_WIDTH), bwd(LANES), st],
        out_specs=[fwd(DN_WIDTH), bwd(DN_WIDTH), st],
        out_shape=[jax.ShapeDtypeStruct((b, s, DN_WIDTH), BF16)] * 2
        + [jax.ShapeDtypeStruct(s0.shape, F32)],
        scratch_shapes=[pltpu.VMEM((nlev + 1, c, c), BF16), pltpu.VMEM((2, c, c), F32)],
        compiler_params=_params(("parallel", "arbitrary")),
        name="scan",
    )(qkv, gb, qkv, gb, s0)


def _window_offsets(w):
    return -(w // 2), w - 1 - (w // 2)


def _pool_inv_counts(rows, width):
    t = jnp.arange(rows * width, dtype=jnp.int32)
    r, c = t // width, t % width
    cols = []
    for w in POOL_WINDOWS:
        lo, hi = _window_offsets(w)
        cnt_r = jnp.minimum(r + hi + 1, rows) - jnp.maximum(r + lo, 0)
        cnt_c = jnp.minimum(c + hi + 1, width) - jnp.maximum(c + lo, 0)
        inv = 1.0 / (cnt_r * cnt_c).astype(F32)
        cols.append(jnp.broadcast_to(inv[:, None], (rows * width, POOL_GROUP_DIM)))
    return jnp.concatenate(cols, axis=1)


def _window_sums(x, unit, lev_small):
    n = x.shape[0]

    def shifted(a, k):
        return pltpu.roll(a, (k * unit) % n, axis=0)

    sums = {1: x + shifted(x, 1)}
    for lev in range(2, lev_small + 2):
        step = 2 ** (lev - 2)
        sums[lev] = shifted(sums[lev - 1], step) + shifted(sums[lev - 1], -step)
    return sums[lev_small], sums[lev_small + 1]


def _pool_tile(ext, inv_cnt, width):
    halo = SUBLANES * width
    tm = ext.shape[0] - 2 * halo
    pad = SUBLANES
    outs = []
    for slab in range(POOL_WIDTH // LANES):
        lanes = slice(slab * LANES, (slab + 1) * LANES)
        e = ext[:, lanes]
        small_lane = lax.broadcasted_iota(jnp.int32, (tm, LANES), 1) < POOL_GROUP_DIM
        levels = int(math.log2(POOL_WINDOWS[2 * slab]))
        rs, rl = _window_sums(e, width, levels)
        m1 = jnp.where(small_lane, rs[halo:halo + tm], rl[halo:halo + tm])
        if width > 1:
            zeros = jnp.zeros((pad, LANES), F32)
            pieces = []
            for r in range(tm // width):
                pieces += [zeros, m1[r * width:(r + 1) * width], zeros]
            cs, cl = _window_sums(jnp.concatenate(pieces, axis=0), 1, levels)
            stride = width + 2 * pad
            pick = lambda a: jnp.concatenate(
                [a[r * stride + pad:r * stride + pad + width] for r in range(tm // width)], axis=0)
            m1 = jnp.where(small_lane, pick(cs), pick(cl))
        outs.append(m1 * inv_cnt[:, lanes] - e[halo:halo + tm])
    return jnp.concatenate(outs, axis=1)


def _mix_kernel(x_ref, xp_ref, xn_ref, sh_ref, sc_ref, gt_ref, g_ref, of_ref, ob_ref,
                pin_ref, pinp_ref, pinn_ref, icnt_ref,
                wz_ref, wsc_ref, wgate_ref, dng_ref, wpool_ref, pscale_ref, scw_ref,
                wa_ref, wb_ref, wc_ref, wo_ref, out_ref, *, width):
    i = pl.program_id(1)
    last = pl.num_programs(1) - 1
    tm, d = x_ref.shape[1], x_ref.shape[2]
    h_ext = _halo_modulate(x_ref, xp_ref, xn_ref, g_ref[...], sh_ref[0], sc_ref[0])
    keep_prev, keep_next = jnp.where(i > 0, 1.0, 0.0), jnp.where(i < last, 1.0, 0.0)
    pool_ext = jnp.concatenate([pinp_ref[0] * keep_prev, pin_ref[0], pinn_ref[0] * keep_next], axis=0)
    pd = _pool_tile(pool_ext, icnt_ref[...], width)
    psc = _dot(h_ext, wsc_ref[...])
    conv = _conv3_ext(psc[:, 2 * SC_WIDTH:] * psc[:, :SC_WIDTH], scw_ref, keep_prev, keep_next)
    yc_in = psc[SUBLANES:SUBLANES + tm, SC_WIDTH:2 * SC_WIDTH] * conv
    sub = tm // MIX_SUBTILES
    for t in range(MIX_SUBTILES):
        rows = slice(t * sub, (t + 1) * sub)
        hb = h_ext[SUBLANES + t * sub:SUBLANES + (t + 1) * sub].astype(BF16)
        z = _dot(hb, wz_ref[...])
        o = of_ref[0, rows, :].astype(F32) + ob_ref[0, rows, :].astype(F32)
        parts = []
        for h in range(DN_HEADS):
            lo = h * DN_HEAD_DIM
            parts.append(_rms(o[:, lo:lo + DN_HEAD_DIM]) * dng_ref[...] * _silu(z[:, lo:lo + DN_HEAD_DIM]))
        y_a = _dot(jnp.concatenate(parts, axis=1), wa_ref[...])
        y_b = _dot(_dot(pd[rows], wpool_ref[...]) * pscale_ref[...], wb_ref[...])
        y_c = _dot(yc_in[rows], wc_ref[...])
        gates = _sigmoid(_dot(hb, wgate_ref[...]))
        y = gates[:, :d] * y_a + gates[:, d:2 * d] * y_b + gates[:, 2 * d:] * y_c
        out_ref[0, rows, :] = x_ref[0, rows, :] + gt_ref[0] * _dot(y, wo_ref[...])


def _mix_call(x, shift, scale, gate, norm_g, o_f, o_b, pin, inv_cnt, width, consts):
    b, s, d = x.shape
    tm = min(TOKEN_TILE, s)
    nt = s // tm
    r8 = tm // SUBLANES
    nb8 = s // SUBLANES
    halo = SUBLANES * width
    assert tm % halo == 0 and s % tm == 0
    hb, nh = tm // halo, s // halo
    tok = lambda n: pl.BlockSpec((1, tm, n), lambda bi, i: (bi, i, 0))
    vec = pl.BlockSpec((1, 1, d), lambda bi, i: (bi, 0, 0))
    return pl.pallas_call(
        functools.partial(_mix_kernel, width=width),
        grid=(b, nt),
        in_specs=[tok(d),
                  pl.BlockSpec((1, SUBLANES, d), lambda bi, i: (bi, jnp.maximum(i * r8 - 1, 0), 0)),
                  pl.BlockSpec((1, SUBLANES, d), lambda bi, i: (bi, jnp.minimum((i + 1) * r8, nb8 - 1), 0)),
                  vec, vec, vec, _const_spec((1, d)),
                  tok(DN_WIDTH), tok(DN_WIDTH), tok(POOL_WIDTH),
                  pl.BlockSpec((1, halo, POOL_WIDTH), lambda bi, i: (bi, jnp.maximum(i * hb - 1, 0), 0)),
                  pl.BlockSpec((1, halo, POOL_WIDTH), lambda bi, i: (bi, jnp.minimum((i + 1) * hb, nh - 1), 0)),
                  pl.BlockSpec((tm, POOL_WIDTH), lambda bi, i: (i, 0))]
        + [_const_spec(w.shape) for w in consts],
        out_specs=tok(d),
        out_shape=jax.ShapeDtypeStruct((b, s, d), F32),
        compiler_params=_params(("parallel", "parallel")),
        name="mix",
    )(x, x, x, shift, scale, gate, norm_g, o_f, o_b, pin, pin, pin, inv_cnt, *consts)


def _ffn_kernel(x_ref, sh_ref, sc_ref, gt_ref, g_ref, wgu_ref, wdown_ref, gf_ref, out_ref, *, final):
    dff = wdown_ref.shape[0]
    tm = x_ref.shape[1]
    sub = tm // FFN_SUBTILES
    for t in range(FFN_SUBTILES):
        rows = slice(t * sub, (t + 1) * sub)
        xm = x_ref[0, rows, :]
        hb = _modulate(xm, g_ref[...], sh_ref[0], sc_ref[0]).astype(BF16)
        gu = _dot(hb, wgu_ref[...])
        act = _silu(gu[:, :dff]) * gu[:, dff:]
        r = xm + gt_ref[0] * _dot(act, wdown_ref[...])
        if final:
            r = _rms(r) * gf_ref[...]
        out_ref[0, rows, :] = r


def _ffn_call(x, shift, scale, gate, norm_g, w_gu, w_down, final_g, final):
    b, s, d = x.shape
    tm = min(TOKEN_TILE, s)
    tok = pl.BlockSpec((1, tm, d), lambda bi, i: (bi, i, 0))
    vec = pl.BlockSpec((1, 1, d), lambda bi, i: (bi, 0, 0))
    return pl.pallas_call(
        functools.partial(_ffn_kernel, final=final),
        grid=(b, s // tm),
        in_specs=[tok, vec, vec, vec, _const_spec((1, d)), _const_spec(w_gu.shape),
                  _const_spec(w_down.shape), _const_spec((1, d))],
        out_specs=tok,
        out_shape=jax.ShapeDtypeStruct((b, s, d), F32),
        compiler_params=_params(("parallel", "parallel")),
        name="ffn",
    )(x, shift, scale, gate, norm_g, w_gu, w_down, final_g)


def _block_diag(w):
    g, ci, co = w.shape
    out = jnp.zeros((g * ci, g * co), w.dtype)
    for j in range(g):
        out = out.at[j * ci:(j + 1) * ci, j * co:(j + 1) * co].set(w[j])
    return out


def kernel(x, c, ctx, c_ctx, w_ada, b_ada, norm1_g, norm2_g, w_in, dn_conv_w, dn_a_log, dn_dt_bias,
           dn_norm_g, pool_w, pool_scale, sc_conv_w, w_br_a, w_br_b, w_br_c, w_o, w_gu, w_down,
           final_norm_g):
    bn, seq, d = x.shape
    depth = w_ada.shape[0]
    rows = seq // GRID_W
    off_z = 3 * DN_WIDTH
    off_a = off_z + DN_WIDTH
    off_pool = off_a + 4 * DN_HEADS
    off_sc = off_pool + POOL_WIDTH
    off_gate = off_sc + 3 * SC_WIDTH

    n_c = -(-(bn + 1) // SUBLANES) * SUBLANES
    cs = jnp.concatenate([c, c_ctx[None], jnp.zeros((n_c - bn - 1, d), F32)], axis=0)
    mod = _ada_call(cs, w_ada, b_ada)

    s0 = jnp.zeros((bn, 2, DN_HEADS, DN_HEAD_DIM, DN_HEAD_DIM), F32)
    icnt_lat = _pool_inv_counts(rows, GRID_W)
    icnt_ctx = _pool_inv_counts(ctx.shape[1], 1)
    final_g = final_norm_g.reshape(1, d)
    for l in range(depth):
        wl = w_in[l]
        w_qkv = wl[:, :off_z].astype(BF16)
        w_z = wl[:, off_z:off_a].astype(BF16)
        w_ab = jnp.pad(wl[:, off_a:off_pool], ((0, 0), (0, LANES - 4 * DN_HEADS))).astype(BF16)
        w_pool = wl[:, off_pool:off_sc].astype(BF16)
        w_sc = wl[:, off_sc:off_gate].astype(BF16)
        w_gate = wl[:, off_gate:].astype(BF16)
        alog = jnp.pad(dn_a_log[l].reshape(1, -1), ((0, 0), (0, LANES - 2 * DN_HEADS)))
        dtb = jnp.pad(dn_dt_bias[l].reshape(1, -1), ((0, 0), (0, LANES - 2 * DN_HEADS)))
        n1 = norm1_g[l].reshape(1, d)
        n2 = norm2_g[l].reshape(1, d)
        mix_consts = (w_z, w_sc, w_gate, dn_norm_g[l].reshape(1, -1),
                      _block_diag(pool_w[l]).astype(BF16), pool_scale[l].reshape(1, -1), sc_conv_w[l],
                      w_br_a[l].astype(BF16), w_br_b[l].astype(BF16), w_br_c[l].astype(BF16),
                      w_o[l].astype(BF16))
        wgu = w_gu[l].astype(BF16)
        wdn = w_down[l].astype(BF16)
        lat = [mod[l, :bn, j * d:(j + 1) * d][:, None, :] for j in range(6)]
        cxm = [jnp.broadcast_to(mod[l, bn:bn + 1, j * d:(j + 1) * d][None], (bn, 1, d)) for j in range(6)]

        qkv, gb, pin = _proj_call(ctx, cxm[0], cxm[1], n1, w_qkv, w_ab, w_pool, dn_conv_w[l], alog, dtb)
        o_f, o_b, s_ctx = _scan_call(qkv, gb, s0)
        if l < depth - 1:
            ctx = _mix_call(ctx, cxm[0], cxm[1], cxm[2], n1, o_f, o_b, pin, icnt_ctx, 1, mix_consts)
            ctx = _ffn_call(ctx, cxm[3], cxm[4], cxm[5], n2, wgu, wdn, final_g, False)

        qkv, gb, pin = _proj_call(x, lat[0], lat[1], n1, w_qkv, w_ab, w_pool, dn_conv_w[l], alog, dtb)
        o_f, o_b, _ = _scan_call(qkv, gb, s_ctx)
        x = _mix_call(x, lat[0], lat[1], lat[2], n1, o_f, o_b, pin, icnt_lat, GRID_W, mix_consts)
        x = _ffn_call(x, lat[3], lat[4], lat[5], n2, wgu, wdn, final_g, l == depth - 1)
    return x
```

```python
import functools
import math

import jax
import jax.numpy as jnp
import numpy as np
from jax import lax
from jax.experimental import pallas as pl
from jax.experimental.pallas import tpu as pltpu

F32 = jnp.float32
BF16 = jnp.bfloat16

EPS = 1e-6
GRID_W = 64
DN_HEADS = 4
DN_HEAD_DIM = 128
DN_WIDTH = DN_HEADS * DN_HEAD_DIM
POOL_WINDOWS = (2, 4, 8, 16)
POOL_GROUP_DIM = 64
POOL_WIDTH = POOL_GROUP_DIM * len(POOL_WINDOWS)
SC_WIDTH = 256
N_BRANCH = 3

LANES = 128
SUBLANES = 8
BF16_ROWS = 16
SCAN_CHUNK = 128
SCAN_BLOCK = 128
TOKEN_TILE = 512
FFN_SUBTILES = 2
MIX_SUBTILES = 2
VMEM_LIMIT = 56 * 1024 * 1024
NEG_BIG = -1e30


def _dot(a, b):
    return jnp.dot(a.astype(BF16), b.astype(BF16), preferred_element_type=F32)


def _dot_nt(a, b):
    return lax.dot_general(a.astype(BF16), b.astype(BF16), (((1,), (1,)), ((), ())),
                           preferred_element_type=F32)


def _dot_tn(a, b):
    return lax.dot_general(a.astype(BF16), b.astype(BF16), (((0,), (0,)), ((), ())),
                           preferred_element_type=F32)


def _sigmoid(x):
    return 0.5 + 0.5 * jnp.tanh(0.5 * x)


def _silu_of_half(hx):
    return hx + hx * jnp.tanh(hx)


def _silu(x):
    return _silu_of_half(0.5 * x)


def _rms(x):
    return x * lax.rsqrt(jnp.mean(x * x, axis=-1, keepdims=True) + EPS)


def _modulate(x, g, shift, scale):
    return _rms(x) * (g * (1.0 + scale)) + shift


def _halo_modulate(x_ref, xp_ref, xn_ref, g, shift, scale):
    x_ext = jnp.concatenate([xp_ref[0], x_ref[0], xn_ref[0]], axis=0)
    return _modulate(x_ext, g, shift, scale)


def _conv3_ext(p_ext, w_ref, keep_prev, keep_next):
    n = p_ext.shape[0] - 2 * SUBLANES
    pe = jnp.concatenate([p_ext[:SUBLANES] * keep_prev, p_ext[SUBLANES:SUBLANES + n],
                          p_ext[SUBLANES + n:] * keep_next], axis=0)
    return (pe[SUBLANES - 1:SUBLANES - 1 + n] * w_ref[0:1] + pe[SUBLANES:SUBLANES + n] * w_ref[1:2]
            + pe[SUBLANES + 1:SUBLANES + 1 + n] * w_ref[2:3])


def _const_spec(shape):
    nd = len(shape)
    return pl.BlockSpec(shape, lambda *_: (0,) * nd, pipeline_mode=pl.Buffered(1))


def _params(sem):
    return pltpu.CompilerParams(dimension_semantics=sem, vmem_limit_bytes=VMEM_LIMIT)


def _ada_kernel(c_ref, w_ref, b_ref, o_ref):
    a = _silu(c_ref[...])
    w = w_ref[0]
    a_hi, w_hi = a.astype(BF16), w.astype(BF16)
    a_lo = (a - a_hi.astype(F32)).astype(BF16)
    w_lo = (w - w_hi.astype(F32)).astype(BF16)
    mm = functools.partial(jnp.dot, preferred_element_type=F32)
    o_ref[0] = (mm(a_hi, w_hi) + (mm(a_lo, w_hi) + mm(a_hi, w_lo))) + b_ref[0]


def _ada_call(cs, w_ada, b_ada):
    nl, d, n6 = w_ada.shape
    rows = cs.shape[0]
    tn = n6 // 4
    return pl.pallas_call(
        _ada_kernel,
        grid=(nl, n6 // tn),
        in_specs=[pl.BlockSpec((rows, d), lambda l, j: (0, 0)),
                  pl.BlockSpec((1, d, tn), lambda l, j: (l, 0, j)),
                  pl.BlockSpec((1, 1, tn), lambda l, j: (l, 0, j))],
        out_specs=pl.BlockSpec((1, rows, tn), lambda l, j: (l, 0, j)),
        out_shape=jax.ShapeDtypeStruct((nl, rows, n6), F32),
        compiler_params=_params(("parallel", "parallel")),
        name="ada",
    )(cs, w_ada, b_ada.reshape(nl, 1, n6))


def _proj_kernel(x_ref, xp_ref, xn_ref, sh_ref, sc_ref, g_ref, wqkv_ref, wab_ref, wpool_ref,
                 cw_ref, alog_ref, dtb_ref, qkv_ref, gb_ref, pin_ref, pbuf_ref, abuf_ref):
    i = pl.program_id(1)
    last = pl.num_programs(1) - 1
    tm = x_ref.shape[1]
    m = tm // SUBLANES
    h_ext = _halo_modulate(x_ref, xp_ref, xn_ref, g_ref[...], sh_ref[0], sc_ref[0])
    hb_ext = h_ext.astype(BF16)
    ncol = 3 * DN_WIDTH // LANES
    keep_prev, keep_next = jnp.where(i > 0, 1.0, 0.0), jnp.where(i < last, 1.0, 0.0)
    for j in range(ncol):
        cols = slice(j * LANES, (j + 1) * LANES)
        if j % 2 == 0:
            p2 = _mm(hb_ext, wqkv_ref[:, j * LANES:(j + 2) * LANES])
            pbuf_ref[j] = p2[:, :LANES]
            pbuf_ref[j + 1] = p2[:, LANES:]
        w0, w1, w2 = 0.5 * cw_ref[0:1, cols], 0.5 * cw_ref[1:2, cols], 0.5 * cw_ref[2:3, cols]
        blk = [pbuf_ref[j, pl.ds(SUBLANES + r, m, stride=SUBLANES), :] for r in range(SUBLANES)]
        before = pbuf_ref[j, SUBLANES - 1:SUBLANES, :] * keep_prev
        after = pbuf_ref[j, SUBLANES + tm:SUBLANES + tm + 1, :] * keep_next
        prev0 = jnp.concatenate([before, blk[SUBLANES - 1][:m - 1]], axis=0)
        next7 = jnp.concatenate([blk[0][1:], after], axis=0)
        for r in range(SUBLANES):
            prev = blk[r - 1] if r else prev0
            nxt = blk[r + 1] if r < SUBLANES - 1 else next7
            a = _silu_of_half(prev * w0 + blk[r] * w1 + nxt * w2)
            if j < 2 * DN_HEADS:
                a = a * lax.rsqrt(jnp.sum(a * a, axis=-1, keepdims=True) + EPS)
            if j < DN_HEADS:
                a = a * (DN_HEAD_DIM ** -0.5)
            abuf_ref[j, pl.ds(r, m, stride=SUBLANES), :] = a
    for j in range(ncol):
        qkv_ref[0, :, j * LANES:(j + 1) * LANES] = abuf_ref[j].astype(BF16)
    pab = _mm(hb_ext, wab_ref[...])[SUBLANES:SUBLANES + tm]
    z = pab + dtb_ref[...]
    softplus = jnp.maximum(z, 0.0) + jnp.log(1.0 + jnp.exp(-jnp.abs(z)))
    gdec = -jnp.exp(alog_ref[...]) * softplus
    lane = lax.broadcasted_iota(jnp.int32, pab.shape, 1)
    gb_ref[0] = jnp.where(lane < 2 * DN_HEADS, gdec,
                          jnp.where(lane < 4 * DN_HEADS, _sigmoid(pab), 0.0))
    pin_ref[0] = _mm(hb_ext, wpool_ref[...])[SUBLANES:SUBLANES + tm]


def _proj_call(x, shift, scale, norm_g, w_qkv, w_ab, w_pool, conv_w, alog, dtb):
    b, s, d = x.shape
    tm = min(TOKEN_TILE, s)
    nt = s // tm
    r8 = tm // SUBLANES
    nb8 = s // SUBLANES
    tok = lambda n: pl.BlockSpec((1, tm, n), lambda bi, i: (bi, i, 0))
    vec = pl.BlockSpec((1, 1, d), lambda bi, i: (bi, 0, 0))
    return pl.pallas_call(
        _proj_kernel,
        grid=(b, nt),
        in_specs=[tok(d),
                  pl.BlockSpec((1, SUBLANES, d), lambda bi, i: (bi, jnp.maximum(i * r8 - 1, 0), 0)),
                  pl.BlockSpec((1, SUBLANES, d), lambda bi, i: (bi, jnp.minimum((i + 1) * r8, nb8 - 1), 0)),
                  vec, vec, _const_spec((1, d)),
                  _const_spec(w_qkv.shape), _const_spec(w_ab.shape), _const_spec(w_pool.shape),
                  _const_spec(conv_w.shape), _const_spec(alog.shape), _const_spec(dtb.shape)],
        out_specs=[tok(3 * DN_WIDTH), tok(LANES), tok(POOL_WIDTH)],
        out_shape=[jax.ShapeDtypeStruct((b, s, 3 * DN_WIDTH), BF16),
                   jax.ShapeDtypeStruct((b, s, LANES), F32), jax.ShapeDtypeStruct((b, s, POOL_WIDTH), F32)],
        scratch_shapes=[pltpu.VMEM((3 * DN_WIDTH // LANES, tm + 2 * SUBLANES, LANES), F32),
                        pltpu.VMEM((3 * DN_WIDTH // LANES, tm, LANES), F32)],
        compiler_params=_params(("parallel", "parallel")),
        name="proj_in",
    )(x, x, x, shift, scale, norm_g, w_qkv, w_ab, w_pool, conv_w, alog, dtb)


def _prefix_sum_rows(x):
    n = x.shape[0]
    r = lax.broadcasted_iota(jnp.int32, x.shape, 0)
    s = 1
    while s < n:
        x = x + jnp.where(r >= s, pltpu.roll(x, s, axis=0), 0.0)
        s *= 2
    return x


def _mm(a, b):
    return jnp.dot(a, b, preferred_element_type=F32)


def _scan_kernel(qkvf_ref, gf_ref, qkvb_ref, gbk_ref, s0_ref, of_ref, ob_ref, s_ref, msk_ref, tri_ref, *, c):
    nlev = msk_ref.shape[0] - 1
    i = pl.program_id(1)

    @pl.when(i == 0)
    def _():
        s_ref[...] = s0_ref[...]
        row = lax.broadcasted_iota(jnp.int32, (c, c), 0)
        col = lax.broadcasted_iota(jnp.int32, (c, c), 1)

        def same(sh):
            return jnp.right_shift(row, sh) == jnp.right_shift(col, sh)

        one = jnp.ones((c, c), F32)
        zero = jnp.zeros((c, c), F32)
        msk_ref[0] = jnp.where(same(1), jnp.where(row == col, zero, one), zero).astype(BF16)
        for j in range(1, nlev):
            msk_ref[j] = jnp.where(same(j + 1), jnp.where(same(j), zero, one), zero).astype(BF16)
        msk_ref[nlev] = jnp.where(row == col, one, zero).astype(BF16)
        tri_ref[0] = jnp.where(row >= col, 0.0, NEG_BIG)
        tri_ref[1] = jnp.where(row <= col, 0.0, NEG_BIG)

    nbat, nblk = qkvf_ref.shape[0], qkvf_ref.shape[1] // c
    units = []
    for d, (qkv_ref, gb_ref, o_ref) in enumerate(((qkvf_ref, gf_ref, of_ref), (qkvb_ref, gbk_ref, ob_ref))):
        for bx, g in [(bx, g) for bx in range(nbat) for g in range(nblk)]:
            rs = slice(g * c, (g + 1) * c)
            gb = gb_ref[bx, rs, :]
            cs = _prefix_sum_rows(gb)
            tot = cs[c - 1:c, :]
            if d == 1:
                cs = tot - cs + gb
            cs_t = cs.T
            for h in range(DN_HEADS):
                gi = d * DN_HEADS + h
                bi = 2 * DN_HEADS + gi
                lo = h * DN_HEAD_DIM
                hs = slice(lo, lo + DN_HEAD_DIM)
                q16 = qkv_ref[bx, rs, hs]
                k16 = qkv_ref[bx, rs, DN_WIDTH + lo:DN_WIDTH + lo + DN_HEAD_DIM]
                v16 = qkv_ref[bx, rs, 2 * DN_WIDTH + lo:2 * DN_WIDTH + lo + DN_HEAD_DIM]
                units.append(dict(bx=bx, d=d, h=h, g=g, rs=rs, hs=hs, o_ref=o_ref, gc=cs[:, gi:gi + 1],
                                  gc_row=cs_t[gi:gi + 1, :], gl=tot[:, gi:gi + 1], beta=gb[:, bi:bi + 1],
                                  q16=q16, k16=k16, q=q16.astype(F32), k=k16.astype(F32),
                                  v=v16.astype(F32)))
    nu = range(len(units))
    dec = [jnp.exp((u["gc"] - u["gc_row"]) + tri_ref[u["d"]]) for u in units]
    egc = [jnp.exp(u["gc"]) for u in units]
    k16 = [u["k16"] for u in units]
    kbeta = [u["k"] * u["beta"] for u in units]
    kq = [_dot_nt(jnp.concatenate([kbeta[j].astype(BF16), units[j]["q16"]], axis=0), k16[j])
          for j in nu]
    a16 = [(kq[j][:c] * dec[j]).astype(BF16) for j in nu]
    aqk16 = [(kq[j][c:] * dec[j]).astype(BF16) for j in nu]
    eye16 = msk_ref[nlev]
    t16 = [eye16 - a16[j] * msk_ref[0] for j in nu]
    for lev in range(1, nlev):
        b = 2 ** lev
        if b < BF16_ROWS:
            p16 = [_mm(t16[j], a16[j] * msk_ref[lev]).astype(BF16) for j in nu]
            t16 = [t16[j] - _mm(p16[j], t16[j]).astype(BF16) for j in nu]
            continue
        upd = [[r0 for r0 in range(0, c, b) if (r0 // b) % 2 == 1 - units[j]["d"]] for j in nu]
        t_sel = [jnp.concatenate([t16[j][r0:r0 + b] for r0 in upd[j]], axis=0) for j in nu]
        p16 = [_mm(t_sel[j], a16[j] * msk_ref[lev]).astype(BF16) for j in nu]
        t_new = [t_sel[j] - _mm(p16[j], t16[j]).astype(BF16) for j in nu]
        t16 = [jnp.concatenate([t_new[j][upd[j].index(r0) * b:(upd[j].index(r0) + 1) * b] if r0 in upd[j]
                                else t16[j][r0:r0 + b] for r0 in range(0, c, b)], axis=0) for j in nu]
    e16 = [t16[j] - eye16 for j in nu]
    vbeta = [units[j]["v"] * units[j]["beta"] for j in nu]
    kd16 = [(kbeta[j] * egc[j]).astype(BF16) for j in nu]
    qd16 = [(units[j]["q"] * egc[j]).astype(BF16) for j in nu]
    kst16 = [(units[j]["k"] * jnp.exp(units[j]["gl"] - units[j]["gc"])).astype(BF16) for j in nu]
    state = {(bx, d, h): s_ref[bx, d, h] for bx in range(nbat) for d in range(2) for h in range(DN_HEADS)}
    for step in range(nblk):
        cur = [j for j in nu if units[j]["g"] == (step if units[j]["d"] == 0 else nblk - 1 - step)]
        s16 = {j: state[units[j]["bx"], units[j]["d"], units[j]["h"]].astype(BF16) for j in cur}
        ws = {j: _mm(jnp.concatenate([kd16[j], qd16[j]], axis=0), s16[j]) for j in cur}
        resid = {j: vbeta[j] - ws[j][:c] for j in cur}
        vn16 = {j: (resid[j] + _mm(e16[j], resid[j].astype(BF16))).astype(BF16) for j in cur}
        for j in cur:
            u = units[j]
            u["o_ref"][u["bx"], u["rs"], u["hs"]] = (ws[j][c:] + _mm(aqk16[j], vn16[j])).astype(BF16)
        for j in cur:
            u = units[j]
            key = (u["bx"], u["d"], u["h"])
            state[key] = (state[key] * jnp.exp(u["gl"])
                          + lax.dot_general(kst16[j], vn16[j], (((0,), (0,)), ((), ())),
                                            preferred_element_type=F32))
    for (bx, d, h), val in state.items():
        s_ref[bx, d, h] = val


def _scan_call(qkv, gb, s0):
    b, s, _ = qkv.shape
    c = min(SCAN_CHUNK, s)
    blk = min(SCAN_BLOCK, s)
    nb = s // blk
    nlev = int(math.log2(c))
    fwd = lambda n: pl.BlockSpec((b, blk, n), lambda bi, i: (bi, i, 0))
    bwd = lambda n: pl.BlockSpec((b, blk, n), lambda bi, i: (bi, nb - 1 - i, 0))
    st = pl.BlockSpec((b, 2, DN_HEADS, DN_HEAD_DIM, DN_HEAD_DIM), lambda bi, i: (bi, 0, 0, 0, 0))
    return pl.pallas_call(
        functools.partial(_scan_kernel, c=c),
        grid=(1, nb),
        in_specs=[fwd(3 * DN_WIDTH), fwd(LANES), bwd(3 * DN_WIDTH), bwd(LANES), st],
        out_specs=[fwd(DN_WIDTH), bwd(DN_WIDTH), st],
        out_shape=[jax.ShapeDtypeStruct((b, s, DN_WIDTH), BF16)] * 2
        + [jax.ShapeDtypeStruct(s0.shape, F32)],
        scratch_shapes=[pltpu.VMEM((nlev + 1, c, c), BF16), pltpu.VMEM((2, c, c), F32)],
        compiler_params=_params(("parallel", "arbitrary")),
        name="scan",
    )(qkv, gb, qkv, gb, s0)


def _window_offsets(w):
    return -(w // 2), w - 1 - (w // 2)


def _pool_inv_counts(rows, width):
    t = np.arange(rows * width, dtype=np.int64)
    r, c = t // width, t % width
    cols = []
    for w in POOL_WINDOWS:
        lo, hi = _window_offsets(w)
        cnt_r = np.minimum(r + hi + 1, rows) - np.maximum(r + lo, 0)
        cnt_c = np.minimum(c + hi + 1, width) - np.maximum(c + lo, 0)
        inv = (1.0 / (cnt_r * cnt_c)).astype(np.float32)
        cols.append(np.broadcast_to(inv[:, None], (rows * width, POOL_GROUP_DIM)))
    return jnp.asarray(np.concatenate(cols, axis=1))


def _window_sums(x, unit, lev_small):
    n = x.shape[0]

    def shifted(a, k):
        return pltpu.roll(a, (k * unit) % n, axis=0)

    sums = {1: x + shifted(x, 1)}
    for lev in range(2, lev_small + 2):
        step = 2 ** (lev - 2)
        sums[lev] = shifted(sums[lev - 1], step) + shifted(sums[lev - 1], -step)
    return sums[lev_small], sums[lev_small + 1]


def _pool_tile(ext, inv_cnt, width):
    halo = SUBLANES * width
    tm = ext.shape[0] - 2 * halo
    pad = SUBLANES
    outs = []
    for slab in range(POOL_WIDTH // LANES):
        lanes = slice(slab * LANES, (slab + 1) * LANES)
        e = ext[:, lanes]
        small_lane = lax.broadcasted_iota(jnp.int32, (tm, LANES), 1) < POOL_GROUP_DIM
        levels = int(math.log2(POOL_WINDOWS[2 * slab]))
        rs, rl = _window_sums(e, width, levels)
        m1 = jnp.where(small_lane, rs[halo:halo + tm], rl[halo:halo + tm])
        if width > 1:
            zeros = jnp.zeros((pad, LANES), F32)
            pieces = []
            for r in range(tm // width):
                pieces += [zeros, m1[r * width:(r + 1) * width], zeros]
            cs, cl = _window_sums(jnp.concatenate(pieces, axis=0), 1, levels)
            stride = width + 2 * pad
            pick = lambda a: jnp.concatenate(
                [a[r * stride + pad:r * stride + pad + width] for r in range(tm // width)], axis=0)
            m1 = jnp.where(small_lane, pick(cs), pick(cl))
        outs.append(m1 * inv_cnt[:, lanes] - e[halo:halo + tm])
    return jnp.concatenate(outs, axis=1)


def _mix_kernel(x_ref, xp_ref, xn_ref, sh_ref, sc_ref, gt_ref, g_ref, of_ref, ob_ref,
                pin_ref, pinp_ref, pinn_ref, icnt_ref,
                wz_ref, wsc_ref, wgate_ref, dng_ref, wpool_ref, pscale_ref, scw_ref,
                wa_ref, wb_ref, wc_ref, wo_ref, out_ref, *, width):
    i = pl.program_id(1)
    last = pl.num_programs(1) - 1
    tm, d = x_ref.shape[1], x_ref.shape[2]
    h_ext = _halo_modulate(x_ref, xp_ref, xn_ref, g_ref[...], sh_ref[0], sc_ref[0])
    keep_prev, keep_next = jnp.where(i > 0, 1.0, 0.0), jnp.where(i < last, 1.0, 0.0)
    pool_ext = jnp.concatenate([pinp_ref[0] * keep_prev, pin_ref[0], pinn_ref[0] * keep_next], axis=0)
    pd = _pool_tile(pool_ext, icnt_ref[...], width)
    psc = _dot(h_ext, wsc_ref[...])
    conv = _conv3_ext(psc[:, 2 * SC_WIDTH:] * psc[:, :SC_WIDTH], scw_ref, keep_prev, keep_next)
    yc_in = psc[SUBLANES:SUBLANES + tm, SC_WIDTH:2 * SC_WIDTH] * conv
    sub = tm // MIX_SUBTILES
    for t in range(MIX_SUBTILES):
        rows = slice(t * sub, (t + 1) * sub)
        hb = h_ext[SUBLANES + t * sub:SUBLANES + (t + 1) * sub].astype(BF16)
        z = _dot(hb, wz_ref[...])
        o = of_ref[0, rows, :].astype(F32) + ob_ref[0, rows, :].astype(F32)
        parts = []
        for h in range(DN_HEADS):
            lo = h * DN_HEAD_DIM
            parts.append(_rms(o[:, lo:lo + DN_HEAD_DIM]) * dng_ref[...] * _silu(z[:, lo:lo + DN_HEAD_DIM]))
        y_a = _dot(jnp.concatenate(parts, axis=1), wa_ref[...])
        y_b = _dot(_dot(pd[rows], wpool_ref[...]) * pscale_ref[...], wb_ref[...])
        y_c = _dot(yc_in[rows], wc_ref[...])
        gates = _sigmoid(_dot(hb, wgate_ref[...]))
        y = gates[:, :d] * y_a + gates[:, d:2 * d] * y_b + gates[:, 2 * d:] * y_c
        out_ref[0, rows, :] = x_ref[0, rows, :] + gt_ref[0] * _dot(y, wo_ref[...])


def _mix_call(x, shift, scale, gate, norm_g, o_f, o_b, pin, inv_cnt, width, consts):
    b, s, d = x.shape
    tm = min(TOKEN_TILE, s)
    nt = s // tm
    r8 = tm // SUBLANES
    nb8 = s // SUBLANES
    halo = SUBLANES * width
    assert tm % halo == 0 and s % tm == 0
    hb, nh = tm // halo, s // halo
    tok = lambda n: pl.BlockSpec((1, tm, n), lambda bi, i: (bi, i, 0))
    vec = pl.BlockSpec((1, 1, d), lambda bi, i: (bi, 0, 0))
    return pl.pallas_call(
        functools.partial(_mix_kernel, width=width),
        grid=(b, nt),
        in_specs=[tok(d),
                  pl.BlockSpec((1, SUBLANES, d), lambda bi, i: (bi, jnp.maximum(i * r8 - 1, 0), 0)),
                  pl.BlockSpec((1, SUBLANES, d), lambda bi, i: (bi, jnp.minimum((i + 1) * r8, nb8 - 1), 0)),
                  vec, vec, vec, _const_spec((1, d)),
                  tok(DN_WIDTH), tok(DN_WIDTH), tok(POOL_WIDTH),
                  pl.BlockSpec((1, halo, POOL_WIDTH), lambda bi, i: (bi, jnp.maximum(i * hb - 1, 0), 0)),
                  pl.BlockSpec((1, halo, POOL_WIDTH), lambda bi, i: (bi, jnp.minimum((i + 1) * hb, nh - 1), 0)),
                  pl.BlockSpec((tm, POOL_WIDTH), lambda bi, i: (i, 0))]
        + [_const_spec(w.shape) for w in consts],
        out_specs=tok(d),
        out_shape=jax.ShapeDtypeStruct((b, s, d), F32),
        compiler_params=_params(("parallel", "parallel")),
        name="mix",
    )(x, x, x, shift, scale, gate, norm_g, o_f, o_b, pin, pin, pin, inv_cnt, *consts)


def _ffn_kernel(x_ref, sh_ref, sc_ref, gt_ref, g_ref, wgu_ref, wdown_ref, gf_ref, out_ref, *, final):
    dff = wdown_ref.shape[0]
    tm = x_ref.shape[1]
    sub = tm // FFN_SUBTILES
    for t in range(FFN_SUBTILES):
        rows = slice(t * sub, (t + 1) * sub)
        xm = x_ref[0, rows, :]
        hb = _modulate(xm, g_ref[...], sh_ref[0], sc_ref[0]).astype(BF16)
        gu = _dot(hb, wgu_ref[...])
        act = _silu(gu[:, :dff]) * gu[:, dff:]
        r = xm + gt_ref[0] * _dot(act, wdown_ref[...])
        if final:
            r = _rms(r) * gf_ref[...]
        out_ref[0, rows, :] = r


def _ffn_call(x, shift, scale, gate, norm_g, w_gu, w_down, final_g, final):
    b, s, d = x.shape
    tm = min(TOKEN_TILE, s)
    tok = pl.BlockSpec((1, tm, d), lambda bi, i: (bi, i, 0))
    vec = pl.BlockSpec((1, 1, d), lambda bi, i: (bi, 0, 0))
    return pl.pallas_call(
        functools.partial(_ffn_kernel, final=final),
        grid=(b, s // tm),
        in_specs=[tok, vec, vec, vec, _const_spec((1, d)), _const_spec(w_gu.shape),
                  _const_spec(w_down.shape), _const_spec((1, d))],
        out_specs=tok,
        out_shape=jax.ShapeDtypeStruct((b, s, d), F32),
        compiler_params=_params(("parallel", "parallel")),
        name="ffn",
    )(x, shift, scale, gate, norm_g, w_gu, w_down, final_g)


def _block_diag(w):
    g, ci, co = w.shape
    out = jnp.zeros((g * ci, g * co), w.dtype)
    for j in range(g):
        out = out.at[j * ci:(j + 1) * ci, j * co:(j + 1) * co].set(w[j])
    return out


def kernel(x, c, ctx, c_ctx, w_ada, b_ada, norm1_g, norm2_g, w_in, dn_conv_w, dn_a_log, dn_dt_bias,
           dn_norm_g, pool_w, pool_scale, sc_conv_w, w_br_a, w_br_b, w_br_c, w_o, w_gu, w_down,
           final_norm_g):
    bn, seq, d = x.shape
    depth = w_ada.shape[0]
    rows = seq // GRID_W
    off_z = 3 * DN_WIDTH
    off_a = off_z + DN_WIDTH
    off_pool = off_a + 4 * DN_HEADS
    off_sc = off_pool + POOL_WIDTH
    off_gate = off_sc + 3 * SC_WIDTH

    n_c = -(-(bn + 1) // SUBLANES) * SUBLANES
    cs = jnp.concatenate([c, c_ctx[None], jnp.zeros((n_c - bn - 1, d), F32)], axis=0)
    mod = _ada_call(cs, w_ada, b_ada)

    s0 = jnp.zeros((bn, 2, DN_HEADS, DN_HEAD_DIM, DN_HEAD_DIM), F32)
    icnt_lat = _pool_inv_counts(rows, GRID_W)
    icnt_ctx = _pool_inv_counts(ctx.shape[1], 1)
    final_g = final_norm_g.reshape(1, d)
    w_in16 = w_in.astype(BF16)
    for l in range(depth):
        wl = w_in16[l]
        w_qkv = wl[:, :off_z]
        w_z = wl[:, off_z:off_a]
        w_ab = jnp.pad(wl[:, off_a:off_pool], ((0, 0), (0, LANES - 4 * DN_HEADS)))
        w_pool = wl[:, off_pool:off_sc]
        w_sc = wl[:, off_sc:off_gate]
        w_gate = wl[:, off_gate:]
        alog = jnp.pad(dn_a_log[l].reshape(1, -1), ((0, 0), (0, LANES - 2 * DN_HEADS)))
        dtb = jnp.pad(dn_dt_bias[l].reshape(1, -1), ((0, 0), (0, LANES - 2 * DN_HEADS)))
        n1 = norm1_g[l].reshape(1, d)
        n2 = norm2_g[l].reshape(1, d)
        mix_consts = (w_z, w_sc, w_gate, dn_norm_g[l].reshape(1, -1),
                      _block_diag(pool_w[l]).astype(BF16), pool_scale[l].reshape(1, -1), sc_conv_w[l],
                      w_br_a[l].astype(BF16), w_br_b[l].astype(BF16), w_br_c[l].astype(BF16),
                      w_o[l].astype(BF16))
        wgu = w_gu[l].astype(BF16)
        wdn = w_down[l].astype(BF16)
        lat = [mod[l, :bn, j * d:(j + 1) * d][:, None, :] for j in range(6)]
        cxm = [jnp.broadcast_to(mod[l, bn:bn + 1, j * d:(j + 1) * d][None], (bn, 1, d)) for j in range(6)]

        qkv, gb, pin = _proj_call(ctx, cxm[0], cxm[1], n1, w_qkv, w_ab, w_pool, dn_conv_w[l], alog, dtb)
        o_f, o_b, s_ctx = _scan_call(qkv, gb, s0)
        if l < depth - 1:
            ctx = _mix_call(ctx, cxm[0], cxm[1], cxm[2], n1, o_f, o_b, pin, icnt_ctx, 1, mix_consts)
            ctx = _ffn_call(ctx, cxm[3], cxm[4], cxm[5], n2, wgu, wdn, final_g, False)

        qkv, gb, pin = _proj_call(x, lat[0], lat[1], n1, w_qkv, w_ab, w_pool, dn_conv_w[l], alog, dtb)
        o_f, o_b, _ = _scan_call(qkv, gb, s_ctx)
        x = _mix_call(x, lat[0], lat[1], lat[2], n1, o_f, o_b, pin, icnt_lat, GRID_W, mix_consts)
        x = _ffn_call(x, lat[3], lat[4], lat[5], n2, wgu, wdn, final_g, l == depth - 1)
    return x
```

```python
import functools
import math

import jax
import jax.numpy as jnp
import numpy as np
from jax import lax
from jax.experimental import pallas as pl
from jax.experimental.pallas import tpu as pltpu

F32 = jnp.float32
BF16 = jnp.bfloat16

EPS = 1e-6
GRID_W = 64
DN_HEADS = 4
DN_HEAD_DIM = 128
DN_WIDTH = DN_HEADS * DN_HEAD_DIM
POOL_WINDOWS = (2, 4, 8, 16)
POOL_GROUP_DIM = 64
POOL_WIDTH = POOL_GROUP_DIM * len(POOL_WINDOWS)
SC_WIDTH = 256
N_BRANCH = 3

LANES = 128
SUBLANES = 8
BF16_ROWS = 16
SCAN_CHUNK = 128
SCAN_BLOCK = 128
TOKEN_TILE = 512
MIX_TILE = 1024
FFN_SUBTILES = 2
MIX_SUBTILES = 4
VMEM_LIMIT = 56 * 1024 * 1024
NEG_BIG = -1e30


def _dot(a, b):
    return jnp.dot(a.astype(BF16), b.astype(BF16), preferred_element_type=F32)


def _dot_nt(a, b):
    return lax.dot_general(a.astype(BF16), b.astype(BF16), (((1,), (1,)), ((), ())),
                           preferred_element_type=F32)


def _dot_tn(a, b):
    return lax.dot_general(a.astype(BF16), b.astype(BF16), (((0,), (0,)), ((), ())),
                           preferred_element_type=F32)


def _sigmoid(x):
    return 0.5 + 0.5 * jnp.tanh(0.5 * x)


def _silu_of_half(hx):
    return hx + hx * jnp.tanh(hx)


def _silu(x):
    return _silu_of_half(0.5 * x)


def _rms(x):
    return x * lax.rsqrt(jnp.mean(x * x, axis=-1, keepdims=True) + EPS)


def _modulate(x, g, shift, scale):
    return _rms(x) * (g * (1.0 + scale)) + shift


def _halo_modulate(x_ref, xp_ref, xn_ref, g, shift, scale):
    x_ext = jnp.concatenate([xp_ref[0], x_ref[0], xn_ref[0]], axis=0)
    return _modulate(x_ext, g, shift, scale)


def _conv3_ext(p_ext, w_ref, keep_prev, keep_next):
    n = p_ext.shape[0] - 2 * SUBLANES
    pe = jnp.concatenate([p_ext[:SUBLANES] * keep_prev, p_ext[SUBLANES:SUBLANES + n],
                          p_ext[SUBLANES + n:] * keep_next], axis=0)
    return (pe[SUBLANES - 1:SUBLANES - 1 + n] * w_ref[0:1] + pe[SUBLANES:SUBLANES + n] * w_ref[1:2]
            + pe[SUBLANES + 1:SUBLANES + 1 + n] * w_ref[2:3])


def _const_spec(shape):
    nd = len(shape)
    return pl.BlockSpec(shape, lambda *_: (0,) * nd, pipeline_mode=pl.Buffered(1))


def _params(sem):
    return pltpu.CompilerParams(dimension_semantics=sem, vmem_limit_bytes=VMEM_LIMIT)


def _ada_kernel(c_ref, w_ref, b_ref, o_ref):
    a = _silu(c_ref[...])
    w = w_ref[0]
    a_hi, w_hi = a.astype(BF16), w.astype(BF16)
    a_lo = (a - a_hi.astype(F32)).astype(BF16)
    w_lo = (w - w_hi.astype(F32)).astype(BF16)
    mm = functools.partial(jnp.dot, preferred_element_type=F32)
    o_ref[0] = (mm(a_hi, w_hi) + (mm(a_lo, w_hi) + mm(a_hi, w_lo))) + b_ref[0]


def _ada_call(cs, w_ada, b_ada):
    nl, d, n6 = w_ada.shape
    rows = cs.shape[0]
    tn = n6 // 4
    return pl.pallas_call(
        _ada_kernel,
        grid=(nl, n6 // tn),
        in_specs=[pl.BlockSpec((rows, d), lambda l, j: (0, 0)),
                  pl.BlockSpec((1, d, tn), lambda l, j: (l, 0, j)),
                  pl.BlockSpec((1, 1, tn), lambda l, j: (l, 0, j))],
        out_specs=pl.BlockSpec((1, rows, tn), lambda l, j: (l, 0, j)),
        out_shape=jax.ShapeDtypeStruct((nl, rows, n6), F32),
        compiler_params=_params(("parallel", "parallel")),
        name="ada",
    )(cs, w_ada, b_ada.reshape(nl, 1, n6))


def _proj_kernel(x_ref, xp_ref, xn_ref, sh_ref, sc_ref, g_ref, wqkv_ref, wab_ref, wpool_ref,
                 cw_ref, alog_ref, dtb_ref, qkv_ref, gb_ref, pin_ref, pbuf_ref, abuf_ref):
    i = pl.program_id(1)
    last = pl.num_programs(1) - 1
    tm = x_ref.shape[1]
    m = tm // SUBLANES
    h_ext = _halo_modulate(x_ref, xp_ref, xn_ref, g_ref[...], sh_ref[0], sc_ref[0])
    hb_ext = h_ext.astype(BF16)
    ncol = 3 * DN_WIDTH // LANES
    keep_prev, keep_next = jnp.where(i > 0, 1.0, 0.0), jnp.where(i < last, 1.0, 0.0)
    for j in range(ncol):
        cols = slice(j * LANES, (j + 1) * LANES)
        if j % 2 == 0:
            p2 = _mm(hb_ext, wqkv_ref[:, j * LANES:(j + 2) * LANES])
            pbuf_ref[j] = p2[:, :LANES]
            pbuf_ref[j + 1] = p2[:, LANES:]
        w0, w1, w2 = 0.5 * cw_ref[0:1, cols], 0.5 * cw_ref[1:2, cols], 0.5 * cw_ref[2:3, cols]
        blk = [pbuf_ref[j, pl.ds(SUBLANES + r, m, stride=SUBLANES), :] for r in range(SUBLANES)]
        before = pbuf_ref[j, SUBLANES - 1:SUBLANES, :] * keep_prev
        after = pbuf_ref[j, SUBLANES + tm:SUBLANES + tm + 1, :] * keep_next
        prev0 = jnp.concatenate([before, blk[SUBLANES - 1][:m - 1]], axis=0)
        next7 = jnp.concatenate([blk[0][1:], after], axis=0)
        for r in range(SUBLANES):
            prev = blk[r - 1] if r else prev0
            nxt = blk[r + 1] if r < SUBLANES - 1 else next7
            a = _silu_of_half(prev * w0 + blk[r] * w1 + nxt * w2)
            if j < 2 * DN_HEADS:
                a = a * lax.rsqrt(jnp.sum(a * a, axis=-1, keepdims=True) + EPS)
            if j < DN_HEADS:
                a = a * (DN_HEAD_DIM ** -0.5)
            abuf_ref[j, pl.ds(r, m, stride=SUBLANES), :] = a
    for j in range(ncol):
        qkv_ref[0, :, j * LANES:(j + 1) * LANES] = abuf_ref[j].astype(BF16)
    pab = _mm(hb_ext, wab_ref[...])[SUBLANES:SUBLANES + tm]
    z = pab + dtb_ref[...]
    softplus = jnp.maximum(z, 0.0) + jnp.log(1.0 + jnp.exp(-jnp.abs(z)))
    gdec = -jnp.exp(alog_ref[...]) * softplus
    lane = lax.broadcasted_iota(jnp.int32, pab.shape, 1)
    gb_ref[0] = jnp.where(lane < 2 * DN_HEADS, gdec,
                          jnp.where(lane < 4 * DN_HEADS, _sigmoid(pab), 0.0))
    pin_ref[0] = _mm(hb_ext, wpool_ref[...])[SUBLANES:SUBLANES + tm]


def _proj_call(x, shift, scale, norm_g, w_qkv, w_ab, w_pool, conv_w, alog, dtb):
    b, s, d = x.shape
    tm = min(TOKEN_TILE, s)
    nt = s // tm
    r8 = tm // SUBLANES
    nb8 = s // SUBLANES
    tok = lambda n: pl.BlockSpec((1, tm, n), lambda bi, i: (bi, i, 0))
    vec = pl.BlockSpec((1, 1, d), lambda bi, i: (bi, 0, 0))
    return pl.pallas_call(
        _proj_kernel,
        grid=(b, nt),
        in_specs=[tok(d),
                  pl.BlockSpec((1, SUBLANES, d), lambda bi, i: (bi, jnp.maximum(i * r8 - 1, 0), 0)),
                  pl.BlockSpec((1, SUBLANES, d), lambda bi, i: (bi, jnp.minimum((i + 1) * r8, nb8 - 1), 0)),
                  vec, vec, _const_spec((1, d)),
                  _const_spec(w_qkv.shape), _const_spec(w_ab.shape), _const_spec(w_pool.shape),
                  _const_spec(conv_w.shape), _const_spec(alog.shape), _const_spec(dtb.shape)],
        out_specs=[tok(3 * DN_WIDTH), tok(LANES), tok(POOL_WIDTH)],
        out_shape=[jax.ShapeDtypeStruct((b, s, 3 * DN_WIDTH), BF16),
                   jax.ShapeDtypeStruct((b, s, LANES), F32), jax.ShapeDtypeStruct((b, s, POOL_WIDTH), F32)],
        scratch_shapes=[pltpu.VMEM((3 * DN_WIDTH // LANES, tm + 2 * SUBLANES, LANES), F32),
                        pltpu.VMEM((3 * DN_WIDTH // LANES, tm, LANES), F32)],
        compiler_params=_params(("parallel", "parallel")),
        name="proj_in",
    )(x, x, x, shift, scale, norm_g, w_qkv, w_ab, w_pool, conv_w, alog, dtb)


def _prefix_sum_rows(x):
    n = x.shape[0]
    r = lax.broadcasted_iota(jnp.int32, x.shape, 0)
    s = 1
    while s < n:
        x = x + jnp.where(r >= s, pltpu.roll(x, s, axis=0), 0.0)
        s *= 2
    return x


def _mm(a, b):
    return jnp.dot(a, b, preferred_element_type=F32)


def _scan_kernel(qkvf_ref, gf_ref, qkvb_ref, gbk_ref, s0_ref, of_ref, ob_ref, s_ref, msk_ref, tri_ref, *, c):
    nlev = msk_ref.shape[0] - 1
    i = pl.program_id(1)

    @pl.when(i == 0)
    def _():
        s_ref[...] = s0_ref[...]
        row = lax.broadcasted_iota(jnp.int32, (c, c), 0)
        col = lax.broadcasted_iota(jnp.int32, (c, c), 1)

        def same(sh):
            return jnp.right_shift(row, sh) == jnp.right_shift(col, sh)

        one = jnp.ones((c, c), F32)
        zero = jnp.zeros((c, c), F32)
        msk_ref[0] = jnp.where(same(1), jnp.where(row == col, zero, one), zero).astype(BF16)
        for j in range(1, nlev):
            msk_ref[j] = jnp.where(same(j + 1), jnp.where(same(j), zero, one), zero).astype(BF16)
        msk_ref[nlev] = jnp.where(row == col, one, zero).astype(BF16)
        tri_ref[0] = jnp.where(row >= col, 0.0, NEG_BIG)
        tri_ref[1] = jnp.where(row <= col, 0.0, NEG_BIG)

    nbat, nblk = qkvf_ref.shape[0], qkvf_ref.shape[1] // c
    units = []
    for d, (qkv_ref, gb_ref, o_ref) in enumerate(((qkvf_ref, gf_ref, of_ref), (qkvb_ref, gbk_ref, ob_ref))):
        for bx, g in [(bx, g) for bx in range(nbat) for g in range(nblk)]:
            rs = slice(g * c, (g + 1) * c)
            gb = gb_ref[bx, rs, :]
            cs = _prefix_sum_rows(gb)
            tot = cs[c - 1:c, :]
            if d == 1:
                cs = tot - cs + gb
            cs_t = cs.T
            for h in range(DN_HEADS):
                gi = d * DN_HEADS + h
                bi = 2 * DN_HEADS + gi
                lo = h * DN_HEAD_DIM
                hs = slice(lo, lo + DN_HEAD_DIM)
                q16 = qkv_ref[bx, rs, hs]
                k16 = qkv_ref[bx, rs, DN_WIDTH + lo:DN_WIDTH + lo + DN_HEAD_DIM]
                v16 = qkv_ref[bx, rs, 2 * DN_WIDTH + lo:2 * DN_WIDTH + lo + DN_HEAD_DIM]
                units.append(dict(bx=bx, d=d, h=h, g=g, rs=rs, hs=hs, o_ref=o_ref, gc=cs[:, gi:gi + 1],
                                  gc_row=cs_t[gi:gi + 1, :], gl=tot[:, gi:gi + 1], beta=gb[:, bi:bi + 1],
                                  q16=q16, k16=k16, q=q16.astype(F32), k=k16.astype(F32),
                                  v=v16.astype(F32)))
    nu = range(len(units))
    dec = [jnp.exp((u["gc"] - u["gc_row"]) + tri_ref[u["d"]]) for u in units]
    egc = [jnp.exp(u["gc"]) for u in units]
    k16 = [u["k16"] for u in units]
    kbeta = [u["k"] * u["beta"] for u in units]
    kq = [_dot_nt(jnp.concatenate([kbeta[j].astype(BF16), units[j]["q16"]], axis=0), k16[j])
          for j in nu]
    a16 = [(kq[j][:c] * dec[j]).astype(BF16) for j in nu]
    aqk16 = [(kq[j][c:] * dec[j]).astype(BF16) for j in nu]
    eye16 = msk_ref[nlev]
    t16 = [eye16 - a16[j] * msk_ref[0] for j in nu]
    for lev in range(1, nlev):
        b = 2 ** lev
        if b < BF16_ROWS:
            p16 = [_mm(t16[j], a16[j] * msk_ref[lev]).astype(BF16) for j in nu]
            t16 = [t16[j] - _mm(p16[j], t16[j]).astype(BF16) for j in nu]
            continue
        upd = [[r0 for r0 in range(0, c, b) if (r0 // b) % 2 == 1 - units[j]["d"]] for j in nu]
        t_sel = [jnp.concatenate([t16[j][r0:r0 + b] for r0 in upd[j]], axis=0) for j in nu]
        p16 = [_mm(t_sel[j], a16[j] * msk_ref[lev]).astype(BF16) for j in nu]
        t_new = [t_sel[j] - _mm(p16[j], t16[j]).astype(BF16) for j in nu]
        t16 = [jnp.concatenate([t_new[j][upd[j].index(r0) * b:(upd[j].index(r0) + 1) * b] if r0 in upd[j]
                                else t16[j][r0:r0 + b] for r0 in range(0, c, b)], axis=0) for j in nu]
    e16 = [t16[j] - eye16 for j in nu]
    vbeta = [units[j]["v"] * units[j]["beta"] for j in nu]
    kd16 = [(kbeta[j] * egc[j]).astype(BF16) for j in nu]
    qd16 = [(units[j]["q"] * egc[j]).astype(BF16) for j in nu]
    kst16 = [(units[j]["k"] * jnp.exp(units[j]["gl"] - units[j]["gc"])).astype(BF16) for j in nu]
    state = {(bx, d, h): s_ref[bx, d, h] for bx in range(nbat) for d in range(2) for h in range(DN_HEADS)}
    for step in range(nblk):
        cur = [j for j in nu if units[j]["g"] == (step if units[j]["d"] == 0 else nblk - 1 - step)]
        s16 = {j: state[units[j]["bx"], units[j]["d"], units[j]["h"]].astype(BF16) for j in cur}
        ws = {j: _mm(jnp.concatenate([kd16[j], qd16[j]], axis=0), s16[j]) for j in cur}
        resid = {j: vbeta[j] - ws[j][:c] for j in cur}
        vn16 = {j: (resid[j] + _mm(e16[j], resid[j].astype(BF16))).astype(BF16) for j in cur}
        for j in cur:
            u = units[j]
            u["o_ref"][u["bx"], u["rs"], u["hs"]] = (ws[j][c:] + _mm(aqk16[j], vn16[j])).astype(BF16)
        for j in cur:
            u = units[j]
            key = (u["bx"], u["d"], u["h"])
            state[key] = (state[key] * jnp.exp(u["gl"])
                          + lax.dot_general(kst16[j], vn16[j], (((0,), (0,)), ((), ())),
                                            preferred_element_type=F32))
    for (bx, d, h), val in state.items():
        s_ref[bx, d, h] = val


def _scan_call(qkv, gb, s0):
    b, s, _ = qkv.shape
    c = min(SCAN_CHUNK, s)
    blk = min(SCAN_BLOCK, s)
    nb = s // blk
    nlev = int(math.log2(c))
    fwd = lambda n: pl.BlockSpec((b, blk, n), lambda bi, i: (bi, i, 0))
    bwd = lambda n: pl.BlockSpec((b, blk, n), lambda bi, i: (bi, nb - 1 - i, 0))
    st = pl.BlockSpec((b, 2, DN_HEADS, DN_HEAD_DIM, DN_HEAD_DIM), lambda bi, i: (bi, 0, 0, 0, 0))
    return pl.pallas_call(
        functools.partial(_scan_kernel, c=c),
        grid=(1, nb),
        in_specs=[fwd(3 * DN_WIDTH), fwd(LANES), bwd(3 * DN_WIDTH), bwd(LANES), st],
        out_specs=[fwd(DN_WIDTH), bwd(DN_WIDTH), st],
        out_shape=[jax.ShapeDtypeStruct((b, s, DN_WIDTH), BF16)] * 2
        + [jax.ShapeDtypeStruct(s0.shape, F32)],
        scratch_shapes=[pltpu.VMEM((nlev + 1, c, c), BF16), pltpu.VMEM((2, c, c), F32)],
        compiler_params=_params(("parallel", "arbitrary")),
        name="scan",
    )(qkv, gb, qkv, gb, s0)


def _window_offsets(w):
    return -(w // 2), w - 1 - (w // 2)


def _pool_inv_counts(rows, width):
    t = np.arange(rows * width, dtype=np.int64)
    r, c = t // width, t % width
    cols = []
    for w in POOL_WINDOWS:
        lo, hi = _window_offsets(w)
        cnt_r = np.minimum(r + hi + 1, rows) - np.maximum(r + lo, 0)
        cnt_c = np.minimum(c + hi + 1, width) - np.maximum(c + lo, 0)
        inv = (1.0 / (cnt_r * cnt_c)).astype(np.float32)
        cols.append(np.broadcast_to(inv[:, None], (rows * width, POOL_GROUP_DIM)))
    return jnp.asarray(np.concatenate(cols, axis=1))


def _window_sums(x, unit, lev_small):
    n = x.shape[0]

    def shifted(a, k):
        return pltpu.roll(a, (k * unit) % n, axis=0)

    sums = {1: x + shifted(x, 1)}
    for lev in range(2, lev_small + 2):
        step = 2 ** (lev - 2)
        sums[lev] = shifted(sums[lev - 1], step) + shifted(sums[lev - 1], -step)
    return sums[lev_small], sums[lev_small + 1]


def _pool_tile(ext, inv_cnt, width):
    halo = SUBLANES * width
    tm = ext.shape[0] - 2 * halo
    pad = SUBLANES
    outs = []
    for slab in range(POOL_WIDTH // LANES):
        lanes = slice(slab * LANES, (slab + 1) * LANES)
        e = ext[:, lanes]
        small_lane = lax.broadcasted_iota(jnp.int32, (tm, LANES), 1) < POOL_GROUP_DIM
        levels = int(math.log2(POOL_WINDOWS[2 * slab]))
        rs, rl = _window_sums(e, width, levels)
        m1 = jnp.where(small_lane, rs[halo:halo + tm], rl[halo:halo + tm])
        if width > 1:
            zeros = jnp.zeros((pad, LANES), F32)
            pieces = []
            for r in range(tm // width):
                pieces += [zeros, m1[r * width:(r + 1) * width], zeros]
            cs, cl = _window_sums(jnp.concatenate(pieces, axis=0), 1, levels)
            stride = width + 2 * pad
            pick = lambda a: jnp.concatenate(
                [a[r * stride + pad:r * stride + pad + width] for r in range(tm // width)], axis=0)
            m1 = jnp.where(small_lane, pick(cs), pick(cl))
        outs.append(m1 * inv_cnt[:, lanes] - e[halo:halo + tm])
    return jnp.concatenate(outs, axis=1)


def _mix_kernel(x_ref, xp_ref, xn_ref, sh_ref, sc_ref, gt_ref, g_ref, of_ref, ob_ref,
                pin_ref, pinp_ref, pinn_ref, icnt_ref,
                wz_ref, wsc_ref, wgate_ref, dng_ref, wpool_ref, pscale_ref, scw_ref,
                wa_ref, wb_ref, wc_ref, wo_ref, out_ref, *, width):
    i = pl.program_id(1)
    last = pl.num_programs(1) - 1
    tm, d = x_ref.shape[1], x_ref.shape[2]
    h_ext = _halo_modulate(x_ref, xp_ref, xn_ref, g_ref[...], sh_ref[0], sc_ref[0])
    keep_prev, keep_next = jnp.where(i > 0, 1.0, 0.0), jnp.where(i < last, 1.0, 0.0)
    pool_ext = jnp.concatenate([pinp_ref[0] * keep_prev, pin_ref[0], pinn_ref[0] * keep_next], axis=0)
    pd = _pool_tile(pool_ext, icnt_ref[...], width)
    psc = _dot(h_ext, wsc_ref[...])
    conv = _conv3_ext(psc[:, 2 * SC_WIDTH:] * psc[:, :SC_WIDTH], scw_ref, keep_prev, keep_next)
    yc_in = psc[SUBLANES:SUBLANES + tm, SC_WIDTH:2 * SC_WIDTH] * conv
    sub = tm // MIX_SUBTILES
    for t in range(MIX_SUBTILES):
        rows = slice(t * sub, (t + 1) * sub)
        hb = h_ext[SUBLANES + t * sub:SUBLANES + (t + 1) * sub].astype(BF16)
        z = _dot(hb, wz_ref[...])
        o = of_ref[0, rows, :].astype(F32) + ob_ref[0, rows, :].astype(F32)
        parts = []
        for h in range(DN_HEADS):
            lo = h * DN_HEAD_DIM
            parts.append(_rms(o[:, lo:lo + DN_HEAD_DIM]) * dng_ref[...] * _silu(z[:, lo:lo + DN_HEAD_DIM]))
        y_a = _dot(jnp.concatenate(parts, axis=1), wa_ref[...])
        y_b = _dot(_dot(pd[rows], wpool_ref[...]) * pscale_ref[...], wb_ref[...])
        y_c = _dot(yc_in[rows], wc_ref[...])
        gates = _sigmoid(_dot(hb, wgate_ref[...]))
        y = gates[:, :d] * y_a + gates[:, d:2 * d] * y_b + gates[:, 2 * d:] * y_c
        out_ref[0, rows, :] = x_ref[0, rows, :] + gt_ref[0] * _dot(y, wo_ref[...])


def _mix_call(x, shift, scale, gate, norm_g, o_f, o_b, pin, inv_cnt, width, consts):
    b, s, d = x.shape
    tm = min(MIX_TILE, s)
    nt = s // tm
    r8 = tm // SUBLANES
    nb8 = s // SUBLANES
    halo = SUBLANES * width
    assert tm % halo == 0 and s % tm == 0
    hb, nh = tm // halo, s // halo
    tok = lambda n: pl.BlockSpec((1, tm, n), lambda bi, i: (bi, i, 0))
    vec = pl.BlockSpec((1, 1, d), lambda bi, i: (bi, 0, 0))
    return pl.pallas_call(
        functools.partial(_mix_kernel, width=width),
        grid=(b, nt),
        in_specs=[tok(d),
                  pl.BlockSpec((1, SUBLANES, d), lambda bi, i: (bi, jnp.maximum(i * r8 - 1, 0), 0)),
                  pl.BlockSpec((1, SUBLANES, d), lambda bi, i: (bi, jnp.minimum((i + 1) * r8, nb8 - 1), 0)),
                  vec, vec, vec, _const_spec((1, d)),
                  tok(DN_WIDTH), tok(DN_WIDTH), tok(POOL_WIDTH),
                  pl.BlockSpec((1, halo, POOL_WIDTH), lambda bi, i: (bi, jnp.maximum(i * hb - 1, 0), 0)),
                  pl.BlockSpec((1, halo, POOL_WIDTH), lambda bi, i: (bi, jnp.minimum((i + 1) * hb, nh - 1), 0)),
                  pl.BlockSpec((tm, POOL_WIDTH), lambda bi, i: (i, 0))]
        + [_const_spec(w.shape) for w in consts],
        out_specs=tok(d),
        out_shape=jax.ShapeDtypeStruct((b, s, d), F32),
        compiler_params=_params(("parallel", "parallel")),
        name="mix",
    )(x, x, x, shift, scale, gate, norm_g, o_f, o_b, pin, pin, pin, inv_cnt, *consts)


def _ffn_kernel(x_ref, sh_ref, sc_ref, gt_ref, g_ref, wgu_ref, wdown_ref, gf_ref, out_ref, *, final):
    dff = wdown_ref.shape[0]
    tm = x_ref.shape[1]
    sub = tm // FFN_SUBTILES
    for t in range(FFN_SUBTILES):
        rows = slice(t * sub, (t + 1) * sub)
        xm = x_ref[0, rows, :]
        hb = _modulate(xm, g_ref[...], sh_ref[0], sc_ref[0]).astype(BF16)
        gu = _dot(hb, wgu_ref[...])
        act = _silu(gu[:, :dff]) * gu[:, dff:]
        r = xm + gt_ref[0] * _dot(act, wdown_ref[...])
        if final:
            r = _rms(r) * gf_ref[...]
        out_ref[0, rows, :] = r


def _ffn_call(x, shift, scale, gate, norm_g, w_gu, w_down, final_g, final):
    b, s, d = x.shape
    tm = min(TOKEN_TILE, s)
    tok = pl.BlockSpec((1, tm, d), lambda bi, i: (bi, i, 0))
    vec = pl.BlockSpec((1, 1, d), lambda bi, i: (bi, 0, 0))
    return pl.pallas_call(
        functools.partial(_ffn_kernel, final=final),
        grid=(b, s // tm),
        in_specs=[tok, vec, vec, vec, _const_spec((1, d)), _const_spec(w_gu.shape),
                  _const_spec(w_down.shape), _const_spec((1, d))],
        out_specs=tok,
        out_shape=jax.ShapeDtypeStruct((b, s, d), F32),
        compiler_params=_params(("parallel", "parallel")),
        name="ffn",
    )(x, shift, scale, gate, norm_g, w_gu, w_down, final_g)


def _block_diag(w):
    g, ci, co = w.shape
    out = jnp.zeros((g * ci, g * co), w.dtype)
    for j in range(g):
        out = out.at[j * ci:(j + 1) * ci, j * co:(j + 1) * co].set(w[j])
    return out


def kernel(x, c, ctx, c_ctx, w_ada, b_ada, norm1_g, norm2_g, w_in, dn_conv_w, dn_a_log, dn_dt_bias,
           dn_norm_g, pool_w, pool_scale, sc_conv_w, w_br_a, w_br_b, w_br_c, w_o, w_gu, w_down,
           final_norm_g):
    bn, seq, d = x.shape
    depth = w_ada.shape[0]
    rows = seq // GRID_W
    off_z = 3 * DN_WIDTH
    off_a = off_z + DN_WIDTH
    off_pool = off_a + 4 * DN_HEADS
    off_sc = off_pool + POOL_WIDTH
    off_gate = off_sc + 3 * SC_WIDTH

    n_c = -(-(bn + 1) // SUBLANES) * SUBLANES
    cs = jnp.concatenate([c, c_ctx[None], jnp.zeros((n_c - bn - 1, d), F32)], axis=0)
    mod = _ada_call(cs, w_ada, b_ada)

    s0 = jnp.zeros((bn, 2, DN_HEADS, DN_HEAD_DIM, DN_HEAD_DIM), F32)
    icnt_lat = _pool_inv_counts(rows, GRID_W)
    icnt_ctx = _pool_inv_counts(ctx.shape[1], 1)
    final_g = final_norm_g.reshape(1, d)
    w_in16 = w_in.astype(BF16)
    for l in range(depth):
        wl = w_in16[l]
        w_qkv = wl[:, :off_z]
        w_z = wl[:, off_z:off_a]
        w_ab = jnp.pad(wl[:, off_a:off_pool], ((0, 0), (0, LANES - 4 * DN_HEADS)))
        w_pool = wl[:, off_pool:off_sc]
        w_sc = wl[:, off_sc:off_gate]
        w_gate = wl[:, off_gate:]
        alog = jnp.pad(dn_a_log[l].reshape(1, -1), ((0, 0), (0, LANES - 2 * DN_HEADS)))
        dtb = jnp.pad(dn_dt_bias[l].reshape(1, -1), ((0, 0), (0, LANES - 2 * DN_HEADS)))
        n1 = norm1_g[l].reshape(1, d)
        n2 = norm2_g[l].reshape(1, d)
        mix_consts = (w_z, w_sc, w_gate, dn_norm_g[l].reshape(1, -1),
                      _block_diag(pool_w[l]).astype(BF16), pool_scale[l].reshape(1, -1), sc_conv_w[l],
                      w_br_a[l].astype(BF16), w_br_b[l].astype(BF16), w_br_c[l].astype(BF16),
                      w_o[l].astype(BF16))
        wgu = w_gu[l].astype(BF16)
        wdn = w_down[l].astype(BF16)
        lat = [mod[l, :bn, j * d:(j + 1) * d][:, None, :] for j in range(6)]
        cxm = [jnp.broadcast_to(mod[l, bn:bn + 1, j * d:(j + 1) * d][None], (bn, 1, d)) for j in range(6)]

        qkv, gb, pin = _proj_call(ctx, cxm[0], cxm[1], n1, w_qkv, w_ab, w_pool, dn_conv_w[l], alog, dtb)
        o_f, o_b, s_ctx = _scan_call(qkv, gb, s0)
        if l < depth - 1:
            ctx = _mix_call(ctx, cxm[0], cxm[1], cxm[2], n1, o_f, o_b, pin, icnt_ctx, 1, mix_consts)
            ctx = _ffn_call(ctx, cxm[3], cxm[4], cxm[5], n2, wgu, wdn, final_g, False)

        qkv, gb, pin = _proj_call(x, lat[0], lat[1], n1, w_qkv, w_ab, w_pool, dn_conv_w[l], alog, dtb)
        o_f, o_b, _ = _scan_call(qkv, gb, s_ctx)
        x = _mix_call(x, lat[0], lat[1], lat[2], n1, o_f, o_b, pin, icnt_lat, GRID_W, mix_consts)
        x = _ffn_call(x, lat[3], lat[4], lat[5], n2, wgu, wdn, final_g, l == depth - 1)
    return x
```

```python
import functools
import math

import jax
import jax.numpy as jnp
import numpy as np
from jax import lax
from jax.experimental import pallas as pl
from jax.experimental.pallas import tpu as pltpu

F32 = jnp.float32
BF16 = jnp.bfloat16

EPS = 1e-6
GRID_W = 64
DN_HEADS = 4
DN_HEAD_DIM = 128
DN_WIDTH = DN_HEADS * DN_HEAD_DIM
POOL_WINDOWS = (2, 4, 8, 16)
POOL_GROUP_DIM = 64
POOL_WIDTH = POOL_GROUP_DIM * len(POOL_WINDOWS)
SC_WIDTH = 256
N_BRANCH = 3

LANES = 128
SUBLANES = 8
BF16_ROWS = 16
SCAN_CHUNK = 128
SCAN_BLOCK = 128
TOKEN_TILE = 512
MIX_TILE = 1024
FFN_SUBTILES = 2
MIX_GROUP_ROWS = 256
VMEM_LIMIT = 56 * 1024 * 1024
NEG_BIG = -1e30


def _dot(a, b):
    return jnp.dot(a.astype(BF16), b.astype(BF16), preferred_element_type=F32)


def _dot_nt(a, b):
    return lax.dot_general(a.astype(BF16), b.astype(BF16), (((1,), (1,)), ((), ())),
                           preferred_element_type=F32)


def _dot_tn(a, b):
    return lax.dot_general(a.astype(BF16), b.astype(BF16), (((0,), (0,)), ((), ())),
                           preferred_element_type=F32)


def _sigmoid(x):
    return 0.5 + 0.5 * jnp.tanh(0.5 * x)


def _silu_of_half(hx):
    return hx + hx * jnp.tanh(hx)


def _silu(x):
    return _silu_of_half(0.5 * x)


def _rms(x):
    return x * lax.rsqrt(jnp.mean(x * x, axis=-1, keepdims=True) + EPS)


def _modulate(x, g, shift, scale):
    return _rms(x) * (g * (1.0 + scale)) + shift


def _halo_modulate(x_ref, xp_ref, xn_ref, g, shift, scale):
    x_ext = jnp.concatenate([xp_ref[0], x_ref[0], xn_ref[0]], axis=0)
    return _modulate(x_ext, g, shift, scale)


def _conv3_ext(p_ext, w_ref, keep_prev, keep_next):
    n = p_ext.shape[0] - 2 * SUBLANES
    pe = jnp.concatenate([p_ext[:SUBLANES] * keep_prev, p_ext[SUBLANES:SUBLANES + n],
                          p_ext[SUBLANES + n:] * keep_next], axis=0)
    return (pe[SUBLANES - 1:SUBLANES - 1 + n] * w_ref[0:1] + pe[SUBLANES:SUBLANES + n] * w_ref[1:2]
            + pe[SUBLANES + 1:SUBLANES + 1 + n] * w_ref[2:3])


def _const_spec(shape):
    nd = len(shape)
    return pl.BlockSpec(shape, lambda *_: (0,) * nd, pipeline_mode=pl.Buffered(1))


def _params(sem):
    return pltpu.CompilerParams(dimension_semantics=sem, vmem_limit_bytes=VMEM_LIMIT)


def _ada_kernel(c_ref, w_ref, b_ref, o_ref):
    a = _silu(c_ref[...])
    w = w_ref[0]
    a_hi, w_hi = a.astype(BF16), w.astype(BF16)
    a_lo = (a - a_hi.astype(F32)).astype(BF16)
    w_lo = (w - w_hi.astype(F32)).astype(BF16)
    mm = functools.partial(jnp.dot, preferred_element_type=F32)
    o_ref[0] = (mm(a_hi, w_hi) + (mm(a_lo, w_hi) + mm(a_hi, w_lo))) + b_ref[0]


def _ada_call(cs, w_ada, b_ada):
    nl, d, n6 = w_ada.shape
    rows = cs.shape[0]
    tn = n6 // 4
    return pl.pallas_call(
        _ada_kernel,
        grid=(nl, n6 // tn),
        in_specs=[pl.BlockSpec((rows, d), lambda l, j: (0, 0)),
                  pl.BlockSpec((1, d, tn), lambda l, j: (l, 0, j)),
                  pl.BlockSpec((1, 1, tn), lambda l, j: (l, 0, j))],
        out_specs=pl.BlockSpec((1, rows, tn), lambda l, j: (l, 0, j)),
        out_shape=jax.ShapeDtypeStruct((nl, rows, n6), F32),
        compiler_params=_params(("parallel", "parallel")),
        name="ada",
    )(cs, w_ada, b_ada.reshape(nl, 1, n6))


def _proj_kernel(x_ref, xp_ref, xn_ref, sh_ref, sc_ref, g_ref, wqkv_ref, wab_ref, wpool_ref,
                 cw_ref, alog_ref, dtb_ref, qkv_ref, gb_ref, pin_ref, pbuf_ref, abuf_ref):
    i = pl.program_id(1)
    last = pl.num_programs(1) - 1
    tm = x_ref.shape[1]
    m = tm // SUBLANES
    h_ext = _halo_modulate(x_ref, xp_ref, xn_ref, g_ref[...], sh_ref[0], sc_ref[0])
    hb_ext = h_ext.astype(BF16)
    ncol = 3 * DN_WIDTH // LANES
    keep_prev, keep_next = jnp.where(i > 0, 1.0, 0.0), jnp.where(i < last, 1.0, 0.0)
    for j in range(ncol):
        cols = slice(j * LANES, (j + 1) * LANES)
        if j % 2 == 0:
            p2 = _mm(hb_ext, wqkv_ref[:, j * LANES:(j + 2) * LANES])
            pbuf_ref[j] = p2[:, :LANES]
            pbuf_ref[j + 1] = p2[:, LANES:]
        w0, w1, w2 = 0.5 * cw_ref[0:1, cols], 0.5 * cw_ref[1:2, cols], 0.5 * cw_ref[2:3, cols]
        blk = [pbuf_ref[j, pl.ds(SUBLANES + r, m, stride=SUBLANES), :] for r in range(SUBLANES)]
        before = pbuf_ref[j, SUBLANES - 1:SUBLANES, :] * keep_prev
        after = pbuf_ref[j, SUBLANES + tm:SUBLANES + tm + 1, :] * keep_next
        prev0 = jnp.concatenate([before, blk[SUBLANES - 1][:m - 1]], axis=0)
        next7 = jnp.concatenate([blk[0][1:], after], axis=0)
        for r in range(SUBLANES):
            prev = blk[r - 1] if r else prev0
            nxt = blk[r + 1] if r < SUBLANES - 1 else next7
            a = _silu_of_half(prev * w0 + blk[r] * w1 + nxt * w2)
            if j < 2 * DN_HEADS:
                a = a * lax.rsqrt(jnp.sum(a * a, axis=-1, keepdims=True) + EPS)
            if j < DN_HEADS:
                a = a * (DN_HEAD_DIM ** -0.5)
            abuf_ref[j, pl.ds(r, m, stride=SUBLANES), :] = a
    for j in range(ncol):
        qkv_ref[0, :, j * LANES:(j + 1) * LANES] = abuf_ref[j].astype(BF16)
    pab = _mm(hb_ext, wab_ref[...])[SUBLANES:SUBLANES + tm]
    z = pab + dtb_ref[...]
    softplus = jnp.maximum(z, 0.0) + jnp.log(1.0 + jnp.exp(-jnp.abs(z)))
    gdec = -jnp.exp(alog_ref[...]) * softplus
    lane = lax.broadcasted_iota(jnp.int32, pab.shape, 1)
    gb_ref[0] = jnp.where(lane < 2 * DN_HEADS, gdec,
                          jnp.where(lane < 4 * DN_HEADS, _sigmoid(pab), 0.0))
    pin_ref[0] = _mm(hb_ext, wpool_ref[...])[SUBLANES:SUBLANES + tm]


def _proj_call(x, shift, scale, norm_g, w_qkv, w_ab, w_pool, conv_w, alog, dtb):
    b, s, d = x.shape
    tm = min(TOKEN_TILE, s)
    nt = s // tm
    r8 = tm // SUBLANES
    nb8 = s // SUBLANES
    tok = lambda n: pl.BlockSpec((1, tm, n), lambda bi, i: (bi, i, 0))
    vec = pl.BlockSpec((1, 1, d), lambda bi, i: (bi, 0, 0))
    return pl.pallas_call(
        _proj_kernel,
        grid=(b, nt),
        in_specs=[tok(d),
                  pl.BlockSpec((1, SUBLANES, d), lambda bi, i: (bi, jnp.maximum(i * r8 - 1, 0), 0)),
                  pl.BlockSpec((1, SUBLANES, d), lambda bi, i: (bi, jnp.minimum((i + 1) * r8, nb8 - 1), 0)),
                  vec, vec, _const_spec((1, d)),
                  _const_spec(w_qkv.shape), _const_spec(w_ab.shape), _const_spec(w_pool.shape),
                  _const_spec(conv_w.shape), _const_spec(alog.shape), _const_spec(dtb.shape)],
        out_specs=[tok(3 * DN_WIDTH), tok(LANES), tok(POOL_WIDTH)],
        out_shape=[jax.ShapeDtypeStruct((b, s, 3 * DN_WIDTH), BF16),
                   jax.ShapeDtypeStruct((b, s, LANES), F32), jax.ShapeDtypeStruct((b, s, POOL_WIDTH), F32)],
        scratch_shapes=[pltpu.VMEM((3 * DN_WIDTH // LANES, tm + 2 * SUBLANES, LANES), F32),
                        pltpu.VMEM((3 * DN_WIDTH // LANES, tm, LANES), F32)],
        compiler_params=_params(("parallel", "parallel")),
        name="proj_in",
    )(x, x, x, shift, scale, norm_g, w_qkv, w_ab, w_pool, conv_w, alog, dtb)


def _prefix_sum_rows(x):
    n = x.shape[0]
    r = lax.broadcasted_iota(jnp.int32, x.shape, 0)
    s = 1
    while s < n:
        x = x + jnp.where(r >= s, pltpu.roll(x, s, axis=0), 0.0)
        s *= 2
    return x


def _mm(a, b):
    return jnp.dot(a, b, preferred_element_type=F32)


def _scan_kernel(qkvf_ref, gf_ref, qkvb_ref, gbk_ref, s0_ref, of_ref, ob_ref, s_ref, msk_ref, tri_ref, *, c):
    nlev = msk_ref.shape[0] - 1
    i = pl.program_id(1)

    @pl.when(i == 0)
    def _():
        s_ref[...] = s0_ref[...]
        row = lax.broadcasted_iota(jnp.int32, (c, c), 0)
        col = lax.broadcasted_iota(jnp.int32, (c, c), 1)

        def same(sh):
            return jnp.right_shift(row, sh) == jnp.right_shift(col, sh)

        one = jnp.ones((c, c), F32)
        zero = jnp.zeros((c, c), F32)
        msk_ref[0] = jnp.where(same(1), jnp.where(row == col, zero, one), zero).astype(BF16)
        for j in range(1, nlev):
            msk_ref[j] = jnp.where(same(j + 1), jnp.where(same(j), zero, one), zero).astype(BF16)
        msk_ref[nlev] = jnp.where(row == col, one, zero).astype(BF16)
        tri_ref[0] = jnp.where(row >= col, 0.0, NEG_BIG)
        tri_ref[1] = jnp.where(row <= col, 0.0, NEG_BIG)

    nbat, nblk = qkvf_ref.shape[0], qkvf_ref.shape[1] // c
    units = []
    for d, (qkv_ref, gb_ref, o_ref) in enumerate(((qkvf_ref, gf_ref, of_ref), (qkvb_ref, gbk_ref, ob_ref))):
        for bx, g in [(bx, g) for bx in range(nbat) for g in range(nblk)]:
            rs = slice(g * c, (g + 1) * c)
            gb = gb_ref[bx, rs, :]
            cs = _prefix_sum_rows(gb)
            tot = cs[c - 1:c, :]
            if d == 1:
                cs = tot - cs + gb
            cs_t = cs.T
            for h in range(DN_HEADS):
                gi = d * DN_HEADS + h
                bi = 2 * DN_HEADS + gi
                lo = h * DN_HEAD_DIM
                hs = slice(lo, lo + DN_HEAD_DIM)
                q16 = qkv_ref[bx, rs, hs]
                k16 = qkv_ref[bx, rs, DN_WIDTH + lo:DN_WIDTH + lo + DN_HEAD_DIM]
                v16 = qkv_ref[bx, rs, 2 * DN_WIDTH + lo:2 * DN_WIDTH + lo + DN_HEAD_DIM]
                units.append(dict(bx=bx, d=d, h=h, g=g, rs=rs, hs=hs, o_ref=o_ref, gc=cs[:, gi:gi + 1],
                                  gc_row=cs_t[gi:gi + 1, :], gl=tot[:, gi:gi + 1], beta=gb[:, bi:bi + 1],
                                  q16=q16, k16=k16, q=q16.astype(F32), k=k16.astype(F32),
                                  v=v16.astype(F32)))
    nu = range(len(units))
    dec = [jnp.exp((u["gc"] - u["gc_row"]) + tri_ref[u["d"]]) for u in units]
    egc = [jnp.exp(u["gc"]) for u in units]
    k16 = [u["k16"] for u in units]
    kbeta = [u["k"] * u["beta"] for u in units]
    kq = [_dot_nt(jnp.concatenate([kbeta[j].astype(BF16), units[j]["q16"]], axis=0), k16[j])
          for j in nu]
    a16 = [(kq[j][:c] * dec[j]).astype(BF16) for j in nu]
    aqk16 = [(kq[j][c:] * dec[j]).astype(BF16) for j in nu]
    eye16 = msk_ref[nlev]
    t16 = [eye16 - a16[j] * msk_ref[0] for j in nu]
    for lev in range(1, nlev):
        b = 2 ** lev
        if b < BF16_ROWS:
            p16 = [_mm(t16[j], a16[j] * msk_ref[lev]).astype(BF16) for j in nu]
            t16 = [t16[j] - _mm(p16[j], t16[j]).astype(BF16) for j in nu]
            continue
        upd = [[r0 for r0 in range(0, c, b) if (r0 // b) % 2 == 1 - units[j]["d"]] for j in nu]
        t_sel = [jnp.concatenate([t16[j][r0:r0 + b] for r0 in upd[j]], axis=0) for j in nu]
        p16 = [_mm(t_sel[j], a16[j] * msk_ref[lev]).astype(BF16) for j in nu]
        t_new = [t_sel[j] - _mm(p16[j], t16[j]).astype(BF16) for j in nu]
        t16 = [jnp.concatenate([t_new[j][upd[j].index(r0) * b:(upd[j].index(r0) + 1) * b] if r0 in upd[j]
                                else t16[j][r0:r0 + b] for r0 in range(0, c, b)], axis=0) for j in nu]
    e16 = [t16[j] - eye16 for j in nu]
    vbeta = [units[j]["v"] * units[j]["beta"] for j in nu]
    kd16 = [(kbeta[j] * egc[j]).astype(BF16) for j in nu]
    qd16 = [(units[j]["q"] * egc[j]).astype(BF16) for j in nu]
    kst16 = [(units[j]["k"] * jnp.exp(units[j]["gl"] - units[j]["gc"])).astype(BF16) for j in nu]
    state = {(bx, d, h): s_ref[bx, d, h] for bx in range(nbat) for d in range(2) for h in range(DN_HEADS)}
    for step in range(nblk):
        cur = [j for j in nu if units[j]["g"] == (step if units[j]["d"] == 0 else nblk - 1 - step)]
        s16 = {j: state[units[j]["bx"], units[j]["d"], units[j]["h"]].astype(BF16) for j in cur}
        ws = {j: _mm(jnp.concatenate([kd16[j], qd16[j]], axis=0), s16[j]) for j in cur}
        resid = {j: vbeta[j] - ws[j][:c] for j in cur}
        vn16 = {j: (resid[j] + _mm(e16[j], resid[j].astype(BF16))).astype(BF16) for j in cur}
        for j in cur:
            u = units[j]
            u["o_ref"][u["bx"], u["rs"], u["hs"]] = (ws[j][c:] + _mm(aqk16[j], vn16[j])).astype(BF16)
        for j in cur:
            u = units[j]
            key = (u["bx"], u["d"], u["h"])
            state[key] = (state[key] * jnp.exp(u["gl"])
                          + lax.dot_general(kst16[j], vn16[j], (((0,), (0,)), ((), ())),
                                            preferred_element_type=F32))
    for (bx, d, h), val in state.items():
        s_ref[bx, d, h] = val


def _scan_call(qkv, gb, s0):
    b, s, _ = qkv.shape
    c = min(SCAN_CHUNK, s)
    blk = min(SCAN_BLOCK, s)
    nb = s // blk
    nlev = int(math.log2(c))
    fwd = lambda n: pl.BlockSpec((b, blk, n), lambda bi, i: (bi, i, 0))
    bwd = lambda n: pl.BlockSpec((b, blk, n), lambda bi, i: (bi, nb - 1 - i, 0))
    st = pl.BlockSpec((b, 2, DN_HEADS, DN_HEAD_DIM, DN_HEAD_DIM), lambda bi, i: (bi, 0, 0, 0, 0))
    return pl.pallas_call(
        functools.partial(_scan_kernel, c=c),
        grid=(1, nb),
        in_specs=[fwd(3 * DN_WIDTH), fwd(LANES), bwd(3 * DN_WIDTH), bwd(LANES), st],
        out_specs=[fwd(DN_WIDTH), bwd(DN_WIDTH), st],
        out_shape=[jax.ShapeDtypeStruct((b, s, DN_WIDTH), BF16)] * 2
        + [jax.ShapeDtypeStruct(s0.shape, F32)],
        scratch_shapes=[pltpu.VMEM((nlev + 1, c, c), BF16), pltpu.VMEM((2, c, c), F32)],
        compiler_params=_params(("parallel", "arbitrary")),
        name="scan",
    )(qkv, gb, qkv, gb, s0)


def _window_offsets(w):
    return -(w // 2), w - 1 - (w // 2)


def _pool_inv_counts(rows, width):
    t = np.arange(rows * width, dtype=np.int64)
    r, c = t // width, t % width
    cols = []
    for w in POOL_WINDOWS:
        lo, hi = _window_offsets(w)
        cnt_r = np.minimum(r + hi + 1, rows) - np.maximum(r + lo, 0)
        cnt_c = np.minimum(c + hi + 1, width) - np.maximum(c + lo, 0)
        inv = (1.0 / (cnt_r * cnt_c)).astype(np.float32)
        cols.append(np.broadcast_to(inv[:, None], (rows * width, POOL_GROUP_DIM)))
    return jnp.asarray(np.concatenate(cols, axis=1))


def _window_sums(x, unit, lev_small):
    n = x.shape[0]

    def shifted(a, k):
        return pltpu.roll(a, (k * unit) % n, axis=0)

    sums = {1: x + shifted(x, 1)}
    for lev in range(2, lev_small + 2):
        step = 2 ** (lev - 2)
        sums[lev] = shifted(sums[lev - 1], step) + shifted(sums[lev - 1], -step)
    return sums[lev_small], sums[lev_small + 1]


def _pool_tile(ext, inv_cnt, width):
    halo = SUBLANES * width
    tm = ext.shape[0] - 2 * halo
    pad = SUBLANES
    outs = []
    for slab in range(POOL_WIDTH // LANES):
        lanes = slice(slab * LANES, (slab + 1) * LANES)
        e = ext[:, lanes]
        small_lane = lax.broadcasted_iota(jnp.int32, (tm, LANES), 1) < POOL_GROUP_DIM
        levels = int(math.log2(POOL_WINDOWS[2 * slab]))
        rs, rl = _window_sums(e, width, levels)
        m1 = jnp.where(small_lane, rs[halo:halo + tm], rl[halo:halo + tm])
        if width > 1:
            zeros = jnp.zeros((pad, LANES), F32)
            pieces = []
            for r in range(tm // width):
                pieces += [zeros, m1[r * width:(r + 1) * width], zeros]
            cs, cl = _window_sums(jnp.concatenate(pieces, axis=0), 1, levels)
            stride = width + 2 * pad
            pick = lambda a: jnp.concatenate(
                [a[r * stride + pad:r * stride + pad + width] for r in range(tm // width)], axis=0)
            m1 = jnp.where(small_lane, pick(cs), pick(cl))
        outs.append(m1 * inv_cnt[:, lanes] - e[halo:halo + tm])
    return jnp.concatenate(outs, axis=1)


def _mix_kernel(x_ref, xp_ref, xn_ref, sh_ref, sc_ref, gt_ref, g_ref, of_ref, ob_ref,
                pin_ref, pinp_ref, pinn_ref, icnt_ref,
                wz_ref, wsc_ref, wgate_ref, dng_ref, wpool_ref, pscale_ref, scw_ref,
                wa_ref, wb_ref, wc_ref, wo_ref, out_ref, *, width):
    i = pl.program_id(1)
    last = pl.num_programs(1) - 1
    tm, d = x_ref.shape[1], x_ref.shape[2]
    h_ext = _halo_modulate(x_ref, xp_ref, xn_ref, g_ref[...], sh_ref[0], sc_ref[0])
    keep_prev, keep_next = jnp.where(i > 0, 1.0, 0.0), jnp.where(i < last, 1.0, 0.0)
    pool_ext = jnp.concatenate([pinp_ref[0] * keep_prev, pin_ref[0], pinn_ref[0] * keep_next], axis=0)
    pd = _pool_tile(pool_ext, icnt_ref[...], width)
    psc = _dot(h_ext, wsc_ref[...])
    conv = _conv3_ext(psc[:, 2 * SC_WIDTH:] * psc[:, :SC_WIDTH], scw_ref, keep_prev, keep_next)
    yc_in = psc[SUBLANES:SUBLANES + tm, SC_WIDTH:2 * SC_WIDTH] * conv
    sub = min(MIX_GROUP_ROWS, tm)
    for t in range(tm // sub):
        rows = slice(t * sub, (t + 1) * sub)
        hb = h_ext[SUBLANES + t * sub:SUBLANES + (t + 1) * sub].astype(BF16)
        z = _dot(hb, wz_ref[...])
        o = of_ref[0, rows, :].astype(F32) + ob_ref[0, rows, :].astype(F32)
        parts = []
        for h in range(DN_HEADS):
            lo = h * DN_HEAD_DIM
            parts.append(_rms(o[:, lo:lo + DN_HEAD_DIM]) * dng_ref[...] * _silu(z[:, lo:lo + DN_HEAD_DIM]))
        y_a = _dot(jnp.concatenate(parts, axis=1), wa_ref[...])
        y_b = _dot(_dot(pd[rows], wpool_ref[...]) * pscale_ref[...], wb_ref[...])
        y_c = _dot(yc_in[rows], wc_ref[...])
        gates = _sigmoid(_dot(hb, wgate_ref[...]))
        y = gates[:, :d] * y_a + gates[:, d:2 * d] * y_b + gates[:, 2 * d:] * y_c
        out_ref[0, rows, :] = x_ref[0, rows, :] + gt_ref[0] * _dot(y, wo_ref[...])


def _mix_call(x, shift, scale, gate, norm_g, o_f, o_b, pin, inv_cnt, width, consts):
    b, s, d = x.shape
    tm = min(MIX_TILE, s)
    nt = s // tm
    r8 = tm // SUBLANES
    nb8 = s // SUBLANES
    halo = SUBLANES * width
    assert tm % halo == 0 and s % tm == 0
    hb, nh = tm // halo, s // halo
    tok = lambda n: pl.BlockSpec((1, tm, n), lambda bi, i: (bi, i, 0))
    vec = pl.BlockSpec((1, 1, d), lambda bi, i: (bi, 0, 0))
    return pl.pallas_call(
        functools.partial(_mix_kernel, width=width),
        grid=(b, nt),
        in_specs=[tok(d),
                  pl.BlockSpec((1, SUBLANES, d), lambda bi, i: (bi, jnp.maximum(i * r8 - 1, 0), 0)),
                  pl.BlockSpec((1, SUBLANES, d), lambda bi, i: (bi, jnp.minimum((i + 1) * r8, nb8 - 1), 0)),
                  vec, vec, vec, _const_spec((1, d)),
                  tok(DN_WIDTH), tok(DN_WIDTH), tok(POOL_WIDTH),
                  pl.BlockSpec((1, halo, POOL_WIDTH), lambda bi, i: (bi, jnp.maximum(i * hb - 1, 0), 0)),
                  pl.BlockSpec((1, halo, POOL_WIDTH), lambda bi, i: (bi, jnp.minimum((i + 1) * hb, nh - 1), 0)),
                  pl.BlockSpec((tm, POOL_WIDTH), lambda bi, i: (i, 0))]
        + [_const_spec(w.shape) for w in consts],
        out_specs=tok(d),
        out_shape=jax.ShapeDtypeStruct((b, s, d), F32),
        compiler_params=_params(("parallel", "parallel")),
        name="mix",
    )(x, x, x, shift, scale, gate, norm_g, o_f, o_b, pin, pin, pin, inv_cnt, *consts)


def _ffn_kernel(x_ref, sh_ref, sc_ref, gt_ref, g_ref, wgu_ref, wdown_ref, gf_ref, out_ref, *, final):
    dff = wdown_ref.shape[0]
    tm = x_ref.shape[1]
    sub = tm // FFN_SUBTILES
    for t in range(FFN_SUBTILES):
        rows = slice(t * sub, (t + 1) * sub)
        xm = x_ref[0, rows, :]
        hb = _modulate(xm, g_ref[...], sh_ref[0], sc_ref[0]).astype(BF16)
        gu = _dot(hb, wgu_ref[...])
        act = _silu(gu[:, :dff]) * gu[:, dff:]
        r = xm + gt_ref[0] * _dot(act, wdown_ref[...])
        if final:
            r = _rms(r) * gf_ref[...]
        out_ref[0, rows, :] = r


def _ffn_call(x, shift, scale, gate, norm_g, w_gu, w_down, final_g, final):
    b, s, d = x.shape
    tm = min(TOKEN_TILE, s)
    tok = pl.BlockSpec((1, tm, d), lambda bi, i: (bi, i, 0))
    vec = pl.BlockSpec((1, 1, d), lambda bi, i: (bi, 0, 0))
    return pl.pallas_call(
        functools.partial(_ffn_kernel, final=final),
        grid=(b, s // tm),
        in_specs=[tok, vec, vec, vec, _const_spec((1, d)), _const_spec(w_gu.shape),
                  _const_spec(w_down.shape), _const_spec((1, d))],
        out_specs=tok,
        out_shape=jax.ShapeDtypeStruct((b, s, d), F32),
        compiler_params=_params(("parallel", "parallel")),
        name="ffn",
    )(x, shift, scale, gate, norm_g, w_gu, w_down, final_g)


def _block_diag(w):
    g, ci, co = w.shape
    out = jnp.zeros((g * ci, g * co), w.dtype)
    for j in range(g):
        out = out.at[j * ci:(j + 1) * ci, j * co:(j + 1) * co].set(w[j])
    return out


def kernel(x, c, ctx, c_ctx, w_ada, b_ada, norm1_g, norm2_g, w_in, dn_conv_w, dn_a_log, dn_dt_bias,
           dn_norm_g, pool_w, pool_scale, sc_conv_w, w_br_a, w_br_b, w_br_c, w_o, w_gu, w_down,
           final_norm_g):
    bn, seq, d = x.shape
    depth = w_ada.shape[0]
    rows = seq // GRID_W
    off_z = 3 * DN_WIDTH
    off_a = off_z + DN_WIDTH
    off_pool = off_a + 4 * DN_HEADS
    off_sc = off_pool + POOL_WIDTH
    off_gate = off_sc + 3 * SC_WIDTH

    n_c = -(-(bn + 1) // SUBLANES) * SUBLANES
    cs = jnp.concatenate([c, c_ctx[None], jnp.zeros((n_c - bn - 1, d), F32)], axis=0)
    mod = _ada_call(cs, w_ada, b_ada)

    s0 = jnp.zeros((bn, 2, DN_HEADS, DN_HEAD_DIM, DN_HEAD_DIM), F32)
    icnt_lat = _pool_inv_counts(rows, GRID_W)
    icnt_ctx = _pool_inv_counts(ctx.shape[1], 1)
    final_g = final_norm_g.reshape(1, d)
    w_in16 = w_in.astype(BF16)
    for l in range(depth):
        wl = w_in16[l]
        w_qkv = wl[:, :off_z]
        w_z = wl[:, off_z:off_a]
        w_ab = jnp.pad(wl[:, off_a:off_pool], ((0, 0), (0, LANES - 4 * DN_HEADS)))
        w_pool = wl[:, off_pool:off_sc]
        w_sc = wl[:, off_sc:off_gate]
        w_gate = wl[:, off_gate:]
        alog = jnp.pad(dn_a_log[l].reshape(1, -1), ((0, 0), (0, LANES - 2 * DN_HEADS)))
        dtb = jnp.pad(dn_dt_bias[l].reshape(1, -1), ((0, 0), (0, LANES - 2 * DN_HEADS)))
        n1 = norm1_g[l].reshape(1, d)
        n2 = norm2_g[l].reshape(1, d)
        mix_consts = (w_z, w_sc, w_gate, dn_norm_g[l].reshape(1, -1),
                      _block_diag(pool_w[l]).astype(BF16), pool_scale[l].reshape(1, -1), sc_conv_w[l],
                      w_br_a[l].astype(BF16), w_br_b[l].astype(BF16), w_br_c[l].astype(BF16),
                      w_o[l].astype(BF16))
        wgu = w_gu[l].astype(BF16)
        wdn = w_down[l].astype(BF16)
        lat = [mod[l, :bn, j * d:(j + 1) * d][:, None, :] for j in range(6)]
        cxm = [jnp.broadcast_to(mod[l, bn:bn + 1, j * d:(j + 1) * d][None], (bn, 1, d)) for j in range(6)]

        qkv, gb, pin = _proj_call(ctx, cxm[0], cxm[1], n1, w_qkv, w_ab, w_pool, dn_conv_w[l], alog, dtb)
        o_f, o_b, s_ctx = _scan_call(qkv, gb, s0)
        if l < depth - 1:
            ctx = _mix_call(ctx, cxm[0], cxm[1], cxm[2], n1, o_f, o_b, pin, icnt_ctx, 1, mix_consts)
            ctx = _ffn_call(ctx, cxm[3], cxm[4], cxm[5], n2, wgu, wdn, final_g, False)

        qkv, gb, pin = _proj_call(x, lat[0], lat[1], n1, w_qkv, w_ab, w_pool, dn_conv_w[l], alog, dtb)
        o_f, o_b, _ = _scan_call(qkv, gb, s_ctx)
        x = _mix_call(x, lat[0], lat[1], lat[2], n1, o_f, o_b, pin, icnt_lat, GRID_W, mix_consts)
        x = _ffn_call(x, lat[3], lat[4], lat[5], n2, wgu, wdn, final_g, l == depth - 1)
    return x
```

```python
import functools
import math

import jax
import jax.numpy as jnp
import numpy as np
from jax import lax
from jax.experimental import pallas as pl
from jax.experimental.pallas import tpu as pltpu

F32 = jnp.float32
BF16 = jnp.bfloat16

EPS = 1e-6
GRID_W = 64
DN_HEADS = 4
DN_HEAD_DIM = 128
DN_WIDTH = DN_HEADS * DN_HEAD_DIM
POOL_WINDOWS = (2, 4, 8, 16)
POOL_GROUP_DIM = 64
POOL_WIDTH = POOL_GROUP_DIM * len(POOL_WINDOWS)
SC_WIDTH = 256
N_BRANCH = 3

LANES = 128
SUBLANES = 8
BF16_ROWS = 16
SCAN_CHUNK = 128
SCAN_BLOCK = 128
TOKEN_TILE = 512
PROJ_TILE = 1024
MIX_TILE = 1024
FFN_SUBTILES = 2
MIX_GROUP_ROWS = 256
VMEM_LIMIT = 56 * 1024 * 1024
NEG_BIG = -1e30


def _dot(a, b):
    return jnp.dot(a.astype(BF16), b.astype(BF16), preferred_element_type=F32)


def _dot_nt(a, b):
    return lax.dot_general(a.astype(BF16), b.astype(BF16), (((1,), (1,)), ((), ())),
                           preferred_element_type=F32)


def _dot_tn(a, b):
    return lax.dot_general(a.astype(BF16), b.astype(BF16), (((0,), (0,)), ((), ())),
                           preferred_element_type=F32)


def _sigmoid(x):
    return 0.5 + 0.5 * jnp.tanh(0.5 * x)


def _silu_of_half(hx):
    return hx + hx * jnp.tanh(hx)


def _silu(x):
    return _silu_of_half(0.5 * x)


def _rms(x):
    return x * lax.rsqrt(jnp.mean(x * x, axis=-1, keepdims=True) + EPS)


def _modulate(x, g, shift, scale):
    return _rms(x) * (g * (1.0 + scale)) + shift


def _halo_modulate(x_ref, xp_ref, xn_ref, g, shift, scale):
    x_ext = jnp.concatenate([xp_ref[0], x_ref[0], xn_ref[0]], axis=0)
    return _modulate(x_ext, g, shift, scale)


def _conv3_ext(p_ext, w_ref, keep_prev, keep_next):
    n = p_ext.shape[0] - 2 * SUBLANES
    pe = jnp.concatenate([p_ext[:SUBLANES] * keep_prev, p_ext[SUBLANES:SUBLANES + n],
                          p_ext[SUBLANES + n:] * keep_next], axis=0)
    return (pe[SUBLANES - 1:SUBLANES - 1 + n] * w_ref[0:1] + pe[SUBLANES:SUBLANES + n] * w_ref[1:2]
            + pe[SUBLANES + 1:SUBLANES + 1 + n] * w_ref[2:3])


def _const_spec(shape):
    nd = len(shape)
    return pl.BlockSpec(shape, lambda *_: (0,) * nd, pipeline_mode=pl.Buffered(1))


def _params(sem):
    return pltpu.CompilerParams(dimension_semantics=sem, vmem_limit_bytes=VMEM_LIMIT)


def _ada_kernel(c_ref, w_ref, b_ref, o_ref):
    a = _silu(c_ref[...])
    w = w_ref[0]
    a_hi, w_hi = a.astype(BF16), w.astype(BF16)
    a_lo = (a - a_hi.astype(F32)).astype(BF16)
    w_lo = (w - w_hi.astype(F32)).astype(BF16)
    mm = functools.partial(jnp.dot, preferred_element_type=F32)
    o_ref[0] = (mm(a_hi, w_hi) + (mm(a_lo, w_hi) + mm(a_hi, w_lo))) + b_ref[0]


def _ada_call(cs, w_ada, b_ada):
    nl, d, n6 = w_ada.shape
    rows = cs.shape[0]
    tn = n6 // 4
    return pl.pallas_call(
        _ada_kernel,
        grid=(nl, n6 // tn),
        in_specs=[pl.BlockSpec((rows, d), lambda l, j: (0, 0)),
                  pl.BlockSpec((1, d, tn), lambda l, j: (l, 0, j)),
                  pl.BlockSpec((1, 1, tn), lambda l, j: (l, 0, j))],
        out_specs=pl.BlockSpec((1, rows, tn), lambda l, j: (l, 0, j)),
        out_shape=jax.ShapeDtypeStruct((nl, rows, n6), F32),
        compiler_params=_params(("parallel", "parallel")),
        name="ada",
    )(cs, w_ada, b_ada.reshape(nl, 1, n6))


def _proj_kernel(x_ref, xp_ref, xn_ref, sh_ref, sc_ref, g_ref, wqkv_ref, wab_ref, wpool_ref,
                 cw_ref, alog_ref, dtb_ref, qkv_ref, gb_ref, pin_ref, pbuf_ref, abuf_ref):
    i = pl.program_id(1)
    last = pl.num_programs(1) - 1
    tm = x_ref.shape[1]
    m = tm // SUBLANES
    h_ext = _halo_modulate(x_ref, xp_ref, xn_ref, g_ref[...], sh_ref[0], sc_ref[0])
    hb_ext = h_ext.astype(BF16)
    ncol = 3 * DN_WIDTH // LANES
    keep_prev, keep_next = jnp.where(i > 0, 1.0, 0.0), jnp.where(i < last, 1.0, 0.0)
    for j in range(ncol):
        cols = slice(j * LANES, (j + 1) * LANES)
        if j % 2 == 0:
            p2 = _mm(hb_ext, wqkv_ref[:, j * LANES:(j + 2) * LANES])
            pbuf_ref[j] = p2[:, :LANES]
            pbuf_ref[j + 1] = p2[:, LANES:]
        w0, w1, w2 = 0.5 * cw_ref[0:1, cols], 0.5 * cw_ref[1:2, cols], 0.5 * cw_ref[2:3, cols]
        blk = [pbuf_ref[j, pl.ds(SUBLANES + r, m, stride=SUBLANES), :] for r in range(SUBLANES)]
        before = pbuf_ref[j, SUBLANES - 1:SUBLANES, :] * keep_prev
        after = pbuf_ref[j, SUBLANES + tm:SUBLANES + tm + 1, :] * keep_next
        prev0 = jnp.concatenate([before, blk[SUBLANES - 1][:m - 1]], axis=0)
        next7 = jnp.concatenate([blk[0][1:], after], axis=0)
        for r in range(SUBLANES):
            prev = blk[r - 1] if r else prev0
            nxt = blk[r + 1] if r < SUBLANES - 1 else next7
            a = _silu_of_half(prev * w0 + blk[r] * w1 + nxt * w2)
            if j < 2 * DN_HEADS:
                a = a * lax.rsqrt(jnp.sum(a * a, axis=-1, keepdims=True) + EPS)
            if j < DN_HEADS:
                a = a * (DN_HEAD_DIM ** -0.5)
            abuf_ref[j, pl.ds(r, m, stride=SUBLANES), :] = a
    for j in range(ncol):
        qkv_ref[0, :, j * LANES:(j + 1) * LANES] = abuf_ref[j].astype(BF16)
    pab = _mm(hb_ext, wab_ref[...])[SUBLANES:SUBLANES + tm]
    z = pab + dtb_ref[...]
    softplus = jnp.maximum(z, 0.0) + jnp.log(1.0 + jnp.exp(-jnp.abs(z)))
    gdec = -jnp.exp(alog_ref[...]) * softplus
    lane = lax.broadcasted_iota(jnp.int32, pab.shape, 1)
    gb_ref[0] = jnp.where(lane < 2 * DN_HEADS, gdec,
                          jnp.where(lane < 4 * DN_HEADS, _sigmoid(pab), 0.0))
    pin_ref[0] = _mm(hb_ext, wpool_ref[...])[SUBLANES:SUBLANES + tm]


def _proj_call(x, shift, scale, norm_g, w_qkv, w_ab, w_pool, conv_w, alog, dtb):
    b, s, d = x.shape
    tm = min(PROJ_TILE, s)
    nt = s // tm
    r8 = tm // SUBLANES
    nb8 = s // SUBLANES
    tok = lambda n: pl.BlockSpec((1, tm, n), lambda bi, i: (bi, i, 0))
    vec = pl.BlockSpec((1, 1, d), lambda bi, i: (bi, 0, 0))
    return pl.pallas_call(
        _proj_kernel,
        grid=(b, nt),
        in_specs=[tok(d),
                  pl.BlockSpec((1, SUBLANES, d), lambda bi, i: (bi, jnp.maximum(i * r8 - 1, 0), 0)),
                  pl.BlockSpec((1, SUBLANES, d), lambda bi, i: (bi, jnp.minimum((i + 1) * r8, nb8 - 1), 0)),
                  vec, vec, _const_spec((1, d)),
                  _const_spec(w_qkv.shape), _const_spec(w_ab.shape), _const_spec(w_pool.shape),
                  _const_spec(conv_w.shape), _const_spec(alog.shape), _const_spec(dtb.shape)],
        out_specs=[tok(3 * DN_WIDTH), tok(LANES), tok(POOL_WIDTH)],
        out_shape=[jax.ShapeDtypeStruct((b, s, 3 * DN_WIDTH), BF16),
                   jax.ShapeDtypeStruct((b, s, LANES), F32), jax.ShapeDtypeStruct((b, s, POOL_WIDTH), F32)],
        scratch_shapes=[pltpu.VMEM((3 * DN_WIDTH // LANES, tm + 2 * SUBLANES, LANES), F32),
                        pltpu.VMEM((3 * DN_WIDTH // LANES, tm, LANES), F32)],
        compiler_params=_params(("parallel", "parallel")),
        name="proj_in",
    )(x, x, x, shift, scale, norm_g, w_qkv, w_ab, w_pool, conv_w, alog, dtb)


def _prefix_sum_rows(x):
    n = x.shape[0]
    r = lax.broadcasted_iota(jnp.int32, x.shape, 0)
    s = 1
    while s < n:
        x = x + jnp.where(r >= s, pltpu.roll(x, s, axis=0), 0.0)
        s *= 2
    return x


def _mm(a, b):
    return jnp.dot(a, b, preferred_element_type=F32)


def _scan_kernel(qkvf_ref, gf_ref, qkvb_ref, gbk_ref, s0_ref, of_ref, ob_ref, s_ref, msk_ref, tri_ref, *, c):
    nlev = msk_ref.shape[0] - 1
    i = pl.program_id(1)

    @pl.when(i == 0)
    def _():
        s_ref[...] = s0_ref[...]
        row = lax.broadcasted_iota(jnp.int32, (c, c), 0)
        col = lax.broadcasted_iota(jnp.int32, (c, c), 1)

        def same(sh):
            return jnp.right_shift(row, sh) == jnp.right_shift(col, sh)

        one = jnp.ones((c, c), F32)
        zero = jnp.zeros((c, c), F32)
        msk_ref[0] = jnp.where(same(1), jnp.where(row == col, zero, one), zero).astype(BF16)
        for j in range(1, nlev):
            msk_ref[j] = jnp.where(same(j + 1), jnp.where(same(j), zero, one), zero).astype(BF16)
        msk_ref[nlev] = jnp.where(row == col, one, zero).astype(BF16)
        tri_ref[0] = jnp.where(row >= col, 0.0, NEG_BIG)
        tri_ref[1] = jnp.where(row <= col, 0.0, NEG_BIG)

    nbat, nblk = qkvf_ref.shape[0], qkvf_ref.shape[1] // c
    units = []
    for d, (qkv_ref, gb_ref, o_ref) in enumerate(((qkvf_ref, gf_ref, of_ref), (qkvb_ref, gbk_ref, ob_ref))):
        for bx, g in [(bx, g) for bx in range(nbat) for g in range(nblk)]:
            rs = slice(g * c, (g + 1) * c)
            gb = gb_ref[bx, rs, :]
            cs = _prefix_sum_rows(gb)
            tot = cs[c - 1:c, :]
            if d == 1:
                cs = tot - cs + gb
            cs_t = cs.T
            for h in range(DN_HEADS):
                gi = d * DN_HEADS + h
                bi = 2 * DN_HEADS + gi
                lo = h * DN_HEAD_DIM
                hs = slice(lo, lo + DN_HEAD_DIM)
                q16 = qkv_ref[bx, rs, hs]
                k16 = qkv_ref[bx, rs, DN_WIDTH + lo:DN_WIDTH + lo + DN_HEAD_DIM]
                v16 = qkv_ref[bx, rs, 2 * DN_WIDTH + lo:2 * DN_WIDTH + lo + DN_HEAD_DIM]
                units.append(dict(bx=bx, d=d, h=h, g=g, rs=rs, hs=hs, o_ref=o_ref, gc=cs[:, gi:gi + 1],
                                  gc_row=cs_t[gi:gi + 1, :], gl=tot[:, gi:gi + 1], beta=gb[:, bi:bi + 1],
                                  q16=q16, k16=k16, q=q16.astype(F32), k=k16.astype(F32),
                                  v=v16.astype(F32)))
    nu = range(len(units))
    dec = [jnp.exp((u["gc"] - u["gc_row"]) + tri_ref[u["d"]]) for u in units]
    egc = [jnp.exp(u["gc"]) for u in units]
    k16 = [u["k16"] for u in units]
    kbeta = [u["k"] * u["beta"] for u in units]
    kq = [_dot_nt(jnp.concatenate([kbeta[j].astype(BF16), units[j]["q16"]], axis=0), k16[j])
          for j in nu]
    a16 = [(kq[j][:c] * dec[j]).astype(BF16) for j in nu]
    aqk16 = [(kq[j][c:] * dec[j]).astype(BF16) for j in nu]
    eye16 = msk_ref[nlev]
    t16 = [eye16 - a16[j] * msk_ref[0] for j in nu]
    for lev in range(1, nlev):
        b = 2 ** lev
        if b < BF16_ROWS:
            p16 = [_mm(t16[j], a16[j] * msk_ref[lev]).astype(BF16) for j in nu]
            t16 = [t16[j] - _mm(p16[j], t16[j]).astype(BF16) for j in nu]
            continue
        upd = [[r0 for r0 in range(0, c, b) if (r0 // b) % 2 == 1 - units[j]["d"]] for j in nu]
        t_sel = [jnp.concatenate([t16[j][r0:r0 + b] for r0 in upd[j]], axis=0) for j in nu]
        p16 = [_mm(t_sel[j], a16[j] * msk_ref[lev]).astype(BF16) for j in nu]
        t_new = [t_sel[j] - _mm(p16[j], t16[j]).astype(BF16) for j in nu]
        t16 = [jnp.concatenate([t_new[j][upd[j].index(r0) * b:(upd[j].index(r0) + 1) * b] if r0 in upd[j]
                                else t16[j][r0:r0 + b] for r0 in range(0, c, b)], axis=0) for j in nu]
    e16 = [t16[j] - eye16 for j in nu]
    vbeta = [units[j]["v"] * units[j]["beta"] for j in nu]
    kd16 = [(kbeta[j] * egc[j]).astype(BF16) for j in nu]
    qd16 = [(units[j]["q"] * egc[j]).astype(BF16) for j in nu]
    kst16 = [(units[j]["k"] * jnp.exp(units[j]["gl"] - units[j]["gc"])).astype(BF16) for j in nu]
    state = {(bx, d, h): s_ref[bx, d, h] for bx in range(nbat) for d in range(2) for h in range(DN_HEADS)}
    for step in range(nblk):
        cur = [j for j in nu if units[j]["g"] == (step if units[j]["d"] == 0 else nblk - 1 - step)]
        s16 = {j: state[units[j]["bx"], units[j]["d"], units[j]["h"]].astype(BF16) for j in cur}
        ws = {j: _mm(jnp.concatenate([kd16[j], qd16[j]], axis=0), s16[j]) for j in cur}
        resid = {j: vbeta[j] - ws[j][:c] for j in cur}
        vn16 = {j: (resid[j] + _mm(e16[j], resid[j].astype(BF16))).astype(BF16) for j in cur}
        for j in cur:
            u = units[j]
            u["o_ref"][u["bx"], u["rs"], u["hs"]] = (ws[j][c:] + _mm(aqk16[j], vn16[j])).astype(BF16)
        for j in cur:
            u = units[j]
            key = (u["bx"], u["d"], u["h"])
            state[key] = (state[key] * jnp.exp(u["gl"])
                          + lax.dot_general(kst16[j], vn16[j], (((0,), (0,)), ((), ())),
                                            preferred_element_type=F32))
    for (bx, d, h), val in state.items():
        s_ref[bx, d, h] = val


def _scan_call(qkv, gb, s0):
    b, s, _ = qkv.shape
    c = min(SCAN_CHUNK, s)
    blk = min(SCAN_BLOCK, s)
    nb = s // blk
    nlev = int(math.log2(c))
    fwd = lambda n: pl.BlockSpec((b, blk, n), lambda bi, i: (bi, i, 0))
    bwd = lambda n: pl.BlockSpec((b, blk, n), lambda bi, i: (bi, nb - 1 - i, 0))
    st = pl.BlockSpec((b, 2, DN_HEADS, DN_HEAD_DIM, DN_HEAD_DIM), lambda bi, i: (bi, 0, 0, 0, 0))
    return pl.pallas_call(
        functools.partial(_scan_kernel, c=c),
        grid=(1, nb),
        in_specs=[fwd(3 * DN_WIDTH), fwd(LANES), bwd(3 * DN_WIDTH), bwd(LANES), st],
        out_specs=[fwd(DN_WIDTH), bwd(DN_WIDTH), st],
        out_shape=[jax.ShapeDtypeStruct((b, s, DN_WIDTH), BF16)] * 2
        + [jax.ShapeDtypeStruct(s0.shape, F32)],
        scratch_shapes=[pltpu.VMEM((nlev + 1, c, c), BF16), pltpu.VMEM((2, c, c), F32)],
        compiler_params=_params(("parallel", "arbitrary")),
        name="scan",
    )(qkv, gb, qkv, gb, s0)


def _window_offsets(w):
    return -(w // 2), w - 1 - (w // 2)


def _pool_inv_counts(rows, width):
    t = np.arange(rows * width, dtype=np.int64)
    r, c = t // width, t % width
    cols = []
    for w in POOL_WINDOWS:
        lo, hi = _window_offsets(w)
        cnt_r = np.minimum(r + hi + 1, rows) - np.maximum(r + lo, 0)
        cnt_c = np.minimum(c + hi + 1, width) - np.maximum(c + lo, 0)
        inv = (1.0 / (cnt_r * cnt_c)).astype(np.float32)
        cols.append(np.broadcast_to(inv[:, None], (rows * width, POOL_GROUP_DIM)))
    return jnp.asarray(np.concatenate(cols, axis=1))


def _window_sums(x, unit, lev_small):
    n = x.shape[0]

    def shifted(a, k):
        return pltpu.roll(a, (k * unit) % n, axis=0)

    sums = {1: x + shifted(x, 1)}
    for lev in range(2, lev_small + 2):
        step = 2 ** (lev - 2)
        sums[lev] = shifted(sums[lev - 1], step) + shifted(sums[lev - 1], -step)
    return sums[lev_small], sums[lev_small + 1]


def _pool_tile(ext, inv_cnt, width):
    halo = SUBLANES * width
    tm = ext.shape[0] - 2 * halo
    pad = SUBLANES
    outs = []
    for slab in range(POOL_WIDTH // LANES):
        lanes = slice(slab * LANES, (slab + 1) * LANES)
        e = ext[:, lanes]
        small_lane = lax.broadcasted_iota(jnp.int32, (tm, LANES), 1) < POOL_GROUP_DIM
        levels = int(math.log2(POOL_WINDOWS[2 * slab]))
        rs, rl = _window_sums(e, width, levels)
        m1 = jnp.where(small_lane, rs[halo:halo + tm], rl[halo:halo + tm])
        if width > 1:
            zeros = jnp.zeros((pad, LANES), F32)
            pieces = []
            for r in range(tm // width):
                pieces += [zeros, m1[r * width:(r + 1) * width], zeros]
            cs, cl = _window_sums(jnp.concatenate(pieces, axis=0), 1, levels)
            stride = width + 2 * pad
            pick = lambda a: jnp.concatenate(
                [a[r * stride + pad:r * stride + pad + width] for r in range(tm // width)], axis=0)
            m1 = jnp.where(small_lane, pick(cs), pick(cl))
        outs.append(m1 * inv_cnt[:, lanes] - e[halo:halo + tm])
    return jnp.concatenate(outs, axis=1)


def _mix_kernel(x_ref, xp_ref, xn_ref, sh_ref, sc_ref, gt_ref, g_ref, of_ref, ob_ref,
                pin_ref, pinp_ref, pinn_ref, icnt_ref,
                wz_ref, wsc_ref, wgate_ref, dng_ref, wpool_ref, pscale_ref, scw_ref,
                wa_ref, wb_ref, wc_ref, wo_ref, out_ref, *, width):
    i = pl.program_id(1)
    last = pl.num_programs(1) - 1
    tm, d = x_ref.shape[1], x_ref.shape[2]
    h_ext = _halo_modulate(x_ref, xp_ref, xn_ref, g_ref[...], sh_ref[0], sc_ref[0])
    keep_prev, keep_next = jnp.where(i > 0, 1.0, 0.0), jnp.where(i < last, 1.0, 0.0)
    pool_ext = jnp.concatenate([pinp_ref[0] * keep_prev, pin_ref[0], pinn_ref[0] * keep_next], axis=0)
    pd = _pool_tile(pool_ext, icnt_ref[...], width)
    psc = _dot(h_ext, wsc_ref[...])
    conv = _conv3_ext(psc[:, 2 * SC_WIDTH:] * psc[:, :SC_WIDTH], scw_ref, keep_prev, keep_next)
    yc_in = psc[SUBLANES:SUBLANES + tm, SC_WIDTH:2 * SC_WIDTH] * conv
    sub = min(MIX_GROUP_ROWS, tm)
    for t in range(tm // sub):
        rows = slice(t * sub, (t + 1) * sub)
        hb = h_ext[SUBLANES + t * sub:SUBLANES + (t + 1) * sub].astype(BF16)
        z = _dot(hb, wz_ref[...])
        o = of_ref[0, rows, :].astype(F32) + ob_ref[0, rows, :].astype(F32)
        parts = []
        for h in range(DN_HEADS):
            lo = h * DN_HEAD_DIM
            parts.append(_rms(o[:, lo:lo + DN_HEAD_DIM]) * dng_ref[...] * _silu(z[:, lo:lo + DN_HEAD_DIM]))
        y_a = _dot(jnp.concatenate(parts, axis=1), wa_ref[...])
        y_b = _dot(_dot(pd[rows], wpool_ref[...]) * pscale_ref[...], wb_ref[...])
        y_c = _dot(yc_in[rows], wc_ref[...])
        gates = _sigmoid(_dot(hb, wgate_ref[...]))
        y = gates[:, :d] * y_a + gates[:, d:2 * d] * y_b + gates[:, 2 * d:] * y_c
        out_ref[0, rows, :] = x_ref[0, rows, :] + gt_ref[0] * _dot(y, wo_ref[...])


def _mix_call(x, shift, scale, gate, norm_g, o_f, o_b, pin, inv_cnt, width, consts):
    b, s, d = x.shape
    tm = min(MIX_TILE, s)
    nt = s // tm
    r8 = tm // SUBLANES
    nb8 = s // SUBLANES
    halo = SUBLANES * width
    assert tm % halo == 0 and s % tm == 0
    hb, nh = tm // halo, s // halo
    tok = lambda n: pl.BlockSpec((1, tm, n), lambda bi, i: (bi, i, 0))
    vec = pl.BlockSpec((1, 1, d), lambda bi, i: (bi, 0, 0))
    return pl.pallas_call(
        functools.partial(_mix_kernel, width=width),
        grid=(b, nt),
        in_specs=[tok(d),
                  pl.BlockSpec((1, SUBLANES, d), lambda bi, i: (bi, jnp.maximum(i * r8 - 1, 0), 0)),
                  pl.BlockSpec((1, SUBLANES, d), lambda bi, i: (bi, jnp.minimum((i + 1) * r8, nb8 - 1), 0)),
                  vec, vec, vec, _const_spec((1, d)),
                  tok(DN_WIDTH), tok(DN_WIDTH), tok(POOL_WIDTH),
                  pl.BlockSpec((1, halo, POOL_WIDTH), lambda bi, i: (bi, jnp.maximum(i * hb - 1, 0), 0)),
                  pl.BlockSpec((1, halo, POOL_WIDTH), lambda bi, i: (bi, jnp.minimum((i + 1) * hb, nh - 1), 0)),
                  pl.BlockSpec((tm, POOL_WIDTH), lambda bi, i: (i, 0))]
        + [_const_spec(w.shape) for w in consts],
        out_specs=tok(d),
        out_shape=jax.ShapeDtypeStruct((b, s, d), F32),
        compiler_params=_params(("parallel", "parallel")),
        name="mix",
    )(x, x, x, shift, scale, gate, norm_g, o_f, o_b, pin, pin, pin, inv_cnt, *consts)


def _ffn_kernel(x_ref, sh_ref, sc_ref, gt_ref, g_ref, wgu_ref, wdown_ref, gf_ref, out_ref, *, final):
    dff = wdown_ref.shape[0]
    tm = x_ref.shape[1]
    sub = tm // FFN_SUBTILES
    for t in range(FFN_SUBTILES):
        rows = slice(t * sub, (t + 1) * sub)
        xm = x_ref[0, rows, :]
        hb = _modulate(xm, g_ref[...], sh_ref[0], sc_ref[0]).astype(BF16)
        gu = _dot(hb, wgu_ref[...])
        act = _silu(gu[:, :dff]) * gu[:, dff:]
        r = xm + gt_ref[0] * _dot(act, wdown_ref[...])
        if final:
            r = _rms(r) * gf_ref[...]
        out_ref[0, rows, :] = r


def _ffn_call(x, shift, scale, gate, norm_g, w_gu, w_down, final_g, final):
    b, s, d = x.shape
    tm = min(TOKEN_TILE, s)
    tok = pl.BlockSpec((1, tm, d), lambda bi, i: (bi, i, 0))
    vec = pl.BlockSpec((1, 1, d), lambda bi, i: (bi, 0, 0))
    return pl.pallas_call(
        functools.partial(_ffn_kernel, final=final),
        grid=(b, s // tm),
        in_specs=[tok, vec, vec, vec, _const_spec((1, d)), _const_spec(w_gu.shape),
                  _const_spec(w_down.shape), _const_spec((1, d))],
        out_specs=tok,
        out_shape=jax.ShapeDtypeStruct((b, s, d), F32),
        compiler_params=_params(("parallel", "parallel")),
        name="ffn",
    )(x, shift, scale, gate, norm_g, w_gu, w_down, final_g)


def _block_diag(w):
    g, ci, co = w.shape
    out = jnp.zeros((g * ci, g * co), w.dtype)
    for j in range(g):
        out = out.at[j * ci:(j + 1) * ci, j * co:(j + 1) * co].set(w[j])
    return out


def kernel(x, c, ctx, c_ctx, w_ada, b_ada, norm1_g, norm2_g, w_in, dn_conv_w, dn_a_log, dn_dt_bias,
           dn_norm_g, pool_w, pool_scale, sc_conv_w, w_br_a, w_br_b, w_br_c, w_o, w_gu, w_down,
           final_norm_g):
    bn, seq, d = x.shape
    depth = w_ada.shape[0]
    rows = seq // GRID_W
    off_z = 3 * DN_WIDTH
    off_a = off_z + DN_WIDTH
    off_pool = off_a + 4 * DN_HEADS
    off_sc = off_pool + POOL_WIDTH
    off_gate = off_sc + 3 * SC_WIDTH

    n_c = -(-(bn + 1) // SUBLANES) * SUBLANES
    cs = jnp.concatenate([c, c_ctx[None], jnp.zeros((n_c - bn - 1, d), F32)], axis=0)
    mod = _ada_call(cs, w_ada, b_ada)

    s0 = jnp.zeros((bn, 2, DN_HEADS, DN_HEAD_DIM, DN_HEAD_DIM), F32)
    icnt_lat = _pool_inv_counts(rows, GRID_W)
    icnt_ctx = _pool_inv_counts(ctx.shape[1], 1)
    final_g = final_norm_g.reshape(1, d)
    w_in16 = w_in.astype(BF16)
    for l in range(depth):
        wl = w_in16[l]
        w_qkv = wl[:, :off_z]
        w_z = wl[:, off_z:off_a]
        w_ab = jnp.pad(wl[:, off_a:off_pool], ((0, 0), (0, LANES - 4 * DN_HEADS)))
        w_pool = wl[:, off_pool:off_sc]
        w_sc = wl[:, off_sc:off_gate]
        w_gate = wl[:, off_gate:]
        alog = jnp.pad(dn_a_log[l].reshape(1, -1), ((0, 0), (0, LANES - 2 * DN_HEADS)))
        dtb = jnp.pad(dn_dt_bias[l].reshape(1, -1), ((0, 0), (0, LANES - 2 * DN_HEADS)))
        n1 = norm1_g[l].reshape(1, d)
        n2 = norm2_g[l].reshape(1, d)
        mix_consts = (w_z, w_sc, w_gate, dn_norm_g[l].reshape(1, -1),
                      _block_diag(pool_w[l]).astype(BF16), pool_scale[l].reshape(1, -1), sc_conv_w[l],
                      w_br_a[l].astype(BF16), w_br_b[l].astype(BF16), w_br_c[l].astype(BF16),
                      w_o[l].astype(BF16))
        wgu = w_gu[l].astype(BF16)
        wdn = w_down[l].astype(BF16)
        lat = [mod[l, :bn, j * d:(j + 1) * d][:, None, :] for j in range(6)]
        cxm = [jnp.broadcast_to(mod[l, bn:bn + 1, j * d:(j + 1) * d][None], (bn, 1, d)) for j in range(6)]

        qkv, gb, pin = _proj_call(ctx, cxm[0], cxm[1], n1, w_qkv, w_ab, w_pool, dn_conv_w[l], alog, dtb)
        o_f, o_b, s_ctx = _scan_call(qkv, gb, s0)
        if l < depth - 1:
            ctx = _mix_call(ctx, cxm[0], cxm[1], cxm[2], n1, o_f, o_b, pin, icnt_ctx, 1, mix_consts)
            ctx = _ffn_call(ctx, cxm[3], cxm[4], cxm[5], n2, wgu, wdn, final_g, False)

        qkv, gb, pin = _proj_call(x, lat[0], lat[1], n1, w_qkv, w_ab, w_pool, dn_conv_w[l], alog, dtb)
        o_f, o_b, _ = _scan_call(qkv, gb, s_ctx)
        x = _mix_call(x, lat[0], lat[1], lat[2], n1, o_f, o_b, pin, icnt_lat, GRID_W, mix_consts)
        x = _ffn_call(x, lat[3], lat[4], lat[5], n2, wgu, wdn, final_g, l == depth - 1)
    return x
```
